```python
import math
import jax, jax.numpy as jnp
from jax import lax
import numpy as np

D_MODEL = 1024
BATCH = 4
SEQ = 4096
DEPTH = 2

DN_ALPHA = (2.0 * DEPTH) ** 0.25
DN_BETA = (8.0 * DEPTH) ** -0.25
LN_EPS = 1e-5
NEG_INF = -1e30

D_FF = 256 * ((8 * D_MODEL // 3 + 255) // 256)
MACARON_WEIGHT = 0.5

CONV_CH = D_MODEL
CONV_WIDTH = 31

SSM_D_INNER = D_MODEL
SSM_HEAD_DIM = 64
SSM_HEADS = SSM_D_INNER // SSM_HEAD_DIM
SSM_GROUPS = 4
SSM_HEADS_PER_GROUP = SSM_HEADS // SSM_GROUPS
SSM_STATE = 128
SSM_CONV_WIDTH = 4
SSM_CHUNK = 256
SSM_CONV_DIM = SSM_D_INNER + 2 * SSM_GROUPS * SSM_STATE

NSA_HEAD_DIM = 64
NSA_HEADS = D_MODEL // NSA_HEAD_DIM
NSA_KV_HEADS = 4
NSA_GQA = NSA_HEADS // NSA_KV_HEADS
CMP_LEN = 32
CMP_STRIDE = 16
CMP_HIDDEN = 2 * NSA_HEAD_DIM
SEL_BLOCK = 64
N_SELECT = 16
WINDOW = 512
Q_BLOCK = 64
FORCED_SCORE = 1e4
ROPE_THETA = 10000.0

N_BRANCH = 3
A_IN = 2 * CONV_CH
B_IN = 2 * SSM_D_INNER + 2 * SSM_GROUPS * SSM_STATE + SSM_HEADS
C_IN = NSA_HEADS * NSA_HEAD_DIM + 6 * NSA_KV_HEADS * NSA_HEAD_DIM + 3 * NSA_HEADS
GATE_IN = N_BRANCH * D_MODEL
IN_COLS = A_IN + B_IN + C_IN + GATE_IN

kernel_name = 'hybrid_conv_ssd_nsa_macaron_deepnorm'


def _split(x, sizes):
    return jnp.split(x, [int(s) for s in np.cumsum(sizes)[:-1]], axis=-1)


def layer_norm(x, g, b):
    x32 = x.astype(jnp.float32)
    mu = jnp.mean(x32, -1, keepdims=True)
    var = jnp.mean(jnp.square(x32 - mu), -1, keepdims=True)
    return ((x32 - mu) * lax.rsqrt(var + LN_EPS) * g + b).astype(x.dtype)


def rms_norm(x, g):
    x32 = x.astype(jnp.float32)
    return x32 * lax.rsqrt(jnp.mean(jnp.square(x32), -1, keepdims=True) + LN_EPS) * g


def causal_dwconv(x, w, b):
    width, ch = w.shape
    y = lax.conv_general_dilated(x, w[:, None, :].astype(x.dtype), (1,), [(width - 1, 0)],
                                 dimension_numbers=('NWC', 'WIO', 'NWC'), feature_group_count=ch)
    return y + b


def rope(x, pos):
    half = x.shape[-1] // 2
    inv_freq = ROPE_THETA ** (-jnp.arange(half, dtype=jnp.float32) / half)
    ang = pos.astype(jnp.float32)[:, None] * inv_freq[None, :]
    cos = jnp.cos(ang)[:, None, :]
    sin = jnp.sin(ang)[:, None, :]
    x32 = x.astype(jnp.float32)
    x1, x2 = x32[..., :half], x32[..., half:]
    return jnp.concatenate([x1 * cos - x2 * sin, x2 * cos + x1 * sin], -1).astype(x.dtype)


def modulate(h, shift, scale):
    return h * (1.0 + scale) + shift


def swiglu_ffn(u, w_gate, w_up, w_down):
    return (jax.nn.silu(u @ w_gate) * (u @ w_up)) @ w_down


def conformer_conv(a_in, conv_w, conv_b, norm_g, norm_b):
    val, gate = _split(a_in, [CONV_CH, CONV_CH])
    a = val * jax.nn.sigmoid(gate)
    a = causal_dwconv(a, conv_w, conv_b)
    return jax.nn.silu(layer_norm(a, norm_g, norm_b))


def ssd_chunked(x, a, b, c):
    bsz, seq, ng, nr, hp = x.shape
    chunk = math.gcd(SSM_CHUNK, seq)
    nc = seq // chunk
    xc = x.reshape(bsz, nc, chunk, ng, nr, hp)
    bc = b.reshape(bsz, nc, chunk, ng, -1)
    cc = c.reshape(bsz, nc, chunk, ng, -1)
    a_cs = jnp.cumsum(a.reshape(bsz, nc, chunk, ng, nr).transpose(0, 3, 4, 1, 2), axis=-1)
    causal = jnp.tril(jnp.ones((chunk, chunk), bool))
    seg = jnp.exp(jnp.where(causal, a_cs[..., :, None] - a_cs[..., None, :], NEG_INF))
    cb = jnp.einsum('bclgn,bcsgn->bgcls', cc, bc)
    y_diag = jnp.einsum('bgrcls,bcsgrp->bclgrp', cb[:, :, None] * seg, xc)
    decay_in = jnp.exp(a_cs[..., -1:] - a_cs)
    chunk_states = jnp.einsum('bclgn,bgrcl,bclgrp->cbgrpn', bc, decay_in, xc)
    chunk_decay = jnp.exp(a_cs[..., -1]).transpose(3, 0, 1, 2)

    def step(h, inp):
        st, dec = inp
        return h * dec[..., None, None] + st, h

    h0 = jnp.zeros(chunk_states.shape[1:], chunk_states.dtype)
    _, h_prev = lax.scan(step, h0, (chunk_states, chunk_decay))
    y_off = jnp.einsum('bclgn,cbgrpn,bgrcl->bclgrp', cc, h_prev, jnp.exp(a_cs))
    return (y_diag + y_off).reshape(bsz, seq, ng, nr, hp)


def mamba2_ssd(ssm_in, conv_w, conv_b, dt_bias, a_log, d_skip, norm_w):
    bsz, seq, _ = ssm_in.shape
    f32 = jnp.float32
    gn = SSM_GROUPS * SSM_STATE
    z, xbc, dt = _split(ssm_in, [SSM_D_INNER, SSM_CONV_DIM, SSM_HEADS])
    xbc = jax.nn.silu(causal_dwconv(xbc, conv_w, conv_b))
    xs, b_in, c_in = _split(xbc, [SSM_D_INNER, gn, gn])
    dt = jax.nn.softplus(dt.astype(f32) + dt_bias.astype(f32)).reshape(bsz, seq, SSM_GROUPS, SSM_HEADS_PER_GROUP)
    a = -jnp.exp(a_log.astype(f32)).reshape(SSM_GROUPS, SSM_HEADS_PER_GROUP)
    xh = xs.astype(f32).reshape(bsz, seq, SSM_GROUPS, SSM_HEADS_PER_GROUP, SSM_HEAD_DIM)
    bm = b_in.astype(f32).reshape(bsz, seq, SSM_GROUPS, SSM_STATE)
    cm = c_in.astype(f32).reshape(bsz, seq, SSM_GROUPS, SSM_STATE)
    y = ssd_chunked(xh * dt[..., None], a * dt, bm, cm)
    y = y + d_skip.astype(f32).reshape(SSM_GROUPS, SSM_HEADS_PER_GROUP, 1) * xh
    y = y.reshape(bsz, seq, SSM_D_INNER) * jax.nn.silu(z.astype(f32))
    return rms_norm(y, norm_w).astype(ssm_in.dtype)


def compress_blocks(kv, pe, w1, w2):
    bsz, seq, nkv, hd = kv.shape
    n_cmp = (seq - CMP_LEN) // CMP_STRIDE + 1
    idx = jnp.arange(n_cmp)[:, None] * CMP_STRIDE + jnp.arange(CMP_LEN)[None, :]
    blocks = kv[:, idx] + pe[:, None, :]
    flat = blocks.transpose(0, 1, 3, 2, 4).reshape(bsz, n_cmp, nkv, CMP_LEN * hd)
    return jax.nn.gelu(flat @ w1) @ w2


def nsa_attention(nsa_in, pe_k, pe_v, k_w1, k_w2, v_w1, v_w2):
    bsz, seq, _ = nsa_in.shape
    dt_ = nsa_in.dtype
    f32 = jnp.float32
    kvd = NSA_KV_HEADS * NSA_HEAD_DIM
    q, k_c, v_c, k_s, v_s, k_w, v_w, gate_logits = _split(nsa_in, [NSA_HEADS * NSA_HEAD_DIM] + [kvd] * 6 + [3 * NSA_HEADS])
    pos = jnp.arange(seq)
    kv_shape = (bsz, seq, NSA_KV_HEADS, NSA_HEAD_DIM)
    q = rope(q.reshape(bsz, seq, NSA_HEADS, NSA_HEAD_DIM), pos).reshape(bsz, seq, NSA_KV_HEADS, NSA_GQA, NSA_HEAD_DIM)
    n_cmp = (seq - CMP_LEN) // CMP_STRIDE + 1
    cmp_end = jnp.arange(n_cmp) * CMP_STRIDE + CMP_LEN - 1
    k_cmp = rope(compress_blocks(k_c.reshape(kv_shape), pe_k, k_w1, k_w2), cmp_end)
    v_cmp = compress_blocks(v_c.reshape(kv_shape), pe_v, v_w1, v_w2)
    n_sb = seq // SEL_BLOCK
    n_sel = min(N_SELECT, n_sb)
    blk_shape = (bsz, n_sb, SEL_BLOCK, NSA_KV_HEADS, NSA_HEAD_DIM)
    k_sel = rope(k_s.reshape(kv_shape), pos).reshape(blk_shape).transpose(0, 3, 1, 2, 4)
    v_sel = v_s.reshape(blk_shape).transpose(0, 3, 1, 2, 4)
    pad = ((0, 0), (WINDOW, 0), (0, 0), (0, 0))
    k_win = jnp.pad(rope(k_w.reshape(kv_shape), pos), pad)
    v_win = jnp.pad(v_w.reshape(kv_shape), pad)
    gates = jax.nn.sigmoid(gate_logits.astype(f32)).astype(dt_).reshape(bsz, seq, NSA_KV_HEADS, NSA_GQA, 3)
    c_start = jnp.arange(n_cmp) * CMP_STRIDE
    s_start = jnp.arange(n_sb) * SEL_BLOCK
    overlap = ((c_start[:, None] < s_start[None, :] + SEL_BLOCK) &
               (c_start[:, None] + CMP_LEN > s_start[None, :])).astype(f32)
    scale = NSA_HEAD_DIM ** -0.5
    b_idx = jnp.arange(bsz)[:, None, None, None]
    g_idx = jnp.arange(NSA_KV_HEADS)[None, :, None, None]
    blk_ids = jnp.arange(n_sb)

    def query_block(i):
        q0 = i * Q_BLOCK
        t = q0 + jnp.arange(Q_BLOCK)
        qb = lax.dynamic_slice_in_dim(q, q0, Q_BLOCK, axis=1)
        s = jnp.einsum('bqgrd,bngd->bgrqn', qb, k_cmp).astype(f32) * scale
        cmask = cmp_end[None, :] <= t[:, None]
        p_cmp = jax.nn.softmax(jnp.where(cmask, s, NEG_INF), -1) * cmask
        o_cmp = jnp.einsum('bgrqn,bngd->bqgrd', p_cmp.astype(dt_), v_cmp)
        imp = jnp.einsum('bgrqn,nj->bgqj', p_cmp, overlap)
        cur = (t // SEL_BLOCK)[:, None]
        forced = (blk_ids[None, :] == 0) | (blk_ids[None, :] == cur) | (blk_ids[None, :] == cur - 1)
        future = blk_ids[None, :] * SEL_BLOCK > t[:, None]
        imp = jnp.where(forced, FORCED_SCORE, jnp.where(future, -1.0, imp))
        _, sel = lax.top_k(imp, n_sel)
        kg = k_sel[b_idx, g_idx, sel]
        vg = v_sel[b_idx, g_idx, sel]
        s = jnp.einsum('bqgrd,bgqkld->bgrqkl', qb, kg).astype(f32) * scale
        kpos = sel[..., None] * SEL_BLOCK + jnp.arange(SEL_BLOCK)
        smask = (kpos <= t[None, None, :, None, None])[:, :, None]
        s = jnp.where(smask, s, NEG_INF).reshape(bsz, NSA_KV_HEADS, NSA_GQA, Q_BLOCK, n_sel * SEL_BLOCK)
        p_sel = jax.nn.softmax(s, -1).reshape(bsz, NSA_KV_HEADS, NSA_GQA, Q_BLOCK, n_sel, SEL_BLOCK)
        o_sel = jnp.einsum('bgrqkl,bgqkld->bqgrd', p_sel.astype(dt_), vg)
        kw = lax.dynamic_slice_in_dim(k_win, q0, WINDOW + Q_BLOCK, axis=1)
        vw = lax.dynamic_slice_in_dim(v_win, q0, WINDOW + Q_BLOCK, axis=1)
        wpos = q0 - WINDOW + jnp.arange(WINDOW + Q_BLOCK)
        wmask = (wpos[None, :] <= t[:, None]) & (wpos[None, :] > t[:, None] - WINDOW) & (wpos[None, :] >= 0)
        s = jnp.einsum('bqgrd,bkgd->bgrqk', qb, kw).astype(f32) * scale
        p_win = jax.nn.softmax(jnp.where(wmask, s, NEG_INF), -1)
        o_win = jnp.einsum('bgrqk,bkgd->bqgrd', p_win.astype(dt_), vw)
        g = lax.dynamic_slice_in_dim(gates, q0, Q_BLOCK, axis=1)
        return g[..., 0:1] * o_cmp + g[..., 1:2] * o_sel + g[..., 2:3] * o_win

    out = lax.map(query_block, jnp.arange(seq // Q_BLOCK))
    return out.transpose(1, 0, 2, 3, 4, 5).reshape(bsz, seq, NSA_HEADS * NSA_HEAD_DIM)


def hybrid_token_mixer(u, w_in, conv_a_w, conv_a_b, norm_a_g, norm_a_b, w_a_out,
                       ssm_conv_w, ssm_conv_b, ssm_dt_bias, ssm_a_log, ssm_d, ssm_norm_w, w_b_out,
                       cmp_pe_k, cmp_pe_v, cmp_k_w1, cmp_k_w2, cmp_v_w1, cmp_v_w2, w_c_out, w_o):
    bsz, seq, _ = u.shape
    proj = jnp.einsum('bsd,de->bse', u, w_in)
    a_in, b_in, c_in, g_in = _split(proj, [A_IN, B_IN, C_IN, GATE_IN])
    y_a = conformer_conv(a_in, conv_a_w, conv_a_b, norm_a_g, norm_a_b) @ w_a_out
    y_b = mamba2_ssd(b_in, ssm_conv_w, ssm_conv_b, ssm_dt_bias, ssm_a_log, ssm_d, ssm_norm_w) @ w_b_out
    y_c = nsa_attention(c_in, cmp_pe_k, cmp_pe_v, cmp_k_w1, cmp_k_w2, cmp_v_w1, cmp_v_w2) @ w_c_out
    g = jax.nn.sigmoid(g_in.astype(jnp.float32)).astype(u.dtype).reshape(bsz, seq, N_BRANCH, D_MODEL)
    merged = g[:, :, 0] * y_a + g[:, :, 1] * y_b + g[:, :, 2] * y_c
    return merged @ w_o


def setup_inputs(seed: int = 0) -> dict:
    key = jax.random.key(seed)
    keys = iter(jax.random.split(key, 48))
    f32 = jnp.float32
    L = DEPTH

    def normal(shape, scale):
        return jax.random.normal(next(keys), shape, f32) * scale

    def uniform(shape, lo, hi):
        return jax.random.uniform(next(keys), shape, f32, lo, hi)

    kvd = NSA_KV_HEADS * NSA_HEAD_DIM
    col_scale = np.concatenate([
        np.ones(A_IN + B_IN + NSA_HEADS * NSA_HEAD_DIM),
        np.tile(np.concatenate([np.ones(kvd), np.full(kvd, DN_BETA)]), 3),
        np.ones(3 * NSA_HEADS + GATE_IN)]).astype(np.float32)
    dt0 = jnp.exp(uniform((L, SSM_HEADS), math.log(1e-3), math.log(1e-1)))
    cmp_in = CMP_LEN * NSA_HEAD_DIM
    return {
        'x': normal((BATCH, SEQ, D_MODEL), 1.0),
        'c': normal((BATCH, D_MODEL), 1.0),
        'ada_w': normal((L, D_MODEL, 9 * D_MODEL), 0.5 * D_MODEL ** -0.5),
        'ada_b': normal((L, 9 * D_MODEL), 0.01),
        'ln_g': 1.0 + normal((L, 3, D_MODEL), 0.02),
        'ln_b': normal((L, 3, D_MODEL), 0.02),
        'ffn_w_gate': normal((L, 2, D_MODEL, D_FF), D_MODEL ** -0.5),
        'ffn_w_up': normal((L, 2, D_MODEL, D_FF), D_MODEL ** -0.5),
        'ffn_w_down': normal((L, 2, D_FF, D_MODEL), DN_BETA * D_FF ** -0.5),
        'w_in': normal((L, D_MODEL, IN_COLS), D_MODEL ** -0.5) * jnp.asarray(col_scale),
        'conv_a_w': normal((L, CONV_WIDTH, CONV_CH), CONV_WIDTH ** -0.5),
        'conv_a_b': normal((L, CONV_CH), 0.01),
        'norm_a_g': 1.0 + normal((L, CONV_CH), 0.02),
        'norm_a_b': normal((L, CONV_CH), 0.02),
        'w_a_out': normal((L, CONV_CH, D_MODEL), CONV_CH ** -0.5),
        'ssm_conv_w': normal((L, SSM_CONV_WIDTH, SSM_CONV_DIM), SSM_CONV_WIDTH ** -0.5),
        'ssm_conv_b': normal((L, SSM_CONV_DIM), 0.01),
        'ssm_dt_bias': dt0 + jnp.log(-jnp.expm1(-dt0)),
        'ssm_a_log': jnp.log(uniform((L, SSM_HEADS), 1.0, 16.0)),
        'ssm_d': 1.0 + normal((L, SSM_HEADS), 0.1),
        'ssm_norm_w': 1.0 + normal((L, SSM_D_INNER), 0.02),
        'w_b_out': normal((L, SSM_D_INNER, D_MODEL), SSM_D_INNER ** -0.5),
        'cmp_pe_k': normal((L, CMP_LEN, NSA_HEAD_DIM), 0.02),
        'cmp_pe_v': normal((L, CMP_LEN, NSA_HEAD_DIM), 0.02),
        'cmp_k_w1': normal((L, cmp_in, CMP_HIDDEN), cmp_in ** -0.5),
        'cmp_k_w2': normal((L, CMP_HIDDEN, NSA_HEAD_DIM), CMP_HIDDEN ** -0.5),
        'cmp_v_w1': normal((L, cmp_in, CMP_HIDDEN), cmp_in ** -0.5),
        'cmp_v_w2': normal((L, CMP_HIDDEN, NSA_HEAD_DIM), CMP_HIDDEN ** -0.5),
        'w_c_out': normal((L, NSA_HEADS * NSA_HEAD_DIM, D_MODEL), (NSA_HEADS * NSA_HEAD_DIM) ** -0.5),
        'w_o': normal((L, D_MODEL, D_MODEL), DN_BETA * D_MODEL ** -0.5),
    }


def reference(x, c, ada_w, ada_b, ln_g, ln_b, ffn_w_gate, ffn_w_up, ffn_w_down, w_in,
              conv_a_w, conv_a_b, norm_a_g, norm_a_b, w_a_out,
              ssm_conv_w, ssm_conv_b, ssm_dt_bias, ssm_a_log, ssm_d, ssm_norm_w, w_b_out,
              cmp_pe_k, cmp_pe_v, cmp_k_w1, cmp_k_w2, cmp_v_w1, cmp_v_w2, w_c_out, w_o):
    bsz = c.shape[0]
    h = x
    for l in range(DEPTH):
        mod = (jax.nn.silu(c) @ ada_w[l] + ada_b[l]).reshape(bsz, 3, 3, 1, D_MODEL)
        u = modulate(h, mod[:, 0, 0], mod[:, 0, 1])
        y = swiglu_ffn(u, ffn_w_gate[l, 0], ffn_w_up[l, 0], ffn_w_down[l, 0])
        h = layer_norm(DN_ALPHA * h + MACARON_WEIGHT * mod[:, 0, 2] * y, ln_g[l, 0], ln_b[l, 0])
        u = modulate(h, mod[:, 1, 0], mod[:, 1, 1])
        y = hybrid_token_mixer(u, w_in[l], conv_a_w[l], conv_a_b[l], norm_a_g[l], norm_a_b[l], w_a_out[l],
                               ssm_conv_w[l], ssm_conv_b[l], ssm_dt_bias[l], ssm_a_log[l], ssm_d[l], ssm_norm_w[l], w_b_out[l],
                               cmp_pe_k[l], cmp_pe_v[l], cmp_k_w1[l], cmp_k_w2[l], cmp_v_w1[l], cmp_v_w2[l], w_c_out[l], w_o[l])
        h = layer_norm(DN_ALPHA * h + mod[:, 1, 2] * y, ln_g[l, 1], ln_b[l, 1])
        u = modulate(h, mod[:, 2, 0], mod[:, 2, 1])
        y = swiglu_ffn(u, ffn_w_gate[l, 1], ffn_w_up[l, 1], ffn_w_down[l, 1])
        h = layer_norm(DN_ALPHA * h + MACARON_WEIGHT * mod[:, 2, 2] * y, ln_g[l, 2], ln_b[l, 2])
    return h
```

```python
import functools
import math

import numpy as np
import jax
import jax.numpy as jnp
from jax import lax
from jax.experimental import pallas as pl
from jax.experimental.pallas import tpu as pltpu

F32 = jnp.float32
BF16 = jnp.bfloat16
HIGHEST = lax.Precision.HIGHEST

LN_EPS = 1e-5
NEG_INF = -1e30
MACARON_WEIGHT = 0.5

SSM_HEAD_DIM = 64
SSM_GROUPS = 4
SSM_STATE = 128
SSM_CHUNK = 256

NSA_HEAD_DIM = 64
NSA_KV_HEADS = 4
CMP_LEN = 32
CMP_STRIDE = 16
SEL_BLOCK = 64
N_SELECT = 16
WINDOW = 512
Q_BLOCK = 64
FORCED_SCORE = 1e4
ROPE_THETA = 10000.0

LANES = 128
SEL_CHUNK = 512
CONV_HALO = 32


def _dot(a, b, precision=None):
    return jnp.dot(a, b, preferred_element_type=F32, precision=precision)


def _dot_nt(a, b, precision=None):
    return lax.dot_general(a, b, (((1,), (1,)), ((), ())), preferred_element_type=F32, precision=precision)


def _layer_norm(z, g, b):
    mu = jnp.mean(z, axis=-1, keepdims=True)
    zc = z - mu
    var = jnp.mean(zc * zc, axis=-1, keepdims=True)
    return zc * lax.rsqrt(var + LN_EPS) * g + b


def _silu(x):
    return x * jax.nn.sigmoid(x)


def _ada_kernel(c_ref, w_ref, b_ref, o_ref):
    a = _silu(c_ref[...]).astype(BF16)
    o_ref[0] = _dot(a, w_ref[0].astype(BF16)) + b_ref[0]


def _ada_mod(c, ada_w, ada_b):
    n_layer, d, n = ada_w.shape
    bsz = c.shape[0]
    rows = 8
    cp = jnp.zeros((rows, d), F32).at[:bsz].set(c)
    tn = 1024
    out = pl.pallas_call(
        _ada_kernel,
        grid=(n_layer, n // tn),
        in_specs=[pl.BlockSpec((rows, d), lambda l, j: (0, 0)),
                  pl.BlockSpec((1, d, tn), lambda l, j: (l, 0, j)),
                  pl.BlockSpec((1, 1, tn), lambda l, j: (l, 0, j))],
        out_specs=pl.BlockSpec((1, rows, tn), lambda l, j: (l, 0, j)),
        out_shape=jax.ShapeDtypeStruct((n_layer, rows, n), F32),
        name="ada_mod",
    )(cp, ada_w, ada_b.reshape(n_layer, 1, n))
    return out[:, :bsz].reshape(n_layer, bsz, 9, d)


def _ffn_kernel(h_ref, mod_ref, wg_ref, wu_ref, wd_ref, lng_ref, lnb_ref, o_ref, u_s, acc_s, *, alpha):
    f = pl.program_id(2)

    @pl.when(f == 0)
    def _():
        u_s[...] = (h_ref[0] * (1.0 + mod_ref[0, 1:2, :]) + mod_ref[0, 0:1, :]).astype(BF16)
        acc_s[...] = jnp.zeros_like(acc_s)

    u = u_s[...]
    g = _dot(u, wg_ref[...])
    up = _dot(u, wu_ref[...])
    a = (_silu(g) * up).astype(BF16)
    acc_s[...] += _dot(a, wd_ref[...])

    @pl.when(f == pl.num_programs(2) - 1)
    def _():
        z = alpha * h_ref[0] + (MACARON_WEIGHT * mod_ref[0, 2:3, :]) * acc_s[...]
        o_ref[0] = _layer_norm(z, lng_ref[...], lnb_ref[...])


def _ffn_block(h, mod3, wg, wu, wd, ln_g, ln_b, alpha, tm=512):
    bsz, seq, d = h.shape
    d_ff = wg.shape[1]
    tf = d_ff // 2 if (d_ff // 2) % LANES == 0 else d_ff
    tm = min(tm, seq)
    return pl.pallas_call(
        functools.partial(_ffn_kernel, alpha=alpha),
        grid=(bsz, seq // tm, d_ff // tf),
        in_specs=[pl.BlockSpec((1, tm, d), lambda b, i, f: (b, i, 0)),
                  pl.BlockSpec((1, 3, d), lambda b, i, f: (b, 0, 0)),
                  pl.BlockSpec((d, tf), lambda b, i, f: (0, f)),
                  pl.BlockSpec((d, tf), lambda b, i, f: (0, f)),
                  pl.BlockSpec((tf, d), lambda b, i, f: (f, 0)),
                  pl.BlockSpec((1, d), lambda b, i, f: (0, 0)),
                  pl.BlockSpec((1, d), lambda b, i, f: (0, 0))],
        out_specs=pl.BlockSpec((1, tm, d), lambda b, i, f: (b, i, 0)),
        out_shape=jax.ShapeDtypeStruct((bsz, seq, d), F32),
        scratch_shapes=[pltpu.VMEM((tm, d), BF16), pltpu.VMEM((tm, d), F32)],
        compiler_params=pltpu.CompilerParams(dimension_semantics=("parallel", "parallel", "arbitrary")),
        name="ffn",
    )(h, mod3, wg, wu, wd, ln_g.reshape(1, d), ln_b.reshape(1, d))


def _inproj_kernel(h_ref, mod_ref, w_ref, o_ref, u_s):
    @pl.when(pl.program_id(2) == 0)
    def _():
        u_s[...] = (h_ref[0] * (1.0 + mod_ref[0, 1:2, :]) + mod_ref[0, 0:1, :]).astype(BF16)

    o_ref[0] = _dot(u_s[...], w_ref[...]).astype(BF16)


def _inproj(h, mod3, w, tm=1024):
    bsz, seq, d = h.shape
    n = w.shape[1]
    tn = n // 5
    tm = min(tm, seq)
    return pl.pallas_call(
        _inproj_kernel,
        grid=(bsz, seq // tm, n // tn),
        in_specs=[pl.BlockSpec((1, tm, d), lambda b, i, j: (b, i, 0)),
                  pl.BlockSpec((1, 3, d), lambda b, i, j: (b, 0, 0)),
                  pl.BlockSpec((d, tn), lambda b, i, j: (0, j))],
        out_specs=pl.BlockSpec((1, tm, tn), lambda b, i, j: (b, i, j)),
        out_shape=jax.ShapeDtypeStruct((bsz, seq, n), BF16),
        scratch_shapes=[pltpu.VMEM((tm, d), BF16)],
        compiler_params=pltpu.CompilerParams(dimension_semantics=("parallel", "parallel", "arbitrary")),
        name="inproj",
    )(h, mod3, w)


def _conf_kernel(cur_ref, halo_ref, w_ref, cb_ref, g_ref, b_ref, o_ref, buf_s, y_s, *, width):
    i = pl.program_id(1)
    ts, ch = y_s.shape
    x = cur_ref[0].astype(F32)
    buf_s[CONV_HALO:, :] = x[:, :ch] * jax.nn.sigmoid(x[:, ch:])
    xh = halo_ref[0].astype(F32)
    ah = xh[:, :ch] * jax.nn.sigmoid(xh[:, ch:])
    buf_s[0:CONV_HALO, :] = jnp.where(i > 0, ah, 0.0)
    off = CONV_HALO - (width - 1)
    rc, lc = 64, 256
    for r0 in range(0, ts, rc):
        for l0 in range(0, ch, lc):
            acc = jnp.zeros((rc, lc), F32) + cb_ref[:, l0:l0 + lc]
            for k in range(width):
                acc = acc + buf_s[r0 + k + off:r0 + k + off + rc, l0:l0 + lc] * w_ref[k:k + 1, l0:l0 + lc]
            y_s[r0:r0 + rc, l0:l0 + lc] = acc
    yn = _layer_norm(y_s[...], g_ref[...], b_ref[...])
    o_ref[0] = _silu(yn).astype(BF16)


def _conformer(proj, conv_w, conv_b, norm_g, norm_b, ts=256):
    bsz, seq, _ = proj.shape
    width, ch = conv_w.shape
    assert width - 1 <= CONV_HALO
    ts = min(ts, seq)
    wpad = jnp.zeros((CONV_HALO, ch), F32).at[:width].set(conv_w)
    hb = ts // CONV_HALO
    return pl.pallas_call(
        functools.partial(_conf_kernel, width=width),
        grid=(bsz, seq // ts),
        in_specs=[pl.BlockSpec((1, ts, 2 * ch), lambda b, i: (b, i, 0)),
                  pl.BlockSpec((1, CONV_HALO, 2 * ch), lambda b, i: (b, jnp.maximum(i * hb - 1, 0), 0)),
                  pl.BlockSpec((CONV_HALO, ch), lambda b, i: (0, 0)),
                  pl.BlockSpec((1, ch), lambda b, i: (0, 0)),
                  pl.BlockSpec((1, ch), lambda b, i: (0, 0)),
                  pl.BlockSpec((1, ch), lambda b, i: (0, 0))],
        out_specs=pl.BlockSpec((1, ts, ch), lambda b, i: (b, i, 0)),
        out_shape=jax.ShapeDtypeStruct((bsz, seq, ch), BF16),
        scratch_shapes=[pltpu.VMEM((CONV_HALO + ts, ch), F32), pltpu.VMEM((ts, ch), F32)],
        compiler_params=pltpu.CompilerParams(dimension_semantics=("parallel", "parallel")),
        name="conformer",
    )(proj, proj, wpad, conv_b.reshape(1, ch), norm_g.reshape(1, ch), norm_b.reshape(1, ch))


def _ssd_kernel(z_ref, xs_ref, bc_ref, sm_ref, cw_ref, cbias_ref, dtb_ref, alog_ref, dfull_ref, nw_ref, ex_ref,
                o_ref, buf_s, st_s, y_s, *, conv_width):
    c = pl.program_id(1)
    lc, di = y_s.shape
    n_grp, n_state, gw = st_s.shape
    hpg = gw // SSM_HEAD_DIM
    carry = 8

    @pl.when(c == 0)
    def _():
        buf_s[0:carry, :] = jnp.zeros((carry, buf_s.shape[1]), F32)
        st_s[...] = jnp.zeros_like(st_s)

    @pl.when(c > 0)
    def _():
        buf_s[0:carry, :] = buf_s[lc:lc + carry, :]

    buf_s[carry:, 0:di] = xs_ref[0].astype(F32)
    buf_s[carry:, di:] = bc_ref[0].astype(F32)
    off = carry - (conv_width - 1)
    acc = jnp.zeros((lc, buf_s.shape[1]), F32) + cbias_ref[...]
    for k in range(conv_width):
        acc = acc + buf_s[k + off:k + off + lc, :] * cw_ref[k:k + 1, :]
    xbc = _silu(acc)
    xs = xbc[:, 0:di]
    gn = n_grp * n_state
    bm = xbc[:, di:di + gn]
    cm = xbc[:, di + gn:di + 2 * gn]

    x_dt = sm_ref[0].astype(F32) + dtb_ref[...]
    dt = jnp.maximum(x_dt, 0.0) + jnp.log(1.0 + jnp.exp(-jnp.abs(x_dt)))
    adt = dt * (-jnp.exp(alog_ref[...]))
    row = lax.broadcasted_iota(jnp.int32, (lc, lc), 0)
    col = lax.broadcasted_iota(jnp.int32, (lc, lc), 1)
    causal = row >= col
    acs = _dot(causal.astype(F32), adt, HIGHEST)
    acs_t = acs.T
    ex = ex_ref[...]
    acs_full = _dot(acs, ex, HIGHEST)
    dt_full = _dot(dt, ex, HIGHEST)
    last_full = acs_full[lc - 1:lc, :]
    xdt = xs * dt_full
    e_acs = jnp.exp(acs_full)
    x_in = (xdt * jnp.exp(last_full - acs_full)).astype(BF16)
    chunk_decay = jnp.exp(last_full)
    lane_head = lax.broadcasted_iota(jnp.int32, (lc, gw), 1) // SSM_HEAD_DIM

    for g in range(n_grp):
        bg = bm[:, g * n_state:(g + 1) * n_state]
        cg = cm[:, g * n_state:(g + 1) * n_state].astype(BF16)
        cb = _dot_nt(cg, bg.astype(BF16))
        hg = st_s[g]
        y_g = _dot(cg, hg.astype(BF16)) * e_acs[:, g * gw:(g + 1) * gw]
        xg = xdt[:, g * gw:(g + 1) * gw]
        for r in range(hpg):
            hd = g * hpg + r
            seg = jnp.exp(jnp.where(causal, acs[:, hd:hd + 1] - acs_t[hd:hd + 1, :], NEG_INF))
            xr = jnp.where(lane_head == r, xg, 0.0).astype(BF16)
            y_g = y_g + _dot((cb * seg).astype(BF16), xr)
        st_s[g] = hg * chunk_decay[:, g * gw:(g + 1) * gw] + _dot(bg.T.astype(BF16), x_in[:, g * gw:(g + 1) * gw])
        y_s[:, g * gw:(g + 1) * gw] = y_g

    y = y_s[...] + dfull_ref[...] * xs
    y = y * _silu(z_ref[0].astype(F32))
    y = y * lax.rsqrt(jnp.mean(y * y, axis=-1, keepdims=True) + LN_EPS) * nw_ref[...]
    o_ref[0] = y.astype(BF16)


def _ssd(proj, col_z, col_xs, col_bc, col_small, conv_w, conv_b, dt_bias, a_log, d_skip, norm_w):
    bsz, seq, _ = proj.shape
    n_heads = dt_bias.shape[0]
    di = n_heads * SSM_HEAD_DIM
    conv_width, conv_dim = conv_w.shape
    gn = SSM_GROUPS * SSM_STATE
    assert conv_dim == di + 2 * gn and 2 * gn == di
    lc = math.gcd(SSM_CHUNK, seq)
    gw = di // SSM_GROUPS
    cw = jnp.zeros((8, conv_dim), F32).at[:conv_width].set(conv_w)
    pad = lambda v: jnp.zeros((1, LANES), F32).at[0, :n_heads].set(v)
    expand = np.zeros((LANES, di), np.float32)
    expand[np.arange(di) // SSM_HEAD_DIM, np.arange(di)] = 1.0
    d_full = jnp.repeat(d_skip, SSM_HEAD_DIM).reshape(1, di)
    cblk = lambda col, w: pl.BlockSpec((1, lc, w), lambda b, c: (b, c, col // w))
    const = lambda shape: pl.BlockSpec(shape, lambda b, c: (0,) * len(shape))
    return pl.pallas_call(
        functools.partial(_ssd_kernel, conv_width=conv_width),
        grid=(bsz, seq // lc),
        in_specs=[cblk(col_z, di), cblk(col_xs, di), cblk(col_bc, di), cblk(col_small, LANES),
                  const((8, conv_dim)), const((1, conv_dim)), const((1, LANES)), const((1, LANES)),
                  const((1, di)), const((1, di)), const((LANES, di))],
        out_specs=pl.BlockSpec((1, lc, di), lambda b, c: (b, c, 0)),
        out_shape=jax.ShapeDtypeStruct((bsz, seq, di), BF16),
        scratch_shapes=[pltpu.VMEM((8 + lc, conv_dim), F32),
                        pltpu.VMEM((SSM_GROUPS, SSM_STATE, gw), F32),
                        pltpu.VMEM((lc, di), F32)],
        compiler_params=pltpu.CompilerParams(dimension_semantics=("parallel", "arbitrary")),
        name="ssd",
    )(proj, proj, proj, proj, cw, conv_b.reshape(1, conv_dim), pad(dt_bias), pad(a_log), d_full,
      norm_w.reshape(1, di), jnp.asarray(expand))


def _rope_tables(pos):
    half = NSA_HEAD_DIM // 2
    inv_freq = ROPE_THETA ** (-np.arange(half, dtype=np.float32) / half)
    ang = jnp.asarray(pos, F32)[:, None] * jnp.asarray(inv_freq)[None, :]
    cos, sin, zero = jnp.cos(ang), jnp.sin(ang), jnp.zeros_like(ang)
    cos_t = jnp.concatenate([cos, cos, cos, cos], -1)
    sin_a = jnp.concatenate([-sin, zero, -sin, zero], -1)
    sin_b = jnp.concatenate([zero, sin, zero, sin], -1)
    return cos_t, sin_a, sin_b


def _rope(x, cos_t, sin_a, sin_b):
    half = NSA_HEAD_DIM // 2
    return x * cos_t + pltpu.roll(x, LANES - half, 1) * sin_a + pltpu.roll(x, half, 1) * sin_b


def _rope_kernel(q_ref, kv_ref, cos_ref, sa_ref, sb_ref, qo_ref, kso_ref, kwo_ref, *, q_scale, col_ks, col_kw):
    cos_t, sin_a, sin_b = cos_ref[...], sa_ref[...], sb_ref[...]
    for l0 in range(0, q_ref.shape[2], LANES):
        xq = q_ref[0, :, l0:l0 + LANES].astype(F32)
        qo_ref[0, :, l0:l0 + LANES] = (_rope(xq, cos_t, sin_a, sin_b) * q_scale).astype(BF16)
    for l0 in range(0, kso_ref.shape[2], LANES):
        xk = kv_ref[0, :, col_ks + l0:col_ks + l0 + LANES].astype(F32)
        kso_ref[0, :, l0:l0 + LANES] = _rope(xk, cos_t, sin_a, sin_b).astype(BF16)
        xk = kv_ref[0, :, col_kw + l0:col_kw + l0 + LANES].astype(F32)
        kwo_ref[0, :, l0:l0 + LANES] = _rope(xk, cos_t, sin_a, sin_b).astype(BF16)


def _rope_qk(proj, col_q, col_kv, n_q, kvd, ts=512):
    bsz, seq, _ = proj.shape
    ts = min(ts, seq)
    cos_t, sin_a, sin_b = _rope_tables(np.arange(seq))
    tab = pl.BlockSpec((ts, LANES), lambda b, i: (i, 0))
    kern = functools.partial(_rope_kernel, q_scale=NSA_HEAD_DIM ** -0.5, col_ks=2 * kvd, col_kw=4 * kvd)
    return pl.pallas_call(
        kern,
        grid=(bsz, seq // ts),
        in_specs=[pl.BlockSpec((1, ts, n_q), lambda b, i: (b, i, col_q // n_q)),
                  pl.BlockSpec((1, ts, 6 * kvd), lambda b, i: (b, i, col_kv // (6 * kvd))),
                  tab, tab, tab],
        out_specs=[pl.BlockSpec((1, ts, n_q), lambda b, i: (b, i, 0)),
                   pl.BlockSpec((1, ts, kvd), lambda b, i: (b, i, 0)),
                   pl.BlockSpec((1, ts, kvd), lambda b, i: (b, i, 0))],
        out_shape=[jax.ShapeDtypeStruct((bsz, seq, n_q), BF16),
                   jax.ShapeDtypeStruct((bsz, seq, kvd), BF16),
                   jax.ShapeDtypeStruct((bsz, seq, kvd), BF16)],
        compiler_params=pltpu.CompilerParams(dimension_semantics=("parallel", "parallel")),
        name="rope_qk",
    )(proj, proj, cos_t, sin_a, sin_b)


def _gelu_tanh(x):
    return 0.5 * x * (1.0 + jnp.tanh(math.sqrt(2.0 / math.pi) * (x + 0.044715 * (x * x * x))))


def _compress_kernel(rk_ref, rv_ref, w1k_ref, w1v_ref, w2k_ref, w2v_ref, pek_ref, pev_ref,
                     cos_ref, sa_ref, sb_ref, ko_ref, vo_ref):
    n_grp, n_rows, half_in = rk_ref.shape[1:]

    def branch(r_ref, w1_ref, w2_ref, pe_ref):
        w1 = w1_ref[...]
        pe_term = _dot(pe_ref[...], w1)[0:1]
        hid = []
        for g in range(n_grp):
            r = r_ref[0, g]
            lo = _dot(r, w1[:half_in])
            hi = _dot(r, w1[half_in:])
            hid.append(_gelu_tanh(lo + pltpu.roll(hi, n_rows - 1, 0) + pe_term).astype(BF16))
        return _dot(jnp.concatenate(hid, axis=1), w2_ref[...])

    k = branch(rk_ref, w1k_ref, w2k_ref, pek_ref)
    cos_t, sin_a, sin_b = cos_ref[...], sa_ref[...], sb_ref[...]
    for l0 in range(0, k.shape[1], LANES):
        ko_ref[0, :, l0:l0 + LANES] = _rope(k[:, l0:l0 + LANES], cos_t, sin_a, sin_b).astype(BF16)
    vo_ref[0] = branch(rv_ref, w1v_ref, w2v_ref, pev_ref).astype(BF16)


def _compress(k_c, v_c, pe_k, pe_v, k_w1, k_w2, v_w1, v_w2):
    bsz, seq, kvd = k_c.shape
    n_kv = kvd // NSA_HEAD_DIM
    assert CMP_LEN == 2 * CMP_STRIDE and seq % CMP_STRIDE == 0
    n_rows = seq // CMP_STRIDE
    half_in = CMP_STRIDE * NSA_HEAD_DIM
    hidden = k_w1.shape[1]

    def rows(a):
        return a.reshape(bsz, n_rows, CMP_STRIDE, n_kv, NSA_HEAD_DIM).transpose(0, 3, 1, 2, 4).reshape(
            bsz, n_kv, n_rows, half_in)

    blockdiag = lambda w: jnp.kron(jnp.eye(n_kv, dtype=F32), w).astype(BF16)
    pe_rows = lambda pe: jnp.broadcast_to(pe.reshape(1, -1), (8, CMP_LEN * NSA_HEAD_DIM)).astype(BF16)
    cos_t, sin_a, sin_b = _rope_tables(np.arange(n_rows) * CMP_STRIDE + CMP_LEN - 1)
    const = lambda shape: pl.BlockSpec(shape, lambda b: (0,) * len(shape))
    rspec = pl.BlockSpec((1, n_kv, n_rows, half_in), lambda b: (b, 0, 0, 0))
    ospec = pl.BlockSpec((1, n_rows, kvd), lambda b: (b, 0, 0))
    return pl.pallas_call(
        _compress_kernel,
        grid=(bsz,),
        in_specs=[rspec, rspec, const((2 * half_in, hidden)), const((2 * half_in, hidden)),
                  const((n_kv * hidden, kvd)), const((n_kv * hidden, kvd)),
                  const((8, 2 * half_in)), const((8, 2 * half_in)),
                  const((n_rows, LANES)), const((n_rows, LANES)), const((n_rows, LANES))],
        out_specs=[ospec, ospec],
        out_shape=[jax.ShapeDtypeStruct((bsz, n_rows, kvd), BF16)] * 2,
        compiler_params=pltpu.CompilerParams(dimension_semantics=("parallel",)),
        name="compress",
    )(rows(k_c), rows(v_c), k_w1.astype(BF16), v_w1.astype(BF16), blockdiag(k_w2), blockdiag(v_w2),
      pe_rows(pe_k), pe_rows(pe_v), cos_t, sin_a, sin_b)


def _softmax_rows(s, mask):
    s = jnp.where(mask, s, NEG_INF)
    e = jnp.exp(s - jnp.max(s, axis=1, keepdims=True))
    return e / jnp.sum(e, axis=1, keepdims=True)


def _nsa_kernel(q_ref, kc_ref, vc_ref, ks_ref, vs_ref, kw_ref, vw_ref, gl_ref, ovt_ref, o_ref, imp_s, *, n_cmp):
    i = pl.program_id(2)
    gqa, qb, hd = q_ref.shape[1:]
    rows = gqa * qb
    n_blk = imp_s.shape[0]
    q = q_ref[0].reshape(rows, hd)
    t = i * qb + lax.broadcasted_iota(jnp.int32, (rows, 1), 0) % qb

    n_rows = kc_ref.shape[2]
    n_idx = lax.broadcasted_iota(jnp.int32, (rows, n_rows), 1)
    cmask = (n_idx * CMP_STRIDE + (CMP_LEN - 1) <= t) & (n_idx < n_cmp)
    p_cmp = jnp.where(cmask, _softmax_rows(_dot_nt(q, kc_ref[0, 0]), cmask), 0.0)
    o_cmp = _dot(p_cmp.astype(BF16), vc_ref[0, 0])

    p_sum = jnp.sum(p_cmp.reshape(gqa, qb, n_rows), axis=0)
    imp = _dot_nt(ovt_ref[...], p_sum, HIGHEST)
    blk = lax.broadcasted_iota(jnp.int32, (n_blk, qb), 0)
    tq = i * qb + lax.broadcasted_iota(jnp.int32, (n_blk, qb), 1)
    cur = tq // SEL_BLOCK
    forced = (blk == 0) | (blk == cur) | (blk == cur - 1)
    imp = jnp.where(forced, FORCED_SCORE, jnp.where(blk * SEL_BLOCK > tq, -1.0, imp))
    imp_s[...] = imp

    n_live = (i * qb + qb - 1) // SEL_BLOCK + 1

    def rank_body(j, cnt):
        other = imp_s[pl.ds(j, 1), :]
        ahead = (other > imp) | ((other == imp) & (j < blk))
        return cnt + ahead.astype(F32)

    rank = lax.fori_loop(0, n_live, rank_body, jnp.zeros((n_blk, qb), F32))
    sel_bias = jnp.where(rank < N_SELECT, 0.0, NEG_INF).T.astype(BF16)

    tq_col = i * qb + lax.broadcasted_iota(jnp.int32, (qb, 1), 0)
    bpc = SEL_CHUNK // SEL_BLOCK

    def sel_body(c, carry):
        m, l, acc = carry
        k0 = pl.multiple_of(c * SEL_CHUNK, SEL_CHUNK)
        s = _dot_nt(q, ks_ref[0, 0, pl.ds(k0, SEL_CHUNK), :])
        e_blk = lax.broadcasted_iota(jnp.int32, (n_blk, SEL_CHUNK), 0)
        e_key = lax.broadcasted_iota(jnp.int32, (n_blk, SEL_CHUNK), 1)
        expand = (e_blk == c * bpc + e_key // SEL_BLOCK).astype(BF16)
        bias = _dot(sel_bias, expand)
        kpos = k0 + lax.broadcasted_iota(jnp.int32, (qb, SEL_CHUNK), 1)
        bias = jnp.where(kpos <= tq_col, bias, NEG_INF)
        s = (s.reshape(gqa, qb, SEL_CHUNK) + bias[None]).reshape(rows, SEL_CHUNK)
        m_new = jnp.maximum(m, jnp.max(s, axis=1, keepdims=True))
        alpha = jnp.exp(m - m_new)
        p = jnp.exp(s - m_new)
        l = alpha * l + jnp.sum(p, axis=1, keepdims=True)
        acc = alpha * acc + _dot(p.astype(BF16), vs_ref[0, 0, pl.ds(k0, SEL_CHUNK), :])
        return m_new, l, acc

    n_chunks = (i * qb + qb - 1) // SEL_CHUNK + 1
    init = (jnp.full((rows, 1), NEG_INF, F32), jnp.zeros((rows, 1), F32), jnp.zeros((rows, hd), F32))
    _, l_sel, acc_sel = lax.fori_loop(0, n_chunks, sel_body, init)
    o_sel = acc_sel / l_sel

    band = WINDOW + qb
    start = pl.multiple_of(jnp.maximum(i * qb - WINDOW, 0), qb)
    wpos = start + lax.broadcasted_iota(jnp.int32, (rows, band), 1)
    wmask = (wpos <= t) & (wpos > t - WINDOW)
    p_win = _softmax_rows(_dot_nt(q, kw_ref[0, 0, pl.ds(start, band), :]), wmask)
    o_win = _dot(p_win.astype(BF16), vw_ref[0, 0, pl.ds(start, band), :])

    gate = jax.nn.sigmoid(gl_ref[0, 0].astype(F32))
    gcol = lambda k: jnp.concatenate([gate[:, 3 * r + k:3 * r + k + 1] for r in range(gqa)], axis=0)
    out = gcol(0) * o_cmp + gcol(1) * o_sel + gcol(2) * o_win
    o_ref[0] = out.reshape(gqa, qb, hd).astype(BF16)


def _nsa_attention(q_hm, kc, vc, ks, vs, kw, vw, gl):
    bsz, n_heads, seq, hd = q_hm.shape
    n_kv = ks.shape[1]
    gqa = n_heads // n_kv
    assert Q_BLOCK == SEL_BLOCK and seq % SEL_CHUNK == 0 and seq >= WINDOW + Q_BLOCK
    n_rows = kc.shape[2]
    n_cmp = (seq - CMP_LEN) // CMP_STRIDE + 1
    n_blk = seq // SEL_BLOCK
    assert N_SELECT <= n_blk
    c_start = np.arange(n_rows)[:, None] * CMP_STRIDE
    s_start = np.arange(n_blk)[None, :] * SEL_BLOCK
    overlap = ((c_start < s_start + SEL_BLOCK) & (c_start + CMP_LEN > s_start)
               & (np.arange(n_rows)[:, None] < n_cmp)).astype(np.float32)
    kv_spec = lambda n: pl.BlockSpec((1, 1, n, hd), lambda b, g, i: (b, g, 0, 0))
    return pl.pallas_call(
        functools.partial(_nsa_kernel, n_cmp=n_cmp),
        grid=(bsz, n_kv, seq // Q_BLOCK),
        in_specs=[pl.BlockSpec((1, gqa, Q_BLOCK, hd), lambda b, g, i: (b, g, i, 0)),
                  kv_spec(n_rows), kv_spec(n_rows), kv_spec(seq), kv_spec(seq), kv_spec(seq), kv_spec(seq),
                  pl.BlockSpec((1, 1, Q_BLOCK, 3 * gqa), lambda b, g, i: (b, g, i, 0)),
                  pl.BlockSpec((n_blk, n_rows), lambda b, g, i: (0, 0))],
        out_specs=pl.BlockSpec((1, gqa, Q_BLOCK, hd), lambda b, g, i: (b, g, i, 0)),
        out_shape=jax.ShapeDtypeStruct((bsz, n_heads, seq, hd), BF16),
        scratch_shapes=[pltpu.VMEM((n_blk, Q_BLOCK), F32)],
        compiler_params=pltpu.CompilerParams(dimension_semantics=("parallel", "parallel", "arbitrary")),
        name="nsa_attention",
    )(q_hm, kc, vc, ks, vs, kw, vw, gl, jnp.asarray(overlap.T))


def _merge_kernel(h_ref, mod_ref, ya_ref, yb_ref, yc_ref, gl_ref, wa_ref, wb_ref, wc_ref, wo_ref,
                  lng_ref, lnb_ref, o_ref, *, alpha):
    d = h_ref.shape[2]
    gate = lambda k: jax.nn.sigmoid(gl_ref[0, :, k * d:(k + 1) * d].astype(F32))
    merged = gate(0) * _dot(ya_ref[0], wa_ref[...])
    merged = merged + gate(1) * _dot(yb_ref[0], wb_ref[...])
    merged = merged + gate(2) * _dot(yc_ref[0], wc_ref[...])
    y = _dot(merged.astype(BF16), wo_ref[...])
    z = alpha * h_ref[0] + mod_ref[0, 2:3, :] * y
    o_ref[0] = _layer_norm(z, lng_ref[...], lnb_ref[...])


def _merge(h, mod3, ya, yb, yc, proj, col_gate, wa, wb, wc, wo, ln_g, ln_b, alpha, tm=512):
    bsz, seq, d = h.shape
    tm = min(tm, seq)
    act = pl.BlockSpec((1, tm, d), lambda b, i: (b, i, 0))
    wspec = pl.BlockSpec((d, d), lambda b, i: (0, 0))
    vec = pl.BlockSpec((1, d), lambda b, i: (0, 0))
    return pl.pallas_call(
        functools.partial(_merge_kernel, alpha=alpha),
        grid=(bsz, seq // tm),
        in_specs=[act, pl.BlockSpec((1, 3, d), lambda b, i: (b, 0, 0)), act, act, act,
                  pl.BlockSpec((1, tm, 3 * d), lambda b, i: (b, i, col_gate // (3 * d))),
                  wspec, wspec, wspec, wspec, vec, vec],
        out_specs=act,
        out_shape=jax.ShapeDtypeStruct((bsz, seq, d), F32),
        compiler_params=pltpu.CompilerParams(dimension_semantics=("parallel", "parallel")),
        name="merge",
    )(h, mod3, ya, yb, yc, proj, wa, wb, wc, wo, ln_g.reshape(1, d), ln_b.reshape(1, d))


def _token_mixer(h, mod3, w_in, conv_a_w, conv_a_b, norm_a_g, norm_a_b, w_a_out,
                 ssm_conv_w, ssm_conv_b, ssm_dt_bias, ssm_a_log, ssm_d, ssm_norm_w, w_b_out,
                 cmp_pe_k, cmp_pe_v, cmp_k_w1, cmp_k_w2, cmp_v_w1, cmp_v_w2, w_c_out, w_o, ln_g, ln_b, alpha):
    bsz, seq, d = h.shape
    ch = conv_a_w.shape[1]
    n_ssm_heads = ssm_dt_bias.shape[0]
    di = n_ssm_heads * SSM_HEAD_DIM
    gn = SSM_GROUPS * SSM_STATE
    n_q = w_c_out.shape[0]
    n_heads = n_q // NSA_HEAD_DIM
    kvd = NSA_KV_HEADS * NSA_HEAD_DIM
    gqa = n_heads // NSA_KV_HEADS
    n_gate = 3 * n_heads
    o_dt = 2 * ch + 2 * di + 2 * gn
    o_q = o_dt + n_ssm_heads
    o_gl = o_q + n_q + 6 * kvd
    o_g = o_gl + n_gate
    assert w_in.shape[1] == o_g + 3 * d
    col_z, col_xs, col_bc = 2 * ch, 2 * ch + di, 2 * ch + 2 * di
    col_q = o_dt
    col_gate = col_q + n_q
    col_kv = col_gate + 3 * d
    col_small = col_kv + 6 * kvd
    assert n_ssm_heads + n_gate <= LANES
    w_perm = jnp.concatenate(
        [w_in[:, :o_dt], w_in[:, o_q:o_q + n_q], w_in[:, o_g:], w_in[:, o_q + n_q:o_gl], w_in[:, o_dt:o_q],
         w_in[:, o_gl:o_g], jnp.zeros((d, LANES - n_ssm_heads - n_gate), F32)], axis=1).astype(BF16)

    proj = _inproj(h, mod3, w_perm)

    y_a = _conformer(proj, conv_a_w, conv_a_b, norm_a_g, norm_a_b)
    y_b = _ssd(proj, col_z, col_xs, col_bc, col_small, ssm_conv_w, ssm_conv_b, ssm_dt_bias, ssm_a_log,
               ssm_d, ssm_norm_w)

    q_r, ks_r, kw_r = _rope_qk(proj, col_q, col_kv, n_q, kvd)
    kv = lambda j: proj[:, :, col_kv + j * kvd:col_kv + (j + 1) * kvd]
    k_cmp, v_cmp = _compress(kv(0), kv(1), cmp_pe_k, cmp_pe_v, cmp_k_w1, cmp_k_w2, cmp_v_w1, cmp_v_w2)
    heads = lambda a: a.reshape(bsz, a.shape[1], -1, NSA_HEAD_DIM).transpose(0, 2, 1, 3)
    gl = proj[:, :, col_small + n_ssm_heads:col_small + n_ssm_heads + n_gate]
    gl = gl.reshape(bsz, seq, NSA_KV_HEADS, 3 * gqa).transpose(0, 2, 1, 3)
    o_hm = _nsa_attention(heads(q_r), heads(k_cmp), heads(v_cmp), heads(ks_r), heads(kv(3)),
                          heads(kw_r), heads(kv(5)), gl)
    y_c = o_hm.transpose(0, 2, 1, 3).reshape(bsz, seq, n_q)

    return _merge(h, mod3, y_a, y_b, y_c, proj, col_gate, w_a_out.astype(BF16), w_b_out.astype(BF16),
                  w_c_out.astype(BF16), w_o.astype(BF16), ln_g, ln_b, alpha)


def kernel(x, c, ada_w, ada_b, ln_g, ln_b, ffn_w_gate, ffn_w_up, ffn_w_down, w_in, conv_a_w, conv_a_b, norm_a_g, norm_a_b, w_a_out, ssm_conv_w, ssm_conv_b, ssm_dt_bias, ssm_a_log, ssm_d, ssm_norm_w, w_b_out, cmp_pe_k, cmp_pe_v, cmp_k_w1, cmp_k_w2, cmp_v_w1, cmp_v_w2, w_c_out, w_o):
    depth = ada_w.shape[0]
    alpha = (2.0 * depth) ** 0.25
    mod = _ada_mod(c, ada_w, ada_b)
    wg, wu, wd = ffn_w_gate.astype(BF16), ffn_w_up.astype(BF16), ffn_w_down.astype(BF16)
    h = x
    for l in range(depth):
        h = _ffn_block(h, mod[l, :, 0:3], wg[l, 0], wu[l, 0], wd[l, 0], ln_g[l, 0], ln_b[l, 0], alpha)
        h = _token_mixer(h, mod[l, :, 3:6], w_in[l], conv_a_w[l], conv_a_b[l], norm_a_g[l], norm_a_b[l], w_a_out[l],
                         ssm_conv_w[l], ssm_conv_b[l], ssm_dt_bias[l], ssm_a_log[l], ssm_d[l], ssm_norm_w[l],
                         w_b_out[l], cmp_pe_k[l], cmp_pe_v[l], cmp_k_w1[l], cmp_k_w2[l], cmp_v_w1[l], cmp_v_w2[l],
                         w_c_out[l], w_o[l], ln_g[l, 1], ln_b[l, 1], alpha)
        h = _ffn_block(h, mod[l, :, 6:9], wg[l, 1], wu[l, 1], wd[l, 1], ln_g[l, 2], ln_b[l, 2], alpha)
    return h
```

```python
import functools
import math

import numpy as np
import jax
import jax.numpy as jnp
from jax import lax
from jax.experimental import pallas as pl
from jax.experimental.pallas import tpu as pltpu

F32 = jnp.float32
BF16 = jnp.bfloat16
HIGHEST = lax.Precision.HIGHEST

LN_EPS = 1e-5
NEG_INF = -1e30
MACARON_WEIGHT = 0.5

SSM_HEAD_DIM = 64
SSM_GROUPS = 4
SSM_STATE = 128
SSM_CHUNK = 256

NSA_HEAD_DIM = 64
NSA_KV_HEADS = 4
CMP_LEN = 32
CMP_STRIDE = 16
SEL_BLOCK = 64
N_SELECT = 16
WINDOW = 512
FORCED_SCORE = 1e4
ROPE_THETA = 10000.0

LANES = 128
SEL_CHUNK = 512
Q_TILE = 128
IMP_PAD = 8
CONV_HALO = 32


def _dot(a, b, precision=None):
    return jnp.dot(a, b, preferred_element_type=F32, precision=precision)


def _dot_nt(a, b, precision=None):
    return lax.dot_general(a, b, (((1,), (1,)), ((), ())), preferred_element_type=F32, precision=precision)


def _layer_norm(z, g, b):
    mu = jnp.mean(z, axis=-1, keepdims=True)
    zc = z - mu
    var = jnp.mean(zc * zc, axis=-1, keepdims=True)
    return zc * lax.rsqrt(var + LN_EPS) * g + b


def _silu(x):
    return x * jax.nn.sigmoid(x)


def _ada_kernel(c_ref, w_ref, b_ref, o_ref):
    a = _silu(c_ref[...]).astype(BF16)
    o_ref[0] = _dot(a, w_ref[0].astype(BF16)) + b_ref[0]


def _ada_mod(c, ada_w, ada_b):
    n_layer, d, n = ada_w.shape
    bsz = c.shape[0]
    rows = 8
    cp = jnp.zeros((rows, d), F32).at[:bsz].set(c)
    tn = 1024
    out = pl.pallas_call(
        _ada_kernel,
        grid=(n_layer, n // tn),
        in_specs=[pl.BlockSpec((rows, d), lambda l, j: (0, 0)),
                  pl.BlockSpec((1, d, tn), lambda l, j: (l, 0, j)),
                  pl.BlockSpec((1, 1, tn), lambda l, j: (l, 0, j))],
        out_specs=pl.BlockSpec((1, rows, tn), lambda l, j: (l, 0, j)),
        out_shape=jax.ShapeDtypeStruct((n_layer, rows, n), F32),
        name="ada_mod",
    )(cp, ada_w, ada_b.reshape(n_layer, 1, n))
    return out[:, :bsz].reshape(n_layer, bsz, 9, d)


def _ffn_kernel(h_ref, mod_ref, wg_ref, wu_ref, wd_ref, lng_ref, lnb_ref, o_ref, u_s, acc_s, *, alpha):
    f = pl.program_id(2)

    @pl.when(f == 0)
    def _():
        u_s[...] = (h_ref[0] * (1.0 + mod_ref[0, 1:2, :]) + mod_ref[0, 0:1, :]).astype(BF16)
        acc_s[...] = jnp.zeros_like(acc_s)

    u = u_s[...]
    g = _dot(u, wg_ref[...])
    up = _dot(u, wu_ref[...])
    a = (_silu(g) * up).astype(BF16)
    acc_s[...] += _dot(a, wd_ref[...])

    @pl.when(f == pl.num_programs(2) - 1)
    def _():
        z = alpha * h_ref[0] + (MACARON_WEIGHT * mod_ref[0, 2:3, :]) * acc_s[...]
        o_ref[0] = _layer_norm(z, lng_ref[...], lnb_ref[...])


def _ffn_block(h, mod3, wg, wu, wd, ln_g, ln_b, alpha, tm=512):
    bsz, seq, d = h.shape
    d_ff = wg.shape[1]
    tf = d_ff // 2 if (d_ff // 2) % LANES == 0 else d_ff
    tm = min(tm, seq)
    return pl.pallas_call(
        functools.partial(_ffn_kernel, alpha=alpha),
        grid=(bsz, seq // tm, d_ff // tf),
        in_specs=[pl.BlockSpec((1, tm, d), lambda b, i, f: (b, i, 0)),
                  pl.BlockSpec((1, 3, d), lambda b, i, f: (b, 0, 0)),
                  pl.BlockSpec((d, tf), lambda b, i, f: (0, f)),
                  pl.BlockSpec((d, tf), lambda b, i, f: (0, f)),
                  pl.BlockSpec((tf, d), lambda b, i, f: (f, 0)),
                  pl.BlockSpec((1, d), lambda b, i, f: (0, 0)),
                  pl.BlockSpec((1, d), lambda b, i, f: (0, 0))],
        out_specs=pl.BlockSpec((1, tm, d), lambda b, i, f: (b, i, 0)),
        out_shape=jax.ShapeDtypeStruct((bsz, seq, d), F32),
        scratch_shapes=[pltpu.VMEM((tm, d), BF16), pltpu.VMEM((tm, d), F32)],
        compiler_params=pltpu.CompilerParams(dimension_semantics=("parallel", "parallel", "arbitrary")),
        name="ffn",
    )(h, mod3, wg, wu, wd, ln_g.reshape(1, d), ln_b.reshape(1, d))


def _inproj_kernel(h_ref, mod_ref, w_ref, o_ref, u_s):
    @pl.when(pl.program_id(2) == 0)
    def _():
        u_s[...] = (h_ref[0] * (1.0 + mod_ref[0, 1:2, :]) + mod_ref[0, 0:1, :]).astype(BF16)

    o_ref[0] = _dot(u_s[...], w_ref[...]).astype(BF16)


def _inproj(h, mod3, w, tm=1024):
    bsz, seq, d = h.shape
    n = w.shape[1]
    tn = n // 5
    tm = min(tm, seq)
    return pl.pallas_call(
        _inproj_kernel,
        grid=(bsz, seq // tm, n // tn),
        in_specs=[pl.BlockSpec((1, tm, d), lambda b, i, j: (b, i, 0)),
                  pl.BlockSpec((1, 3, d), lambda b, i, j: (b, 0, 0)),
                  pl.BlockSpec((d, tn), lambda b, i, j: (0, j))],
        out_specs=pl.BlockSpec((1, tm, tn), lambda b, i, j: (b, i, j)),
        out_shape=jax.ShapeDtypeStruct((bsz, seq, n), BF16),
        scratch_shapes=[pltpu.VMEM((tm, d), BF16)],
        compiler_params=pltpu.CompilerParams(dimension_semantics=("parallel", "parallel", "arbitrary")),
        name="inproj",
    )(h, mod3, w)


def _conf_kernel(cur_ref, halo_ref, w_ref, cb_ref, g_ref, b_ref, o_ref, buf_s, y_s, *, width):
    i = pl.program_id(1)
    ts, ch = y_s.shape
    x = cur_ref[0].astype(F32)
    buf_s[CONV_HALO:, :] = x[:, :ch] * jax.nn.sigmoid(x[:, ch:])
    xh = halo_ref[0].astype(F32)
    ah = xh[:, :ch] * jax.nn.sigmoid(xh[:, ch:])
    buf_s[0:CONV_HALO, :] = jnp.where(i > 0, ah, 0.0)
    off = CONV_HALO - (width - 1)
    rc, lc = 64, 256
    for r0 in range(0, ts, rc):
        for l0 in range(0, ch, lc):
            acc = jnp.zeros((rc, lc), F32) + cb_ref[:, l0:l0 + lc]
            for k in range(width):
                acc = acc + buf_s[r0 + k + off:r0 + k + off + rc, l0:l0 + lc] * w_ref[k:k + 1, l0:l0 + lc]
            y_s[r0:r0 + rc, l0:l0 + lc] = acc
    yn = _layer_norm(y_s[...], g_ref[...], b_ref[...])
    o_ref[0] = _silu(yn).astype(BF16)


def _conformer(proj, conv_w, conv_b, norm_g, norm_b, ts=256):
    bsz, seq, _ = proj.shape
    width, ch = conv_w.shape
    assert width - 1 <= CONV_HALO
    ts = min(ts, seq)
    wpad = jnp.zeros((CONV_HALO, ch), F32).at[:width].set(conv_w)
    hb = ts // CONV_HALO
    return pl.pallas_call(
        functools.partial(_conf_kernel, width=width),
        grid=(bsz, seq // ts),
        in_specs=[pl.BlockSpec((1, ts, 2 * ch), lambda b, i: (b, i, 0)),
                  pl.BlockSpec((1, CONV_HALO, 2 * ch), lambda b, i: (b, jnp.maximum(i * hb - 1, 0), 0)),
                  pl.BlockSpec((CONV_HALO, ch), lambda b, i: (0, 0)),
                  pl.BlockSpec((1, ch), lambda b, i: (0, 0)),
                  pl.BlockSpec((1, ch), lambda b, i: (0, 0)),
                  pl.BlockSpec((1, ch), lambda b, i: (0, 0))],
        out_specs=pl.BlockSpec((1, ts, ch), lambda b, i: (b, i, 0)),
        out_shape=jax.ShapeDtypeStruct((bsz, seq, ch), BF16),
        scratch_shapes=[pltpu.VMEM((CONV_HALO + ts, ch), F32), pltpu.VMEM((ts, ch), F32)],
        compiler_params=pltpu.CompilerParams(dimension_semantics=("parallel", "parallel")),
        name="conformer",
    )(proj, proj, wpad, conv_b.reshape(1, ch), norm_g.reshape(1, ch), norm_b.reshape(1, ch))


def _ssd_kernel(z_ref, xs_ref, bc_ref, sm_ref, cw_ref, cbias_ref, dtb_ref, alog_ref, dfull_ref, nw_ref, ex_ref,
                o_ref, buf_s, st_s, y_s, *, conv_width):
    c = pl.program_id(1)
    lc, di = y_s.shape
    n_grp, n_state, gw = st_s.shape
    hpg = gw // SSM_HEAD_DIM
    carry = 8

    @pl.when(c == 0)
    def _():
        buf_s[0:carry, :] = jnp.zeros((carry, buf_s.shape[1]), F32)
        st_s[...] = jnp.zeros_like(st_s)

    @pl.when(c > 0)
    def _():
        buf_s[0:carry, :] = buf_s[lc:lc + carry, :]

    buf_s[carry:, 0:di] = xs_ref[0].astype(F32)
    buf_s[carry:, di:] = bc_ref[0].astype(F32)
    off = carry - (conv_width - 1)
    acc = jnp.zeros((lc, buf_s.shape[1]), F32) + cbias_ref[...]
    for k in range(conv_width):
        acc = acc + buf_s[k + off:k + off + lc, :] * cw_ref[k:k + 1, :]
    xbc = _silu(acc)
    xs = xbc[:, 0:di]
    gn = n_grp * n_state
    bm = xbc[:, di:di + gn]
    cm = xbc[:, di + gn:di + 2 * gn]

    x_dt = sm_ref[0].astype(F32) + dtb_ref[...]
    dt = jnp.maximum(x_dt, 0.0) + jnp.log(1.0 + jnp.exp(-jnp.abs(x_dt)))
    adt = dt * (-jnp.exp(alog_ref[...]))
    row = lax.broadcasted_iota(jnp.int32, (lc, lc), 0)
    col = lax.broadcasted_iota(jnp.int32, (lc, lc), 1)
    causal = row >= col
    acs = _dot(causal.astype(F32), adt, HIGHEST)
    acs_t = acs.T
    ex = ex_ref[...]
    acs_full = _dot(acs, ex, HIGHEST)
    dt_full = _dot(dt, ex, HIGHEST)
    last_full = acs_full[lc - 1:lc, :]
    xdt = xs * dt_full
    e_acs = jnp.exp(acs_full)
    x_in = (xdt * jnp.exp(last_full - acs_full)).astype(BF16)
    chunk_decay = jnp.exp(last_full)
    lane_head = lax.broadcasted_iota(jnp.int32, (lc, gw), 1) // SSM_HEAD_DIM

    for g in range(n_grp):
        bg = bm[:, g * n_state:(g + 1) * n_state]
        cg = cm[:, g * n_state:(g + 1) * n_state].astype(BF16)
        cb = _dot_nt(cg, bg.astype(BF16))
        hg = st_s[g]
        y_g = _dot(cg, hg.astype(BF16)) * e_acs[:, g * gw:(g + 1) * gw]
        xg = xdt[:, g * gw:(g + 1) * gw]
        for r in range(hpg):
            hd = g * hpg + r
            seg = jnp.exp(jnp.where(causal, acs[:, hd:hd + 1] - acs_t[hd:hd + 1, :], NEG_INF))
            xr = jnp.where(lane_head == r, xg, 0.0).astype(BF16)
            y_g = y_g + _dot((cb * seg).astype(BF16), xr)
        st_s[g] = hg * chunk_decay[:, g * gw:(g + 1) * gw] + _dot(bg.T.astype(BF16), x_in[:, g * gw:(g + 1) * gw])
        y_s[:, g * gw:(g + 1) * gw] = y_g

    y = y_s[...] + dfull_ref[...] * xs
    y = y * _silu(z_ref[0].astype(F32))
    y = y * lax.rsqrt(jnp.mean(y * y, axis=-1, keepdims=True) + LN_EPS) * nw_ref[...]
    o_ref[0] = y.astype(BF16)


def _ssd(proj, col_z, col_xs, col_bc, col_small, conv_w, conv_b, dt_bias, a_log, d_skip, norm_w):
    bsz, seq, _ = proj.shape
    n_heads = dt_bias.shape[0]
    di = n_heads * SSM_HEAD_DIM
    conv_width, conv_dim = conv_w.shape
    gn = SSM_GROUPS * SSM_STATE
    assert conv_dim == di + 2 * gn and 2 * gn == di
    lc = math.gcd(SSM_CHUNK, seq)
    gw = di // SSM_GROUPS
    cw = jnp.zeros((8, conv_dim), F32).at[:conv_width].set(conv_w)
    pad = lambda v: jnp.zeros((1, LANES), F32).at[0, :n_heads].set(v)
    expand = np.zeros((LANES, di), np.float32)
    expand[np.arange(di) // SSM_HEAD_DIM, np.arange(di)] = 1.0
    d_full = jnp.repeat(d_skip, SSM_HEAD_DIM).reshape(1, di)
    cblk = lambda col, w: pl.BlockSpec((1, lc, w), lambda b, c: (b, c, col // w))
    const = lambda shape: pl.BlockSpec(shape, lambda b, c: (0,) * len(shape))
    return pl.pallas_call(
        functools.partial(_ssd_kernel, conv_width=conv_width),
        grid=(bsz, seq // lc),
        in_specs=[cblk(col_z, di), cblk(col_xs, di), cblk(col_bc, di), cblk(col_small, LANES),
                  const((8, conv_dim)), const((1, conv_dim)), const((1, LANES)), const((1, LANES)),
                  const((1, di)), const((1, di)), const((LANES, di))],
        out_specs=pl.BlockSpec((1, lc, di), lambda b, c: (b, c, 0)),
        out_shape=jax.ShapeDtypeStruct((bsz, seq, di), BF16),
        scratch_shapes=[pltpu.VMEM((8 + lc, conv_dim), F32),
                        pltpu.VMEM((SSM_GROUPS, SSM_STATE, gw), F32),
                        pltpu.VMEM((lc, di), F32)],
        compiler_params=pltpu.CompilerParams(dimension_semantics=("parallel", "arbitrary")),
        name="ssd",
    )(proj, proj, proj, proj, cw, conv_b.reshape(1, conv_dim), pad(dt_bias), pad(a_log), d_full,
      norm_w.reshape(1, di), jnp.asarray(expand))


def _rope_tables(pos):
    half = NSA_HEAD_DIM // 2
    inv_freq = ROPE_THETA ** (-np.arange(half, dtype=np.float32) / half)
    ang = jnp.asarray(pos, F32)[:, None] * jnp.asarray(inv_freq)[None, :]
    cos, sin, zero = jnp.cos(ang), jnp.sin(ang), jnp.zeros_like(ang)
    cos_t = jnp.concatenate([cos, cos, cos, cos], -1)
    sin_a = jnp.concatenate([-sin, zero, -sin, zero], -1)
    sin_b = jnp.concatenate([zero, sin, zero, sin], -1)
    return cos_t, sin_a, sin_b


def _rope(x, cos_t, sin_a, sin_b):
    half = NSA_HEAD_DIM // 2
    return x * cos_t + pltpu.roll(x, LANES - half, 1) * sin_a + pltpu.roll(x, half, 1) * sin_b


def _rope_kernel(q_ref, kv_ref, cos_ref, sa_ref, sb_ref, qo_ref, kso_ref, kwo_ref, *, q_scale, col_ks, col_kw):
    cos_t, sin_a, sin_b = cos_ref[...], sa_ref[...], sb_ref[...]
    for l0 in range(0, q_ref.shape[2], LANES):
        xq = q_ref[0, :, l0:l0 + LANES].astype(F32)
        qo_ref[0, :, l0:l0 + LANES] = (_rope(xq, cos_t, sin_a, sin_b) * q_scale).astype(BF16)
    for l0 in range(0, kso_ref.shape[2], LANES):
        xk = kv_ref[0, :, col_ks + l0:col_ks + l0 + LANES].astype(F32)
        kso_ref[0, :, l0:l0 + LANES] = _rope(xk, cos_t, sin_a, sin_b).astype(BF16)
        xk = kv_ref[0, :, col_kw + l0:col_kw + l0 + LANES].astype(F32)
        kwo_ref[0, :, l0:l0 + LANES] = _rope(xk, cos_t, sin_a, sin_b).astype(BF16)


def _rope_qk(proj, col_q, col_kv, n_q, kvd, ts=512):
    bsz, seq, _ = proj.shape
    ts = min(ts, seq)
    cos_t, sin_a, sin_b = _rope_tables(np.arange(seq))
    tab = pl.BlockSpec((ts, LANES), lambda b, i: (i, 0))
    kern = functools.partial(_rope_kernel, q_scale=NSA_HEAD_DIM ** -0.5, col_ks=2 * kvd, col_kw=4 * kvd)
    return pl.pallas_call(
        kern,
        grid=(bsz, seq // ts),
        in_specs=[pl.BlockSpec((1, ts, n_q), lambda b, i: (b, i, col_q // n_q)),
                  pl.BlockSpec((1, ts, 6 * kvd), lambda b, i: (b, i, col_kv // (6 * kvd))),
                  tab, tab, tab],
        out_specs=[pl.BlockSpec((1, ts, n_q), lambda b, i: (b, i, 0)),
                   pl.BlockSpec((1, ts, kvd), lambda b, i: (b, i, 0)),
                   pl.BlockSpec((1, ts, kvd), lambda b, i: (b, i, 0))],
        out_shape=[jax.ShapeDtypeStruct((bsz, seq, n_q), BF16),
                   jax.ShapeDtypeStruct((bsz, seq, kvd), BF16),
                   jax.ShapeDtypeStruct((bsz, seq, kvd), BF16)],
        compiler_params=pltpu.CompilerParams(dimension_semantics=("parallel", "parallel")),
        name="rope_qk",
    )(proj, proj, cos_t, sin_a, sin_b)


def _gelu_tanh(x):
    return 0.5 * x * (1.0 + jnp.tanh(math.sqrt(2.0 / math.pi) * (x + 0.044715 * (x * x * x))))


def _compress_kernel(rk_ref, rv_ref, w1k_ref, w1v_ref, w2k_ref, w2v_ref, pek_ref, pev_ref,
                     cos_ref, sa_ref, sb_ref, ko_ref, vo_ref):
    n_grp, n_rows, half_in = rk_ref.shape[1:]

    def branch(r_ref, w1_ref, w2_ref, pe_ref):
        w1 = w1_ref[...]
        pe_term = _dot(pe_ref[...], w1)[0:1]
        hid = []
        for g in range(n_grp):
            r = r_ref[0, g]
            lo = _dot(r, w1[:half_in])
            hi = _dot(r, w1[half_in:])
            hid.append(_gelu_tanh(lo + pltpu.roll(hi, n_rows - 1, 0) + pe_term).astype(BF16))
        return _dot(jnp.concatenate(hid, axis=1), w2_ref[...])

    k = branch(rk_ref, w1k_ref, w2k_ref, pek_ref)
    cos_t, sin_a, sin_b = cos_ref[...], sa_ref[...], sb_ref[...]
    for l0 in range(0, k.shape[1], LANES):
        ko_ref[0, :, l0:l0 + LANES] = _rope(k[:, l0:l0 + LANES], cos_t, sin_a, sin_b).astype(BF16)
    vo_ref[0] = branch(rv_ref, w1v_ref, w2v_ref, pev_ref).astype(BF16)


def _compress(k_c, v_c, pe_k, pe_v, k_w1, k_w2, v_w1, v_w2):
    bsz, seq, kvd = k_c.shape
    n_kv = kvd // NSA_HEAD_DIM
    assert CMP_LEN == 2 * CMP_STRIDE and seq % CMP_STRIDE == 0
    n_rows = seq // CMP_STRIDE
    half_in = CMP_STRIDE * NSA_HEAD_DIM
    hidden = k_w1.shape[1]

    def rows(a):
        return a.reshape(bsz, n_rows, CMP_STRIDE, n_kv, NSA_HEAD_DIM).transpose(0, 3, 1, 2, 4).reshape(
            bsz, n_kv, n_rows, half_in)

    blockdiag = lambda w: jnp.kron(jnp.eye(n_kv, dtype=F32), w).astype(BF16)
    pe_rows = lambda pe: jnp.broadcast_to(pe.reshape(1, -1), (8, CMP_LEN * NSA_HEAD_DIM)).astype(BF16)
    cos_t, sin_a, sin_b = _rope_tables(np.arange(n_rows) * CMP_STRIDE + CMP_LEN - 1)
    const = lambda shape: pl.BlockSpec(shape, lambda b: (0,) * len(shape))
    rspec = pl.BlockSpec((1, n_kv, n_rows, half_in), lambda b: (b, 0, 0, 0))
    ospec = pl.BlockSpec((1, n_rows, kvd), lambda b: (b, 0, 0))
    return pl.pallas_call(
        _compress_kernel,
        grid=(bsz,),
        in_specs=[rspec, rspec, const((2 * half_in, hidden)), const((2 * half_in, hidden)),
                  const((n_kv * hidden, kvd)), const((n_kv * hidden, kvd)),
                  const((8, 2 * half_in)), const((8, 2 * half_in)),
                  const((n_rows, LANES)), const((n_rows, LANES)), const((n_rows, LANES))],
        out_specs=[ospec, ospec],
        out_shape=[jax.ShapeDtypeStruct((bsz, n_rows, kvd), BF16)] * 2,
        compiler_params=pltpu.CompilerParams(dimension_semantics=("parallel",)),
        name="compress",
    )(rows(k_c), rows(v_c), k_w1.astype(BF16), v_w1.astype(BF16), blockdiag(k_w2), blockdiag(v_w2),
      pe_rows(pe_k), pe_rows(pe_v), cos_t, sin_a, sin_b)


def _dot_tn(a, b):
    return lax.dot_general(a, b, (((0,), (0,)), ((), ())), preferred_element_type=F32)


def _nsa_kernel(q_ref, kc_ref, vc_ref, ka_ref, vs_ref, kw_ref, vw_ref, gl_ref, o_ref, psum_s, imp_s, *, n_cmp):
    i = pl.program_id(2)
    gqa, qt, hd = o_ref.shape[1:]
    cols = gqa * qt
    n_blk = imp_s.shape[0]
    n_rows = kc_ref.shape[2]
    band = WINDOW + qt
    qp = q_ref[0].reshape(cols, LANES)
    q = qp[:, :hd]
    lane_q = lax.broadcasted_iota(jnp.int32, (1, cols), 1) % qt
    t0 = i * qt
    q0 = pl.multiple_of(t0, qt)
    w0 = pl.multiple_of(jnp.maximum(t0 - WINDOW, 0), qt)

    k_all = jnp.concatenate([kc_ref[0, 0], kw_ref[0, 0, pl.ds(w0, band), :], ka_ref[0, 0, pl.ds(q0, qt), :hd]], axis=0)
    s_all = _dot_nt(k_all, q)
    s_c, s_w, s_d = s_all[:n_rows], s_all[n_rows:n_rows + band], s_all[n_rows + band:]

    n_idx = lax.broadcasted_iota(jnp.int32, (n_rows, cols), 0)
    cmask = (n_idx * CMP_STRIDE + (CMP_LEN - 1) - lane_q <= t0) & (n_idx < n_cmp)
    s_c = jnp.where(cmask, s_c, NEG_INF)
    e_c = jnp.exp(s_c - jnp.max(s_c, axis=0, keepdims=True))
    p_cmp = jnp.where(cmask, e_c / jnp.sum(e_c, axis=0, keepdims=True), 0.0)
    o_cmp = _dot_tn(vc_ref[0, 0], p_cmp.astype(BF16))

    w_rel = lax.broadcasted_iota(jnp.int32, (band, cols), 0) - lane_q
    wmask = (w_rel <= t0 - w0) & (w_rel > t0 - w0 - WINDOW)
    s_w = jnp.where(wmask, s_w, NEG_INF)
    e_w = jnp.exp(s_w - jnp.max(s_w, axis=0, keepdims=True))
    p_win = e_w / jnp.sum(e_w, axis=0, keepdims=True)
    o_win = _dot_tn(vw_ref[0, 0, pl.ds(w0, band), :], p_win.astype(BF16))

    s_d = jnp.where(lax.broadcasted_iota(jnp.int32, (qt, cols), 0) <= lane_q, s_d, NEG_INF)
    m0 = jnp.max(s_d, axis=0, keepdims=True)
    p_d = jnp.exp(s_d - m0)
    l0 = jnp.sum(p_d, axis=0, keepdims=True)
    acc0 = _dot_tn(vs_ref[0, 0, pl.ds(q0, qt), :], p_d.astype(BF16))

    p_sum = p_cmp[:, 0:qt]
    for r in range(1, gqa):
        p_sum = p_sum + p_cmp[:, r * qt:(r + 1) * qt]
    psum_s[0:IMP_PAD, :] = jnp.zeros((IMP_PAD, qt), F32)
    psum_s[IMP_PAD:, :] = p_sum
    ratio = SEL_BLOCK // CMP_STRIDE
    imp = psum_s[pl.ds(IMP_PAD - 1, n_blk, stride=ratio), :]
    for k in range(ratio):
        imp = imp + psum_s[pl.ds(IMP_PAD + k, n_blk, stride=ratio), :]
    blk = lax.broadcasted_iota(jnp.int32, (n_blk, qt), 0)
    tq = t0 + lax.broadcasted_iota(jnp.int32, (n_blk, qt), 1)
    cur = tq // SEL_BLOCK
    forced = (blk == 0) | (blk == cur) | (blk == cur - 1)
    imp = jnp.where(forced, FORCED_SCORE, jnp.where(blk * SEL_BLOCK > tq, -1.0, imp))
    imp_s[...] = imp

    n_live = (t0 + qt - 1) // SEL_BLOCK + 1

    def rank_body(j, cnt):
        other = imp_s[pl.ds(j, 1), :]
        ahead = (other > imp) | ((other == imp) & (j < blk))
        return cnt + ahead.astype(F32)

    rank = lax.fori_loop(0, jnp.where(n_live > N_SELECT, n_live, 0), rank_body, jnp.zeros((n_blk, qt), F32))
    sel_bias = jnp.where((rank < N_SELECT) & (blk * SEL_BLOCK < t0), 0.0, NEG_INF)
    pad_rows = [jnp.zeros((LANES - hd - n_blk, qt), F32)] if LANES - hd > n_blk else []
    bias_t = jnp.concatenate([jnp.zeros((hd, qt), F32), sel_bias] + pad_rows, axis=0).T
    qa = (qp.astype(F32) + jnp.concatenate([bias_t] * gqa, axis=0)).astype(BF16)

    def sel_body(c, carry):
        m, l, acc = carry
        k0 = pl.multiple_of(c * SEL_CHUNK, SEL_CHUNK)
        s = _dot_nt(ka_ref[0, 0, pl.ds(k0, SEL_CHUNK), :], qa)
        m_new = jnp.maximum(m, jnp.max(s, axis=0, keepdims=True))
        alpha = jnp.exp(m - m_new)
        p = jnp.exp(s - m_new)
        l = alpha * l + jnp.sum(p, axis=0, keepdims=True)
        acc = alpha * acc + _dot_tn(vs_ref[0, 0, pl.ds(k0, SEL_CHUNK), :], p.astype(BF16))
        return m_new, l, acc

    n_chunks = (t0 + SEL_CHUNK - 1) // SEL_CHUNK
    _, l_sel, acc_sel = lax.fori_loop(0, n_chunks, sel_body, (m0, l0, acc0))
    o_sel = acc_sel / l_sel

    gate = jax.nn.sigmoid(gl_ref[0, 0, 0].astype(F32))
    out = gate[0:1] * o_cmp + gate[1:2] * o_sel + gate[2:3] * o_win
    o_ref[0] = out.T.reshape(gqa, qt, hd).astype(BF16)


def _nsa_attention(q_hm, kc, vc, ks, vs, kw, vw, gl):
    bsz, n_heads, seq, hd = q_hm.shape
    n_kv = ks.shape[1]
    gqa = n_heads // n_kv
    n_rows = kc.shape[2]
    n_cmp = (seq - CMP_LEN) // CMP_STRIDE + 1
    nq = seq // Q_TILE
    nb = LANES - hd
    n_blk = seq // SEL_BLOCK
    assert N_SELECT <= n_blk <= nb and Q_TILE == 2 * SEL_BLOCK and seq % SEL_CHUNK == 0 and seq >= WINDOW + Q_TILE
    assert n_rows * CMP_STRIDE == seq and n_blk * (SEL_BLOCK // CMP_STRIDE) <= n_rows and CMP_LEN == 2 * CMP_STRIDE
    q_pad = jnp.concatenate([q_hm, jnp.zeros_like(q_hm)], axis=-1)
    onehot = (np.arange(seq)[:, None] // SEL_BLOCK == np.arange(nb)[None, :]).astype(np.float32)
    ka = jnp.concatenate([ks, jnp.broadcast_to(jnp.asarray(onehot, BF16), (bsz, n_kv, seq, nb))], axis=-1)
    gl2 = gl.reshape(bsz, n_kv, nq, Q_TILE, gqa, 3).transpose(0, 1, 2, 5, 4, 3).reshape(bsz, n_kv, nq, 3, gqa * Q_TILE)
    kv_spec = lambda n, w: pl.BlockSpec((1, 1, n, w), lambda b, g, i: (b, g, 0, 0))
    return pl.pallas_call(
        functools.partial(_nsa_kernel, n_cmp=n_cmp),
        grid=(bsz, n_kv, nq),
        in_specs=[pl.BlockSpec((1, gqa, Q_TILE, LANES), lambda b, g, i: (b, g, i, 0)),
                  kv_spec(n_rows, hd), kv_spec(n_rows, hd), kv_spec(seq, LANES), kv_spec(seq, hd),
                  kv_spec(seq, hd), kv_spec(seq, hd),
                  pl.BlockSpec((1, 1, 1, 3, gqa * Q_TILE), lambda b, g, i: (b, g, i, 0, 0))],
        out_specs=pl.BlockSpec((1, gqa, Q_TILE, hd), lambda b, g, i: (b, g, i, 0)),
        out_shape=jax.ShapeDtypeStruct((bsz, n_heads, seq, hd), BF16),
        scratch_shapes=[pltpu.VMEM((IMP_PAD + n_rows, Q_TILE), F32), pltpu.VMEM((n_blk, Q_TILE), F32)],
        compiler_params=pltpu.CompilerParams(dimension_semantics=("parallel", "parallel", "arbitrary")),
        name="nsa_attention",
    )(q_pad, kc, vc, ka, vs, kw, vw, gl2)


def _merge_kernel(h_ref, mod_ref, ya_ref, yb_ref, yc_ref, gl_ref, wa_ref, wb_ref, wc_ref, wo_ref,
                  lng_ref, lnb_ref, o_ref, *, alpha):
    d = h_ref.shape[2]
    gate = lambda k: jax.nn.sigmoid(gl_ref[0, :, k * d:(k + 1) * d].astype(F32))
    merged = gate(0) * _dot(ya_ref[0], wa_ref[...])
    merged = merged + gate(1) * _dot(yb_ref[0], wb_ref[...])
    merged = merged + gate(2) * _dot(yc_ref[0], wc_ref[...])
    y = _dot(merged.astype(BF16), wo_ref[...])
    z = alpha * h_ref[0] + mod_ref[0, 2:3, :] * y
    o_ref[0] = _layer_norm(z, lng_ref[...], lnb_ref[...])


def _merge(h, mod3, ya, yb, yc, proj, col_gate, wa, wb, wc, wo, ln_g, ln_b, alpha, tm=512):
    bsz, seq, d = h.shape
    tm = min(tm, seq)
    act = pl.BlockSpec((1, tm, d), lambda b, i: (b, i, 0))
    wspec = pl.BlockSpec((d, d), lambda b, i: (0, 0))
    vec = pl.BlockSpec((1, d), lambda b, i: (0, 0))
    return pl.pallas_call(
        functools.partial(_merge_kernel, alpha=alpha),
        grid=(bsz, seq // tm),
        in_specs=[act, pl.BlockSpec((1, 3, d), lambda b, i: (b, 0, 0)), act, act, act,
                  pl.BlockSpec((1, tm, 3 * d), lambda b, i: (b, i, col_gate // (3 * d))),
                  wspec, wspec, wspec, wspec, vec, vec],
        out_specs=act,
        out_shape=jax.ShapeDtypeStruct((bsz, seq, d), F32),
        compiler_params=pltpu.CompilerParams(dimension_semantics=("parallel", "parallel")),
        name="merge",
    )(h, mod3, ya, yb, yc, proj, wa, wb, wc, wo, ln_g.reshape(1, d), ln_b.reshape(1, d))


def _token_mixer(h, mod3, w_in, conv_a_w, conv_a_b, norm_a_g, norm_a_b, w_a_out,
                 ssm_conv_w, ssm_conv_b, ssm_dt_bias, ssm_a_log, ssm_d, ssm_norm_w, w_b_out,
                 cmp_pe_k, cmp_pe_v, cmp_k_w1, cmp_k_w2, cmp_v_w1, cmp_v_w2, w_c_out, w_o, ln_g, ln_b, alpha):
    bsz, seq, d = h.shape
    ch = conv_a_w.shape[1]
    n_ssm_heads = ssm_dt_bias.shape[0]
    di = n_ssm_heads * SSM_HEAD_DIM
    gn = SSM_GROUPS * SSM_STATE
    n_q = w_c_out.shape[0]
    n_heads = n_q // NSA_HEAD_DIM
    kvd = NSA_KV_HEADS * NSA_HEAD_DIM
    gqa = n_heads // NSA_KV_HEADS
    n_gate = 3 * n_heads
    o_dt = 2 * ch + 2 * di + 2 * gn
    o_q = o_dt + n_ssm_heads
    o_gl = o_q + n_q + 6 * kvd
    o_g = o_gl + n_gate
    assert w_in.shape[1] == o_g + 3 * d
    col_z, col_xs, col_bc = 2 * ch, 2 * ch + di, 2 * ch + 2 * di
    col_q = o_dt
    col_gate = col_q + n_q
    col_kv = col_gate + 3 * d
    col_small = col_kv + 6 * kvd
    assert n_ssm_heads + n_gate <= LANES
    w_perm = jnp.concatenate(
        [w_in[:, :o_dt], w_in[:, o_q:o_q + n_q], w_in[:, o_g:], w_in[:, o_q + n_q:o_gl], w_in[:, o_dt:o_q],
         w_in[:, o_gl:o_g], jnp.zeros((d, LANES - n_ssm_heads - n_gate), F32)], axis=1).astype(BF16)

    proj = _inproj(h, mod3, w_perm)

    y_a = _conformer(proj, conv_a_w, conv_a_b, norm_a_g, norm_a_b)
    y_b = _ssd(proj, col_z, col_xs, col_bc, col_small, ssm_conv_w, ssm_conv_b, ssm_dt_bias, ssm_a_log,
               ssm_d, ssm_norm_w)

    q_r, ks_r, kw_r = _rope_qk(proj, col_q, col_kv, n_q, kvd)
    kv = lambda j: proj[:, :, col_kv + j * kvd:col_kv + (j + 1) * kvd]
    k_cmp, v_cmp = _compress(kv(0), kv(1), cmp_pe_k, cmp_pe_v, cmp_k_w1, cmp_k_w2, cmp_v_w1, cmp_v_w2)
    heads = lambda a: a.reshape(bsz, a.shape[1], -1, NSA_HEAD_DIM).transpose(0, 2, 1, 3)
    gl = proj[:, :, col_small + n_ssm_heads:col_small + n_ssm_heads + n_gate]
    gl = gl.reshape(bsz, seq, NSA_KV_HEADS, 3 * gqa).transpose(0, 2, 1, 3)
    o_hm = _nsa_attention(heads(q_r), heads(k_cmp), heads(v_cmp), heads(ks_r), heads(kv(3)),
                          heads(kw_r), heads(kv(5)), gl)
    y_c = o_hm.transpose(0, 2, 1, 3).reshape(bsz, seq, n_q)

    return _merge(h, mod3, y_a, y_b, y_c, proj, col_gate, w_a_out.astype(BF16), w_b_out.astype(BF16),
                  w_c_out.astype(BF16), w_o.astype(BF16), ln_g, ln_b, alpha)


def kernel(x, c, ada_w, ada_b, ln_g, ln_b, ffn_w_gate, ffn_w_up, ffn_w_down, w_in, conv_a_w, conv_a_b, norm_a_g, norm_a_b, w_a_out, ssm_conv_w, ssm_conv_b, ssm_dt_bias, ssm_a_log, ssm_d, ssm_norm_w, w_b_out, cmp_pe_k, cmp_pe_v, cmp_k_w1, cmp_k_w2, cmp_v_w1, cmp_v_w2, w_c_out, w_o):
    depth = ada_w.shape[0]
    alpha = (2.0 * depth) ** 0.25
    mod = _ada_mod(c, ada_w, ada_b)
    wg, wu, wd = ffn_w_gate.astype(BF16), ffn_w_up.astype(BF16), ffn_w_down.astype(BF16)
    h = x
    for l in range(depth):
        h = _ffn_block(h, mod[l, :, 0:3], wg[l, 0], wu[l, 0], wd[l, 0], ln_g[l, 0], ln_b[l, 0], alpha)
        h = _token_mixer(h, mod[l, :, 3:6], w_in[l], conv_a_w[l], conv_a_b[l], norm_a_g[l], norm_a_b[l], w_a_out[l],
                         ssm_conv_w[l], ssm_conv_b[l], ssm_dt_bias[l], ssm_a_log[l], ssm_d[l], ssm_norm_w[l],
                         w_b_out[l], cmp_pe_k[l], cmp_pe_v[l], cmp_k_w1[l], cmp_k_w2[l], cmp_v_w1[l], cmp_v_w2[l],
                         w_c_out[l], w_o[l], ln_g[l, 1], ln_b[l, 1], alpha)
        h = _ffn_block(h, mod[l, :, 6:9], wg[l, 1], wu[l, 1], wd[l, 1], ln_g[l, 2], ln_b[l, 2], alpha)
    return h
```

```python
import functools
import math

import numpy as np
import jax
import jax.numpy as jnp
from jax import lax
from jax.experimental import pallas as pl
from jax.experimental.pallas import tpu as pltpu

F32 = jnp.float32
BF16 = jnp.bfloat16
HIGHEST = lax.Precision.HIGHEST

LN_EPS = 1e-5
NEG_INF = -1e30
MACARON_WEIGHT = 0.5

SSM_HEAD_DIM = 64
SSM_GROUPS = 4
SSM_STATE = 128
SSM_CHUNK = 256

NSA_HEAD_DIM = 64
NSA_KV_HEADS = 4
CMP_LEN = 32
CMP_STRIDE = 16
SEL_BLOCK = 64
N_SELECT = 16
WINDOW = 512
FORCED_SCORE = 1e4
ROPE_THETA = 10000.0

LANES = 128
SUBLANES = 8
SEL_CHUNK = 1024
LAZY_MAX_SLACK = 60.0
RANK_UNROLL = 2
Q_TILE = 128
IMP_PAD = 8
CONV_HALO = 32


def _dot(a, b, precision=None):
    return jnp.dot(a, b, preferred_element_type=F32, precision=precision)


def _dot_nt(a, b, precision=None):
    return lax.dot_general(a, b, (((1,), (1,)), ((), ())), preferred_element_type=F32, precision=precision)


def _layer_norm(z, g, b):
    mu = jnp.mean(z, axis=-1, keepdims=True)
    zc = z - mu
    var = jnp.mean(zc * zc, axis=-1, keepdims=True)
    return zc * lax.rsqrt(var + LN_EPS) * g + b


def _silu(x):
    return x * jax.nn.sigmoid(x)


def _ada_kernel(c_ref, w_ref, b_ref, o_ref):
    a = _silu(c_ref[...]).astype(BF16)
    o_ref[0] = _dot(a, w_ref[0].astype(BF16)) + b_ref[0]


def _ada_mod(c, ada_w, ada_b):
    n_layer, d, n = ada_w.shape
    bsz = c.shape[0]
    rows = 8
    cp = jnp.zeros((rows, d), F32).at[:bsz].set(c)
    tn = 1024
    out = pl.pallas_call(
        _ada_kernel,
        grid=(n_layer, n // tn),
        in_specs=[pl.BlockSpec((rows, d), lambda l, j: (0, 0)),
                  pl.BlockSpec((1, d, tn), lambda l, j: (l, 0, j)),
                  pl.BlockSpec((1, 1, tn), lambda l, j: (l, 0, j))],
        out_specs=pl.BlockSpec((1, rows, tn), lambda l, j: (l, 0, j)),
        out_shape=jax.ShapeDtypeStruct((n_layer, rows, n), F32),
        name="ada_mod",
    )(cp, ada_w, ada_b.reshape(n_layer, 1, n))
    return out[:, :bsz].reshape(n_layer, bsz, 9, d)


def _ffn_kernel(h_ref, mod_ref, wg_ref, wu_ref, wd_ref, lng_ref, lnb_ref, o_ref, u_s, acc_s, *, alpha):
    f = pl.program_id(2)

    @pl.when(f == 0)
    def _():
        u_s[...] = (h_ref[0] * (1.0 + mod_ref[0, 1:2, :]) + mod_ref[0, 0:1, :]).astype(BF16)
        acc_s[...] = jnp.zeros_like(acc_s)

    u = u_s[...]
    g = _dot(u, wg_ref[...])
    up = _dot(u, wu_ref[...])
    a = (_silu(g) * up).astype(BF16)
    acc_s[...] += _dot(a, wd_ref[...])

    @pl.when(f == pl.num_programs(2) - 1)
    def _():
        z = alpha * h_ref[0] + (MACARON_WEIGHT * mod_ref[0, 2:3, :]) * acc_s[...]
        o_ref[0] = _layer_norm(z, lng_ref[...], lnb_ref[...])


def _ffn_block(h, mod3, wg, wu, wd, ln_g, ln_b, alpha, tm=512):
    bsz, seq, d = h.shape
    d_ff = wg.shape[1]
    tf = d_ff // 2 if (d_ff // 2) % LANES == 0 else d_ff
    tm = min(tm, seq)
    return pl.pallas_call(
        functools.partial(_ffn_kernel, alpha=alpha),
        grid=(bsz, seq // tm, d_ff // tf),
        in_specs=[pl.BlockSpec((1, tm, d), lambda b, i, f: (b, i, 0)),
                  pl.BlockSpec((1, 3, d), lambda b, i, f: (b, 0, 0)),
                  pl.BlockSpec((d, tf), lambda b, i, f: (0, f)),
                  pl.BlockSpec((d, tf), lambda b, i, f: (0, f)),
                  pl.BlockSpec((tf, d), lambda b, i, f: (f, 0)),
                  pl.BlockSpec((1, d), lambda b, i, f: (0, 0)),
                  pl.BlockSpec((1, d), lambda b, i, f: (0, 0))],
        out_specs=pl.BlockSpec((1, tm, d), lambda b, i, f: (b, i, 0)),
        out_shape=jax.ShapeDtypeStruct((bsz, seq, d), F32),
        scratch_shapes=[pltpu.VMEM((tm, d), BF16), pltpu.VMEM((tm, d), F32)],
        compiler_params=pltpu.CompilerParams(dimension_semantics=("parallel", "parallel", "arbitrary")),
        name="ffn",
    )(h, mod3, wg, wu, wd, ln_g.reshape(1, d), ln_b.reshape(1, d))


def _inproj_kernel(h_ref, mod_ref, w_ref, o_ref, u_s):
    @pl.when(pl.program_id(2) == 0)
    def _():
        u_s[...] = (h_ref[0] * (1.0 + mod_ref[0, 1:2, :]) + mod_ref[0, 0:1, :]).astype(BF16)

    o_ref[0] = _dot(u_s[...], w_ref[...]).astype(BF16)


def _inproj(h, mod3, w, tm=1024):
    bsz, seq, d = h.shape
    n = w.shape[1]
    tn = n // 5
    tm = min(tm, seq)
    return pl.pallas_call(
        _inproj_kernel,
        grid=(bsz, seq // tm, n // tn),
        in_specs=[pl.BlockSpec((1, tm, d), lambda b, i, j: (b, i, 0)),
                  pl.BlockSpec((1, 3, d), lambda b, i, j: (b, 0, 0)),
                  pl.BlockSpec((d, tn), lambda b, i, j: (0, j))],
        out_specs=pl.BlockSpec((1, tm, tn), lambda b, i, j: (b, i, j)),
        out_shape=jax.ShapeDtypeStruct((bsz, seq, n), BF16),
        scratch_shapes=[pltpu.VMEM((tm, d), BF16)],
        compiler_params=pltpu.CompilerParams(dimension_semantics=("parallel", "parallel", "arbitrary")),
        name="inproj",
    )(h, mod3, w)


def _conf_kernel(cur_ref, halo_ref, w_ref, cb_ref, g_ref, b_ref, o_ref, buf_s, sh_s, y_s, *, width):
    i = pl.program_id(1)
    ts, ch = y_s.shape
    x = cur_ref[0].astype(F32)
    buf_s[CONV_HALO:, :] = x[:, :ch] * jax.nn.sigmoid(x[:, ch:])
    xh = halo_ref[0].astype(F32)
    ah = xh[:, :ch] * jax.nn.sigmoid(xh[:, ch:])
    buf_s[0:CONV_HALO, :] = jnp.where(i > 0, ah, 0.0)
    off = CONV_HALO - (width - 1)
    n_sh = sh_s.shape[1]
    for r in range(1, SUBLANES):
        sh_s[r - 1] = buf_s[r:r + n_sh, :]
    rc, lc = 64, 256
    for r0 in range(0, ts, rc):
        for l0 in range(0, ch, lc):
            acc = jnp.zeros((rc, lc), F32) + cb_ref[:, l0:l0 + lc]
            for k in range(width):
                a, r = divmod(k + off, SUBLANES)
                rows = slice(r0 + SUBLANES * a, r0 + SUBLANES * a + rc)
                tap = buf_s[rows, l0:l0 + lc] if r == 0 else sh_s[r - 1, rows, l0:l0 + lc]
                acc = acc + tap * w_ref[k:k + 1, l0:l0 + lc]
            y_s[r0:r0 + rc, l0:l0 + lc] = acc
    yn = _layer_norm(y_s[...], g_ref[...], b_ref[...])
    o_ref[0] = _silu(yn).astype(BF16)


def _conformer(proj, conv_w, conv_b, norm_g, norm_b, ts=256):
    bsz, seq, _ = proj.shape
    width, ch = conv_w.shape
    assert width - 1 <= CONV_HALO
    ts = min(ts, seq)
    wpad = jnp.zeros((CONV_HALO, ch), F32).at[:width].set(conv_w)
    hb = ts // CONV_HALO
    return pl.pallas_call(
        functools.partial(_conf_kernel, width=width),
        grid=(bsz, seq // ts),
        in_specs=[pl.BlockSpec((1, ts, 2 * ch), lambda b, i: (b, i, 0)),
                  pl.BlockSpec((1, CONV_HALO, 2 * ch), lambda b, i: (b, jnp.maximum(i * hb - 1, 0), 0)),
                  pl.BlockSpec((CONV_HALO, ch), lambda b, i: (0, 0)),
                  pl.BlockSpec((1, ch), lambda b, i: (0, 0)),
                  pl.BlockSpec((1, ch), lambda b, i: (0, 0)),
                  pl.BlockSpec((1, ch), lambda b, i: (0, 0))],
        out_specs=pl.BlockSpec((1, ts, ch), lambda b, i: (b, i, 0)),
        out_shape=jax.ShapeDtypeStruct((bsz, seq, ch), BF16),
        scratch_shapes=[pltpu.VMEM((CONV_HALO + ts, ch), F32),
                        pltpu.VMEM((SUBLANES - 1, CONV_HALO + ts - SUBLANES, ch), F32),
                        pltpu.VMEM((ts, ch), F32)],
        compiler_params=pltpu.CompilerParams(dimension_semantics=("parallel", "parallel")),
        name="conformer",
    )(proj, proj, wpad, conv_b.reshape(1, ch), norm_g.reshape(1, ch), norm_b.reshape(1, ch))


def _ssd_kernel(z_ref, xs_ref, bc_ref, sm_ref, cw_ref, cbias_ref, dtb_ref, alog_ref, dfull_ref, nw_ref, ex_ref,
                o_ref, buf_s, st_s, y_s, *, conv_width):
    c = pl.program_id(1)
    lc, di = y_s.shape
    n_grp, n_state, gw = st_s.shape
    hpg = gw // SSM_HEAD_DIM
    carry = 8

    @pl.when(c == 0)
    def _():
        buf_s[0:carry, :] = jnp.zeros((carry, buf_s.shape[1]), F32)
        st_s[...] = jnp.zeros_like(st_s)

    @pl.when(c > 0)
    def _():
        buf_s[0:carry, :] = buf_s[lc:lc + carry, :]

    buf_s[carry:, 0:di] = xs_ref[0].astype(F32)
    buf_s[carry:, di:] = bc_ref[0].astype(F32)
    off = carry - (conv_width - 1)
    acc = jnp.zeros((lc, buf_s.shape[1]), F32) + cbias_ref[...]
    for k in range(conv_width):
        acc = acc + buf_s[k + off:k + off + lc, :] * cw_ref[k:k + 1, :]
    xbc = _silu(acc)
    xs = xbc[:, 0:di]
    gn = n_grp * n_state
    bm = xbc[:, di:di + gn]
    cm = xbc[:, di + gn:di + 2 * gn]

    x_dt = sm_ref[0].astype(F32) + dtb_ref[...]
    dt = jnp.maximum(x_dt, 0.0) + jnp.log(1.0 + jnp.exp(-jnp.abs(x_dt)))
    adt = dt * (-jnp.exp(alog_ref[...]))
    row = lax.broadcasted_iota(jnp.int32, (lc, lc), 0)
    col = lax.broadcasted_iota(jnp.int32, (lc, lc), 1)
    causal = row >= col
    acs = _dot(causal.astype(F32), adt, HIGHEST)
    acs_t = acs.T
    ex = ex_ref[...]
    acs_full = _dot(acs, ex, HIGHEST)
    dt_full = _dot(dt, ex, HIGHEST)
    last_full = acs_full[lc - 1:lc, :]
    xdt = xs * dt_full
    e_acs = jnp.exp(acs_full)
    x_in = (xdt * jnp.exp(last_full - acs_full)).astype(BF16)
    chunk_decay = jnp.exp(last_full)
    lane_head = lax.broadcasted_iota(jnp.int32, (lc, gw), 1) // SSM_HEAD_DIM

    for g in range(n_grp):
        bg = bm[:, g * n_state:(g + 1) * n_state]
        cg = cm[:, g * n_state:(g + 1) * n_state].astype(BF16)
        cb = _dot_nt(cg, bg.astype(BF16))
        hg = st_s[g]
        y_g = _dot(cg, hg.astype(BF16)) * e_acs[:, g * gw:(g + 1) * gw]
        xg = xdt[:, g * gw:(g + 1) * gw]
        for r in range(hpg):
            hd = g * hpg + r
            seg = jnp.exp(jnp.where(causal, acs[:, hd:hd + 1] - acs_t[hd:hd + 1, :], NEG_INF))
            xr = jnp.where(lane_head == r, xg, 0.0).astype(BF16)
            y_g = y_g + _dot((cb * seg).astype(BF16), xr)
        st_s[g] = hg * chunk_decay[:, g * gw:(g + 1) * gw] + _dot(bg.T.astype(BF16), x_in[:, g * gw:(g + 1) * gw])
        y_s[:, g * gw:(g + 1) * gw] = y_g

    y = y_s[...] + dfull_ref[...] * xs
    y = y * _silu(z_ref[0].astype(F32))
    y = y * lax.rsqrt(jnp.mean(y * y, axis=-1, keepdims=True) + LN_EPS) * nw_ref[...]
    o_ref[0] = y.astype(BF16)


def _ssd(proj, col_z, col_xs, col_bc, col_small, conv_w, conv_b, dt_bias, a_log, d_skip, norm_w):
    bsz, seq, _ = proj.shape
    n_heads = dt_bias.shape[0]
    di = n_heads * SSM_HEAD_DIM
    conv_width, conv_dim = conv_w.shape
    gn = SSM_GROUPS * SSM_STATE
    assert conv_dim == di + 2 * gn and 2 * gn == di
    lc = math.gcd(SSM_CHUNK, seq)
    gw = di // SSM_GROUPS
    cw = jnp.zeros((8, conv_dim), F32).at[:conv_width].set(conv_w)
    pad = lambda v: jnp.zeros((1, LANES), F32).at[0, :n_heads].set(v)
    expand = np.zeros((LANES, di), np.float32)
    expand[np.arange(di) // SSM_HEAD_DIM, np.arange(di)] = 1.0
    d_full = jnp.repeat(d_skip, SSM_HEAD_DIM).reshape(1, di)
    cblk = lambda col, w: pl.BlockSpec((1, lc, w), lambda b, c: (b, c, col // w))
    const = lambda shape: pl.BlockSpec(shape, lambda b, c: (0,) * len(shape))
    return pl.pallas_call(
        functools.partial(_ssd_kernel, conv_width=conv_width),
        grid=(bsz, seq // lc),
        in_specs=[cblk(col_z, di), cblk(col_xs, di), cblk(col_bc, di), cblk(col_small, LANES),
                  const((8, conv_dim)), const((1, conv_dim)), const((1, LANES)), const((1, LANES)),
                  const((1, di)), const((1, di)), const((LANES, di))],
        out_specs=pl.BlockSpec((1, lc, di), lambda b, c: (b, c, 0)),
        out_shape=jax.ShapeDtypeStruct((bsz, seq, di), BF16),
        scratch_shapes=[pltpu.VMEM((8 + lc, conv_dim), F32),
                        pltpu.VMEM((SSM_GROUPS, SSM_STATE, gw), F32),
                        pltpu.VMEM((lc, di), F32)],
        compiler_params=pltpu.CompilerParams(dimension_semantics=("parallel", "arbitrary")),
        name="ssd",
    )(proj, proj, proj, proj, cw, conv_b.reshape(1, conv_dim), pad(dt_bias), pad(a_log), d_full,
      norm_w.reshape(1, di), jnp.asarray(expand))


def _rope_tables(pos):
    half = NSA_HEAD_DIM // 2
    inv_freq = ROPE_THETA ** (-np.arange(half, dtype=np.float32) / half)
    ang = jnp.asarray(pos, F32)[:, None] * jnp.asarray(inv_freq)[None, :]
    cos, sin, zero = jnp.cos(ang), jnp.sin(ang), jnp.zeros_like(ang)
    cos_t = jnp.concatenate([cos, cos, cos, cos], -1)
    sin_a = jnp.concatenate([-sin, zero, -sin, zero], -1)
    sin_b = jnp.concatenate([zero, sin, zero, sin], -1)
    return cos_t, sin_a, sin_b


def _rope(x, cos_t, sin_a, sin_b):
    half = NSA_HEAD_DIM // 2
    return x * cos_t + pltpu.roll(x, LANES - half, 1) * sin_a + pltpu.roll(x, half, 1) * sin_b


def _pair_slabs(x):
    return x, pltpu.roll(x, NSA_HEAD_DIM, 1)


def _nsa_prep_kernel(q_ref, kv_ref, cos_ref, sa_ref, sb_ref, qo_ref, ka_ref, kw_ref, vs_ref, vw_ref,
                     *, q_scale, kvd):
    i = pl.program_id(1)
    ts = q_ref.shape[1]
    hd = NSA_HEAD_DIM
    cos_t, sin_a, sin_b = cos_ref[...], sa_ref[...], sb_ref[...]
    lane = lax.broadcasted_iota(jnp.int32, (ts, LANES), 1)
    low = lane < hd
    for j in range(q_ref.shape[2] // LANES):
        xq = _rope(q_ref[0, :, j * LANES:(j + 1) * LANES].astype(F32), cos_t, sin_a, sin_b) * q_scale
        for u, slab in enumerate(_pair_slabs(xq)):
            qo_ref[0, 2 * j + u] = jnp.where(low, slab, 0.0).astype(BF16)
    blk = (i * ts + lax.broadcasted_iota(jnp.int32, (ts, LANES), 0)) // SEL_BLOCK
    onehot = (blk == lane - hd).astype(F32)
    slab_of = lambda c0, j: kv_ref[0, :, c0 + j * LANES:c0 + (j + 1) * LANES].astype(F32)
    for j in range(kvd // LANES):
        for u, slab in enumerate(_pair_slabs(_rope(slab_of(2 * kvd, j), cos_t, sin_a, sin_b))):
            ka_ref[0, 2 * j + u] = jnp.where(low, slab, onehot).astype(BF16)
        for u, slab in enumerate(_pair_slabs(_rope(slab_of(4 * kvd, j), cos_t, sin_a, sin_b))):
            kw_ref[0, 2 * j + u] = slab.astype(BF16)
        for u, slab in enumerate(_pair_slabs(slab_of(3 * kvd, j))):
            vs_ref[0, 2 * j + u] = slab.astype(BF16)
        for u, slab in enumerate(_pair_slabs(slab_of(5 * kvd, j))):
            vw_ref[0, 2 * j + u] = slab.astype(BF16)


def _nsa_prep(proj, col_q, col_kv, n_q, kvd, ts=512):
    bsz, seq, _ = proj.shape
    hd = NSA_HEAD_DIM
    ts = min(ts, seq)
    n_heads, n_kv = n_q // hd, kvd // hd
    assert 2 * hd == LANES and seq // SEL_BLOCK <= LANES - hd
    cos_t, sin_a, sin_b = _rope_tables(np.arange(seq))
    tab = pl.BlockSpec((ts, LANES), lambda b, i: (i, 0))
    per_head = lambda n: pl.BlockSpec((1, n, ts, LANES), lambda b, i: (b, 0, i, 0))
    return pl.pallas_call(
        functools.partial(_nsa_prep_kernel, q_scale=hd ** -0.5, kvd=kvd),
        grid=(bsz, seq // ts),
        in_specs=[pl.BlockSpec((1, ts, n_q), lambda b, i: (b, i, col_q // n_q)),
                  pl.BlockSpec((1, ts, 6 * kvd), lambda b, i: (b, i, col_kv // (6 * kvd))),
                  tab, tab, tab],
        out_specs=[per_head(n_heads)] + [per_head(n_kv)] * 4,
        out_shape=[jax.ShapeDtypeStruct((bsz, n_heads, seq, LANES), BF16)]
        + [jax.ShapeDtypeStruct((bsz, n_kv, seq, LANES), BF16)] * 4,
        compiler_params=pltpu.CompilerParams(dimension_semantics=("parallel", "parallel")),
        name="nsa_prep",
    )(proj, proj, cos_t, sin_a, sin_b)


def _gelu_tanh(x):
    return 0.5 * x * (1.0 + jnp.tanh(math.sqrt(2.0 / math.pi) * (x + 0.044715 * (x * x * x))))


def _compress_kernel(rk_ref, rv_ref, w1k_ref, w1v_ref, w2k_ref, w2v_ref, pek_ref, pev_ref,
                     cos_ref, sa_ref, sb_ref, ko_ref, vo_ref):
    n_grp, n_rows, half_in = rk_ref.shape[1:]

    def branch(r_ref, w1_ref, w2_ref, pe_ref):
        w1 = w1_ref[...]
        pe_term = _dot(pe_ref[...], w1)[0:1]
        hid = []
        for g in range(n_grp):
            r = r_ref[0, g]
            lo = _dot(r, w1[:half_in])
            hi = _dot(r, w1[half_in:])
            hid.append(_gelu_tanh(lo + pltpu.roll(hi, n_rows - 1, 0) + pe_term).astype(BF16))
        return _dot(jnp.concatenate(hid, axis=1), w2_ref[...])

    k = branch(rk_ref, w1k_ref, w2k_ref, pek_ref)
    v = branch(rv_ref, w1v_ref, w2v_ref, pev_ref)
    cos_t, sin_a, sin_b = cos_ref[...], sa_ref[...], sb_ref[...]
    for j in range(k.shape[1] // LANES):
        for u, slab in enumerate(_pair_slabs(_rope(k[:, j * LANES:(j + 1) * LANES], cos_t, sin_a, sin_b))):
            ko_ref[0, 2 * j + u] = slab.astype(BF16)
        for u, slab in enumerate(_pair_slabs(v[:, j * LANES:(j + 1) * LANES])):
            vo_ref[0, 2 * j + u] = slab.astype(BF16)


def _compress(k_c, v_c, pe_k, pe_v, k_w1, k_w2, v_w1, v_w2):
    bsz, seq, kvd = k_c.shape
    n_kv = kvd // NSA_HEAD_DIM
    assert CMP_LEN == 2 * CMP_STRIDE and seq % CMP_STRIDE == 0
    n_rows = seq // CMP_STRIDE
    half_in = CMP_STRIDE * NSA_HEAD_DIM
    hidden = k_w1.shape[1]

    def rows(a):
        return a.reshape(bsz, n_rows, CMP_STRIDE, n_kv, NSA_HEAD_DIM).transpose(0, 3, 1, 2, 4).reshape(
            bsz, n_kv, n_rows, half_in)

    blockdiag = lambda w: jnp.kron(jnp.eye(n_kv, dtype=F32), w).astype(BF16)
    pe_rows = lambda pe: jnp.broadcast_to(pe.reshape(1, -1), (8, CMP_LEN * NSA_HEAD_DIM)).astype(BF16)
    cos_t, sin_a, sin_b = _rope_tables(np.arange(n_rows) * CMP_STRIDE + CMP_LEN - 1)
    const = lambda shape: pl.BlockSpec(shape, lambda b: (0,) * len(shape))
    rspec = pl.BlockSpec((1, n_kv, n_rows, half_in), lambda b: (b, 0, 0, 0))
    ospec = pl.BlockSpec((1, n_kv, n_rows, LANES), lambda b: (b, 0, 0, 0))
    return pl.pallas_call(
        _compress_kernel,
        grid=(bsz,),
        in_specs=[rspec, rspec, const((2 * half_in, hidden)), const((2 * half_in, hidden)),
                  const((n_kv * hidden, kvd)), const((n_kv * hidden, kvd)),
                  const((8, 2 * half_in)), const((8, 2 * half_in)),
                  const((n_rows, LANES)), const((n_rows, LANES)), const((n_rows, LANES))],
        out_specs=[ospec, ospec],
        out_shape=[jax.ShapeDtypeStruct((bsz, n_kv, n_rows, LANES), BF16)] * 2,
        compiler_params=pltpu.CompilerParams(dimension_semantics=("parallel",)),
        name="compress",
    )(rows(k_c), rows(v_c), k_w1.astype(BF16), v_w1.astype(BF16), blockdiag(k_w2), blockdiag(v_w2),
      pe_rows(pe_k), pe_rows(pe_v), cos_t, sin_a, sin_b)


def _dot_tn(a, b):
    return lax.dot_general(a, b, (((0,), (0,)), ((), ())), preferred_element_type=F32)


def _nsa_kernel(q_ref, kc_ref, vc_ref, ka_ref, vs_ref, kw_ref, vw_ref, gl_ref, crel_ref, wrel_ref, o_ref,
                psum_s, imp_s):
    i = pl.program_id(2)
    gqa, qt = q_ref.shape[1:3]
    hd = NSA_HEAD_DIM
    cols = gqa * qt
    n_blk = imp_s.shape[0]
    n_rows = kc_ref.shape[2]
    band = WINDOW + qt
    qp = q_ref[0].reshape(cols, LANES)
    lane_q = lax.broadcasted_iota(jnp.int32, (1, cols), 1) % qt
    t0 = i * qt
    q0 = pl.multiple_of(t0, qt)
    w0 = pl.multiple_of(jnp.maximum(t0 - WINDOW, 0), qt)

    k_all = jnp.concatenate([kc_ref[0, 0], kw_ref[0, 0, pl.ds(w0, band), :], ka_ref[0, 0, pl.ds(q0, qt), :]], axis=0)
    s_all = _dot_nt(k_all, qp)
    s_c, s_w, s_d = s_all[:n_rows], s_all[n_rows:n_rows + band], s_all[n_rows + band:]

    cmask = crel_ref[...] <= t0
    s_c = jnp.where(cmask, s_c, NEG_INF)
    e_c = jnp.exp(s_c - jnp.max(s_c, axis=0, keepdims=True))
    has_cmp = t0 + lane_q >= CMP_LEN - 1
    p_cmp = e_c * jnp.where(has_cmp, 1.0 / jnp.sum(e_c, axis=0, keepdims=True), 0.0)
    o_cmp = _dot_tn(vc_ref[0, 0], p_cmp.astype(BF16))[:hd]

    w_rel = wrel_ref[...]
    wmask = (w_rel <= t0 - w0) & (w_rel > t0 - w0 - WINDOW)
    s_w = jnp.where(wmask, s_w, NEG_INF)
    e_w = jnp.exp(s_w - jnp.max(s_w, axis=0, keepdims=True))
    p_win = e_w * (1.0 / jnp.sum(e_w, axis=0, keepdims=True))
    o_win = _dot_tn(vw_ref[0, 0, pl.ds(w0, band), :], p_win.astype(BF16))[:hd]

    s_d = jnp.where(lax.broadcasted_iota(jnp.int32, (qt, cols), 0) <= lane_q, s_d, NEG_INF)
    m0 = jnp.max(s_d, axis=0, keepdims=True)
    p_d = jnp.exp(s_d - m0)
    l0 = jnp.sum(p_d, axis=0, keepdims=True)
    acc0 = _dot_tn(vs_ref[0, 0, pl.ds(q0, qt), :], p_d.astype(BF16))[:hd]

    p_sum = p_cmp[:, 0:qt]
    for r in range(1, gqa):
        p_sum = p_sum + p_cmp[:, r * qt:(r + 1) * qt]
    psum_s[0:IMP_PAD, :] = jnp.zeros((IMP_PAD, qt), F32)
    psum_s[IMP_PAD:, :] = p_sum
    ratio = SEL_BLOCK // CMP_STRIDE
    imp = psum_s[pl.ds(IMP_PAD - 1, n_blk, stride=ratio), :]
    for k in range(ratio):
        imp = imp + psum_s[pl.ds(IMP_PAD + k, n_blk, stride=ratio), :]
    blk = lax.broadcasted_iota(jnp.int32, (n_blk, qt), 0)
    tq = t0 + lax.broadcasted_iota(jnp.int32, (n_blk, qt), 1)
    cur = tq // SEL_BLOCK
    forced = (blk == 0) | (blk == cur) | (blk == cur - 1)
    imp = jnp.where(forced, FORCED_SCORE, jnp.where(blk * SEL_BLOCK > tq, -1.0, imp))
    imp_s[...] = imp

    n_live = (t0 + qt - 1) // SEL_BLOCK + 1

    def rank_body(jj, cnt):
        for u in range(RANK_UNROLL):
            j = jj * RANK_UNROLL + u
            other = imp_s[pl.ds(j, 1), :]
            ahead = (other > imp) | ((other == imp) & (j < blk))
            cnt = cnt + ahead.astype(F32)
        return cnt

    trips = jnp.where(n_live > N_SELECT, n_live // RANK_UNROLL, 0)
    rank = lax.fori_loop(0, trips, rank_body, jnp.zeros((n_blk, qt), F32))
    sel_bias = jnp.where((rank < N_SELECT) & (blk * SEL_BLOCK < t0), 0.0, NEG_INF)
    pad_rows = [jnp.zeros((LANES - hd - n_blk, qt), F32)] if LANES - hd > n_blk else []
    bias_t = jnp.concatenate([jnp.zeros((hd, qt), F32), sel_bias] + pad_rows, axis=0).T
    qa = (qp.astype(F32) + jnp.concatenate([bias_t] * gqa, axis=0)).astype(BF16)

    def sel_body(c, carry):
        m, l, acc = carry
        k0 = pl.multiple_of(c * SEL_CHUNK, SEL_CHUNK)
        kk = ka_ref[0, 0, pl.ds(k0, SEL_CHUNK), :]
        vv = vs_ref[0, 0, pl.ds(k0, SEL_CHUNK), :]
        s = _dot_nt(kk, qa)
        p = jnp.exp(s - m)
        m_new = jnp.maximum(m, jnp.max(s, axis=0, keepdims=True))
        alpha = jnp.exp(m - m_new)
        l_lazy = alpha * (l + jnp.sum(p, axis=0, keepdims=True))
        acc_lazy = alpha * (acc + _dot_tn(vv, p.astype(BF16))[:hd])
        safe = jnp.max(m_new - m) <= LAZY_MAX_SLACK

        def exact():
            p2 = jnp.exp(_dot_nt(kk, qa) - m_new)
            return alpha * l + jnp.sum(p2, axis=0, keepdims=True), alpha * acc + _dot_tn(vv, p2.astype(BF16))[:hd]

        l, acc = lax.cond(safe, lambda: (l_lazy, acc_lazy), exact)
        return m_new, l, acc

    n_chunks = (t0 + SEL_CHUNK - 1) // SEL_CHUNK
    _, l_sel, acc_sel = lax.fori_loop(0, n_chunks, sel_body, (m0, l0, acc0))
    o_sel = acc_sel / l_sel

    gate = jax.nn.sigmoid(gl_ref[0, 0, 0].astype(F32))
    out = gate[0:1] * o_cmp + gate[1:2] * o_sel + gate[2:3] * o_win
    pairs = [jnp.concatenate([out[:, r * qt:(r + 1) * qt], out[:, (r + 1) * qt:(r + 2) * qt]], axis=0).T
             for r in range(0, gqa, 2)]
    o_ref[0] = jnp.concatenate(pairs, axis=1).astype(BF16)


def _nsa_attention(q_hm, kc, vc, ka, vs, kw, vw, gl):
    bsz, n_heads, seq, _ = q_hm.shape
    hd = NSA_HEAD_DIM
    n_kv = ka.shape[1]
    gqa = n_heads // n_kv
    n_rows = kc.shape[2]
    n_cmp = (seq - CMP_LEN) // CMP_STRIDE + 1
    nq = seq // Q_TILE
    n_blk = seq // SEL_BLOCK
    assert N_SELECT <= n_blk <= LANES - hd and Q_TILE == 2 * SEL_BLOCK == 2 * hd and gqa % 2 == 0
    assert seq % SEL_CHUNK == 0 and seq >= WINDOW + Q_TILE
    assert n_rows * CMP_STRIDE == seq and n_blk * (SEL_BLOCK // CMP_STRIDE) <= n_rows and CMP_LEN == 2 * CMP_STRIDE
    cols = gqa * Q_TILE
    gl2 = gl.reshape(bsz, nq, Q_TILE, n_kv, gqa, 3).transpose(0, 3, 1, 5, 4, 2).reshape(bsz, n_kv, nq, 3, cols)
    lane_q = np.arange(cols)[None, :] % Q_TILE
    n_idx = np.arange(n_rows)[:, None]
    crel = np.where(n_idx < n_cmp, n_idx * CMP_STRIDE + (CMP_LEN - 1) - lane_q, np.iinfo(np.int32).max).astype(np.int32)
    wrel = (np.arange(WINDOW + Q_TILE)[:, None] - lane_q).astype(np.int32)
    const = lambda a: pl.BlockSpec(a.shape, lambda b, g, i: (0, 0))
    kv_spec = lambda n: pl.BlockSpec((1, 1, n, LANES), lambda b, g, i: (b, g, 0, 0))
    return pl.pallas_call(
        _nsa_kernel,
        grid=(bsz, n_kv, nq),
        in_specs=[pl.BlockSpec((1, gqa, Q_TILE, LANES), lambda b, g, i: (b, g, i, 0)),
                  kv_spec(n_rows), kv_spec(n_rows), kv_spec(seq), kv_spec(seq), kv_spec(seq), kv_spec(seq),
                  pl.BlockSpec((1, 1, 1, 3, cols), lambda b, g, i: (b, g, i, 0, 0)),
                  const(crel), const(wrel)],
        out_specs=pl.BlockSpec((1, Q_TILE, gqa * hd), lambda b, g, i: (b, i, g)),
        out_shape=jax.ShapeDtypeStruct((bsz, seq, n_heads * hd), BF16),
        scratch_shapes=[pltpu.VMEM((IMP_PAD + n_rows, Q_TILE), F32), pltpu.VMEM((n_blk, Q_TILE), F32)],
        compiler_params=pltpu.CompilerParams(dimension_semantics=("parallel", "parallel", "arbitrary")),
        name="nsa_attention",
    )(q_hm, kc, vc, ka, vs, kw, vw, gl2, jnp.asarray(crel), jnp.asarray(wrel))


def _merge_kernel(h_ref, mod_ref, ya_ref, yb_ref, yc_ref, gl_ref, wa_ref, wb_ref, wc_ref, wo_ref,
                  lng_ref, lnb_ref, o_ref, *, alpha):
    d = h_ref.shape[2]
    gate = lambda k: jax.nn.sigmoid(gl_ref[0, :, k * d:(k + 1) * d].astype(F32))
    merged = gate(0) * _dot(ya_ref[0], wa_ref[...])
    merged = merged + gate(1) * _dot(yb_ref[0], wb_ref[...])
    merged = merged + gate(2) * _dot(yc_ref[0], wc_ref[...])
    y = _dot(merged.astype(BF16), wo_ref[...])
    z = alpha * h_ref[0] + mod_ref[0, 2:3, :] * y
    o_ref[0] = _layer_norm(z, lng_ref[...], lnb_ref[...])


def _merge(h, mod3, ya, yb, yc, proj, col_gate, wa, wb, wc, wo, ln_g, ln_b, alpha, tm=512):
    bsz, seq, d = h.shape
    tm = min(tm, seq)
    act = pl.BlockSpec((1, tm, d), lambda b, i: (b, i, 0))
    wspec = pl.BlockSpec((d, d), lambda b, i: (0, 0))
    vec = pl.BlockSpec((1, d), lambda b, i: (0, 0))
    return pl.pallas_call(
        functools.partial(_merge_kernel, alpha=alpha),
        grid=(bsz, seq // tm),
        in_specs=[act, pl.BlockSpec((1, 3, d), lambda b, i: (b, 0, 0)), act, act, act,
                  pl.BlockSpec((1, tm, 3 * d), lambda b, i: (b, i, col_gate // (3 * d))),
                  wspec, wspec, wspec, wspec, vec, vec],
        out_specs=act,
        out_shape=jax.ShapeDtypeStruct((bsz, seq, d), F32),
        compiler_params=pltpu.CompilerParams(dimension_semantics=("parallel", "parallel")),
        name="merge",
    )(h, mod3, ya, yb, yc, proj, wa, wb, wc, wo, ln_g.reshape(1, d), ln_b.reshape(1, d))


def _token_mixer(h, mod3, w_in, conv_a_w, conv_a_b, norm_a_g, norm_a_b, w_a_out,
                 ssm_conv_w, ssm_conv_b, ssm_dt_bias, ssm_a_log, ssm_d, ssm_norm_w, w_b_out,
                 cmp_pe_k, cmp_pe_v, cmp_k_w1, cmp_k_w2, cmp_v_w1, cmp_v_w2, w_c_out, w_o, ln_g, ln_b, alpha):
    bsz, seq, d = h.shape
    ch = conv_a_w.shape[1]
    n_ssm_heads = ssm_dt_bias.shape[0]
    di = n_ssm_heads * SSM_HEAD_DIM
    gn = SSM_GROUPS * SSM_STATE
    n_q = w_c_out.shape[0]
    n_heads = n_q // NSA_HEAD_DIM
    kvd = NSA_KV_HEADS * NSA_HEAD_DIM
    gqa = n_heads // NSA_KV_HEADS
    n_gate = 3 * n_heads
    o_dt = 2 * ch + 2 * di + 2 * gn
    o_q = o_dt + n_ssm_heads
    o_gl = o_q + n_q + 6 * kvd
    o_g = o_gl + n_gate
    assert w_in.shape[1] == o_g + 3 * d
    col_z, col_xs, col_bc = 2 * ch, 2 * ch + di, 2 * ch + 2 * di
    col_q = o_dt
    col_gate = col_q + n_q
    col_kv = col_gate + 3 * d
    col_small = col_kv + 6 * kvd
    assert n_ssm_heads + n_gate <= LANES
    w_perm = jnp.concatenate(
        [w_in[:, :o_dt], w_in[:, o_q:o_q + n_q], w_in[:, o_g:], w_in[:, o_q + n_q:o_gl], w_in[:, o_dt:o_q],
         w_in[:, o_gl:o_g], jnp.zeros((d, LANES - n_ssm_heads - n_gate), F32)], axis=1).astype(BF16)

    proj = _inproj(h, mod3, w_perm)

    y_a = _conformer(proj, conv_a_w, conv_a_b, norm_a_g, norm_a_b)
    y_b = _ssd(proj, col_z, col_xs, col_bc, col_small, ssm_conv_w, ssm_conv_b, ssm_dt_bias, ssm_a_log,
               ssm_d, ssm_norm_w)

    q_hm, ka, kw, vs, vw = _nsa_prep(proj, col_q, col_kv, n_q, kvd)
    kv = lambda j: proj[:, :, col_kv + j * kvd:col_kv + (j + 1) * kvd]
    k_cmp, v_cmp = _compress(kv(0), kv(1), cmp_pe_k, cmp_pe_v, cmp_k_w1, cmp_k_w2, cmp_v_w1, cmp_v_w2)
    gl = proj[:, :, col_small + n_ssm_heads:col_small + n_ssm_heads + n_gate]
    y_c = _nsa_attention(q_hm, k_cmp, v_cmp, ka, vs, kw, vw, gl)

    return _merge(h, mod3, y_a, y_b, y_c, proj, col_gate, w_a_out.astype(BF16), w_b_out.astype(BF16),
                  w_c_out.astype(BF16), w_o.astype(BF16), ln_g, ln_b, alpha)


def kernel(x, c, ada_w, ada_b, ln_g, ln_b, ffn_w_gate, ffn_w_up, ffn_w_down, w_in, conv_a_w, conv_a_b, norm_a_g, norm_a_b, w_a_out, ssm_conv_w, ssm_conv_b, ssm_dt_bias, ssm_a_log, ssm_d, ssm_norm_w, w_b_out, cmp_pe_k, cmp_pe_v, cmp_k_w1, cmp_k_w2, cmp_v_w1, cmp_v_w2, w_c_out, w_o):
    depth = ada_w.shape[0]
    alpha = (2.0 * depth) ** 0.25
    mod = _ada_mod(c, ada_w, ada_b)
    wg, wu, wd = ffn_w_gate.astype(BF16), ffn_w_up.astype(BF16), ffn_w_down.astype(BF16)
    h = x
    for l in range(depth):
        h = _ffn_block(h, mod[l, :, 0:3], wg[l, 0], wu[l, 0], wd[l, 0], ln_g[l, 0], ln_b[l, 0], alpha)
        h = _token_mixer(h, mod[l, :, 3:6], w_in[l], conv_a_w[l], conv_a_b[l], norm_a_g[l], norm_a_b[l], w_a_out[l],
                         ssm_conv_w[l], ssm_conv_b[l], ssm_dt_bias[l], ssm_a_log[l], ssm_d[l], ssm_norm_w[l],
                         w_b_out[l], cmp_pe_k[l], cmp_pe_v[l], cmp_k_w1[l], cmp_k_w2[l], cmp_v_w1[l], cmp_v_w2[l],
                         w_c_out[l], w_o[l], ln_g[l, 1], ln_b[l, 1], alpha)
        h = _ffn_block(h, mod[l, :, 6:9], wg[l, 1], wu[l, 1], wd[l, 1], ln_g[l, 2], ln_b[l, 2], alpha)
    return h
```

```python
import functools
import math

import numpy as np
import jax
import jax.numpy as jnp
from jax import lax
from jax.experimental import pallas as pl
from jax.experimental.pallas import tpu as pltpu

F32 = jnp.float32
BF16 = jnp.bfloat16
HIGHEST = lax.Precision.HIGHEST

LN_EPS = 1e-5
NEG_INF = -1e30
MACARON_WEIGHT = 0.5

SSM_HEAD_DIM = 64
SSM_GROUPS = 4
SSM_STATE = 128
SSM_CHUNK = 256

NSA_HEAD_DIM = 64
NSA_KV_HEADS = 4
CMP_LEN = 32
CMP_STRIDE = 16
SEL_BLOCK = 64
N_SELECT = 16
WINDOW = 512
FORCED_SCORE = 1e4
ROPE_THETA = 10000.0

LANES = 128
SUBLANES = 8
SEL_CHUNK = 1024
LAZY_MAX_SLACK = 60.0
RANK_UNROLL = 2
Q_TILE = 128
IMP_PAD = 8
CONV_HALO = 32


def _dot(a, b, precision=None):
    return jnp.dot(a, b, preferred_element_type=F32, precision=precision)


def _dot_nt(a, b, precision=None):
    return lax.dot_general(a, b, (((1,), (1,)), ((), ())), preferred_element_type=F32, precision=precision)


def _layer_norm(z, g, b):
    mu = jnp.mean(z, axis=-1, keepdims=True)
    zc = z - mu
    var = jnp.mean(zc * zc, axis=-1, keepdims=True)
    return zc * lax.rsqrt(var + LN_EPS) * g + b


def _silu(x):
    return x * jax.nn.sigmoid(x)


def _ada_kernel(c_ref, w_ref, b_ref, o_ref):
    a = _silu(c_ref[...]).astype(BF16)
    o_ref[0] = _dot(a, w_ref[0].astype(BF16)) + b_ref[0]


def _ada_mod(c, ada_w, ada_b):
    n_layer, d, n = ada_w.shape
    bsz = c.shape[0]
    rows = 8
    cp = jnp.zeros((rows, d), F32).at[:bsz].set(c)
    tn = 1024
    out = pl.pallas_call(
        _ada_kernel,
        grid=(n_layer, n // tn),
        in_specs=[pl.BlockSpec((rows, d), lambda l, j: (0, 0)),
                  pl.BlockSpec((1, d, tn), lambda l, j: (l, 0, j)),
                  pl.BlockSpec((1, 1, tn), lambda l, j: (l, 0, j))],
        out_specs=pl.BlockSpec((1, rows, tn), lambda l, j: (l, 0, j)),
        out_shape=jax.ShapeDtypeStruct((n_layer, rows, n), F32),
        name="ada_mod",
    )(cp, ada_w, ada_b.reshape(n_layer, 1, n))
    return out[:, :bsz].reshape(n_layer, bsz, 9, d)


def _ffn_kernel(h_ref, mod_ref, wg_ref, wu_ref, wd_ref, lng_ref, lnb_ref, o_ref, u_s, acc_s, *, alpha):
    f = pl.program_id(2)

    @pl.when(f == 0)
    def _():
        u_s[...] = (h_ref[0] * (1.0 + mod_ref[0, 1:2, :]) + mod_ref[0, 0:1, :]).astype(BF16)
        acc_s[...] = jnp.zeros_like(acc_s)

    u = u_s[...]
    g = _dot(u, wg_ref[...])
    up = _dot(u, wu_ref[...])
    a = (_silu(g) * up).astype(BF16)
    acc_s[...] += _dot(a, wd_ref[...])

    @pl.when(f == pl.num_programs(2) - 1)
    def _():
        z = alpha * h_ref[0] + (MACARON_WEIGHT * mod_ref[0, 2:3, :]) * acc_s[...]
        o_ref[0] = _layer_norm(z, lng_ref[...], lnb_ref[...])


def _ffn_block(h, mod3, wg, wu, wd, ln_g, ln_b, alpha, tm=512):
    bsz, seq, d = h.shape
    d_ff = wg.shape[1]
    tf = d_ff // 2 if (d_ff // 2) % LANES == 0 else d_ff
    tm = min(tm, seq)
    return pl.pallas_call(
        functools.partial(_ffn_kernel, alpha=alpha),
        grid=(bsz, seq // tm, d_ff // tf),
        in_specs=[pl.BlockSpec((1, tm, d), lambda b, i, f: (b, i, 0)),
                  pl.BlockSpec((1, 3, d), lambda b, i, f: (b, 0, 0)),
                  pl.BlockSpec((d, tf), lambda b, i, f: (0, f)),
                  pl.BlockSpec((d, tf), lambda b, i, f: (0, f)),
                  pl.BlockSpec((tf, d), lambda b, i, f: (f, 0)),
                  pl.BlockSpec((1, d), lambda b, i, f: (0, 0)),
                  pl.BlockSpec((1, d), lambda b, i, f: (0, 0))],
        out_specs=pl.BlockSpec((1, tm, d), lambda b, i, f: (b, i, 0)),
        out_shape=jax.ShapeDtypeStruct((bsz, seq, d), F32),
        scratch_shapes=[pltpu.VMEM((tm, d), BF16), pltpu.VMEM((tm, d), F32)],
        compiler_params=pltpu.CompilerParams(dimension_semantics=("parallel", "parallel", "arbitrary")),
        name="ffn",
    )(h, mod3, wg, wu, wd, ln_g.reshape(1, d), ln_b.reshape(1, d))


def _inproj_kernel(h_ref, mod_ref, w_ref, o_ref, u_s):
    @pl.when(pl.program_id(2) == 0)
    def _():
        u_s[...] = (h_ref[0] * (1.0 + mod_ref[0, 1:2, :]) + mod_ref[0, 0:1, :]).astype(BF16)

    o_ref[0] = _dot(u_s[...], w_ref[...]).astype(BF16)


def _inproj(h, mod3, w, tm=1024):
    bsz, seq, d = h.shape
    n = w.shape[1]
    tn = n // 5
    tm = min(tm, seq)
    return pl.pallas_call(
        _inproj_kernel,
        grid=(bsz, seq // tm, n // tn),
        in_specs=[pl.BlockSpec((1, tm, d), lambda b, i, j: (b, i, 0)),
                  pl.BlockSpec((1, 3, d), lambda b, i, j: (b, 0, 0)),
                  pl.BlockSpec((d, tn), lambda b, i, j: (0, j))],
        out_specs=pl.BlockSpec((1, tm, tn), lambda b, i, j: (b, i, j)),
        out_shape=jax.ShapeDtypeStruct((bsz, seq, n), BF16),
        scratch_shapes=[pltpu.VMEM((tm, d), BF16)],
        compiler_params=pltpu.CompilerParams(dimension_semantics=("parallel", "parallel", "arbitrary")),
        name="inproj",
    )(h, mod3, w)


def _conf_kernel(cur_ref, halo_ref, w_ref, cb_ref, g_ref, b_ref, o_ref, buf_s, sh_s, y_s, *, width):
    i = pl.program_id(1)
    ts, ch = y_s.shape
    x = cur_ref[0].astype(F32)
    buf_s[CONV_HALO:, :] = x[:, :ch] * jax.nn.sigmoid(x[:, ch:])
    xh = halo_ref[0].astype(F32)
    ah = xh[:, :ch] * jax.nn.sigmoid(xh[:, ch:])
    buf_s[0:CONV_HALO, :] = jnp.where(i > 0, ah, 0.0)
    off = CONV_HALO - (width - 1)
    n_sh = sh_s.shape[1]
    for r in range(1, SUBLANES):
        sh_s[r - 1] = buf_s[r:r + n_sh, :]
    rc, lc = 64, 256
    for r0 in range(0, ts, rc):
        for l0 in range(0, ch, lc):
            acc = jnp.zeros((rc, lc), F32) + cb_ref[:, l0:l0 + lc]
            for k in range(width):
                a, r = divmod(k + off, SUBLANES)
                rows = slice(r0 + SUBLANES * a, r0 + SUBLANES * a + rc)
                tap = buf_s[rows, l0:l0 + lc] if r == 0 else sh_s[r - 1, rows, l0:l0 + lc]
                acc = acc + tap * w_ref[k:k + 1, l0:l0 + lc]
            y_s[r0:r0 + rc, l0:l0 + lc] = acc
    yn = _layer_norm(y_s[...], g_ref[...], b_ref[...])
    o_ref[0] = _silu(yn).astype(BF16)


def _conformer(proj, conv_w, conv_b, norm_g, norm_b, ts=256):
    bsz, seq, _ = proj.shape
    width, ch = conv_w.shape
    assert width - 1 <= CONV_HALO
    ts = min(ts, seq)
    wpad = jnp.zeros((CONV_HALO, ch), F32).at[:width].set(conv_w)
    hb = ts // CONV_HALO
    return pl.pallas_call(
        functools.partial(_conf_kernel, width=width),
        grid=(bsz, seq // ts),
        in_specs=[pl.BlockSpec((1, ts, 2 * ch), lambda b, i: (b, i, 0)),
                  pl.BlockSpec((1, CONV_HALO, 2 * ch), lambda b, i: (b, jnp.maximum(i * hb - 1, 0), 0)),
                  pl.BlockSpec((CONV_HALO, ch), lambda b, i: (0, 0)),
                  pl.BlockSpec((1, ch), lambda b, i: (0, 0)),
                  pl.BlockSpec((1, ch), lambda b, i: (0, 0)),
                  pl.BlockSpec((1, ch), lambda b, i: (0, 0))],
        out_specs=pl.BlockSpec((1, ts, ch), lambda b, i: (b, i, 0)),
        out_shape=jax.ShapeDtypeStruct((bsz, seq, ch), BF16),
        scratch_shapes=[pltpu.VMEM((CONV_HALO + ts, ch), F32),
                        pltpu.VMEM((SUBLANES - 1, CONV_HALO + ts - SUBLANES, ch), F32),
                        pltpu.VMEM((ts, ch), F32)],
        compiler_params=pltpu.CompilerParams(dimension_semantics=("parallel", "parallel")),
        name="conformer",
    )(proj, proj, wpad, conv_b.reshape(1, ch), norm_g.reshape(1, ch), norm_b.reshape(1, ch))


def _ssd_kernel(z_ref, xs_ref, bc_ref, sm_ref, cw_ref, cbias_ref, dtb_ref, alog_ref, dfull_ref, nw_ref, ex_ref,
                o_ref, buf_s, st_s, y_s, *, conv_width):
    c = pl.program_id(1)
    lc, di = y_s.shape
    n_grp, n_state, gw = st_s.shape
    hpg = gw // SSM_HEAD_DIM
    carry = 8

    @pl.when(c == 0)
    def _():
        buf_s[0:carry, :] = jnp.zeros((carry, buf_s.shape[1]), F32)
        st_s[...] = jnp.zeros_like(st_s)

    @pl.when(c > 0)
    def _():
        buf_s[0:carry, :] = buf_s[lc:lc + carry, :]

    buf_s[carry:, 0:di] = xs_ref[0].astype(F32)
    buf_s[carry:, di:] = bc_ref[0].astype(F32)
    off = carry - (conv_width - 1)
    acc = jnp.zeros((lc, buf_s.shape[1]), F32) + cbias_ref[...]
    for k in range(conv_width):
        acc = acc + buf_s[k + off:k + off + lc, :] * cw_ref[k:k + 1, :]
    xbc = _silu(acc)
    xs = xbc[:, 0:di]
    gn = n_grp * n_state
    bm = xbc[:, di:di + gn]
    cm = xbc[:, di + gn:di + 2 * gn]

    x_dt = sm_ref[0].astype(F32) + dtb_ref[...]
    dt = jnp.maximum(x_dt, 0.0) + jnp.log(1.0 + jnp.exp(-jnp.abs(x_dt)))
    adt = dt * (-jnp.exp(alog_ref[...]))
    row = lax.broadcasted_iota(jnp.int32, (lc, lc), 0)
    col = lax.broadcasted_iota(jnp.int32, (lc, lc), 1)
    causal = row >= col
    acs = _dot(causal.astype(F32), adt, HIGHEST)
    acs_t = acs.T
    ex = ex_ref[...]
    acs_full = _dot(acs, ex, HIGHEST)
    dt_full = _dot(dt, ex, HIGHEST)
    last_full = acs_full[lc - 1:lc, :]
    xdt = xs * dt_full
    e_acs = jnp.exp(acs_full)
    x_in = (xdt * jnp.exp(last_full - acs_full)).astype(BF16)
    chunk_decay = jnp.exp(last_full)
    lane_head = lax.broadcasted_iota(jnp.int32, (lc, gw), 1) // SSM_HEAD_DIM

    for g in range(n_grp):
        bg = bm[:, g * n_state:(g + 1) * n_state]
        cg = cm[:, g * n_state:(g + 1) * n_state].astype(BF16)
        cb = _dot_nt(cg, bg.astype(BF16))
        hg = st_s[g]
        y_g = _dot(cg, hg.astype(BF16)) * e_acs[:, g * gw:(g + 1) * gw]
        xg = xdt[:, g * gw:(g + 1) * gw]
        for r in range(hpg):
            hd = g * hpg + r
            seg = jnp.exp(jnp.where(causal, acs[:, hd:hd + 1] - acs_t[hd:hd + 1, :], NEG_INF))
            xr = jnp.where(lane_head == r, xg, 0.0).astype(BF16)
            y_g = y_g + _dot((cb * seg).astype(BF16), xr)
        st_s[g] = hg * chunk_decay[:, g * gw:(g + 1) * gw] + _dot(bg.T.astype(BF16), x_in[:, g * gw:(g + 1) * gw])
        y_s[:, g * gw:(g + 1) * gw] = y_g

    y = y_s[...] + dfull_ref[...] * xs
    y = y * _silu(z_ref[0].astype(F32))
    y = y * lax.rsqrt(jnp.mean(y * y, axis=-1, keepdims=True) + LN_EPS) * nw_ref[...]
    o_ref[0] = y.astype(BF16)


def _ssd(proj, col_z, col_xs, col_bc, col_small, conv_w, conv_b, dt_bias, a_log, d_skip, norm_w):
    bsz, seq, _ = proj.shape
    n_heads = dt_bias.shape[0]
    di = n_heads * SSM_HEAD_DIM
    conv_width, conv_dim = conv_w.shape
    gn = SSM_GROUPS * SSM_STATE
    assert conv_dim == di + 2 * gn and 2 * gn == di
    lc = math.gcd(SSM_CHUNK, seq)
    gw = di // SSM_GROUPS
    cw = jnp.zeros((8, conv_dim), F32).at[:conv_width].set(conv_w)
    pad = lambda v: jnp.zeros((1, LANES), F32).at[0, :n_heads].set(v)
    expand = np.zeros((LANES, di), np.float32)
    expand[np.arange(di) // SSM_HEAD_DIM, np.arange(di)] = 1.0
    d_full = jnp.repeat(d_skip, SSM_HEAD_DIM).reshape(1, di)
    cblk = lambda col, w: pl.BlockSpec((1, lc, w), lambda b, c: (b, c, col // w))
    const = lambda shape: pl.BlockSpec(shape, lambda b, c: (0,) * len(shape))
    return pl.pallas_call(
        functools.partial(_ssd_kernel, conv_width=conv_width),
        grid=(bsz, seq // lc),
        in_specs=[cblk(col_z, di), cblk(col_xs, di), cblk(col_bc, di), cblk(col_small, LANES),
                  const((8, conv_dim)), const((1, conv_dim)), const((1, LANES)), const((1, LANES)),
                  const((1, di)), const((1, di)), const((LANES, di))],
        out_specs=pl.BlockSpec((1, lc, di), lambda b, c: (b, c, 0)),
        out_shape=jax.ShapeDtypeStruct((bsz, seq, di), BF16),
        scratch_shapes=[pltpu.VMEM((8 + lc, conv_dim), F32),
                        pltpu.VMEM((SSM_GROUPS, SSM_STATE, gw), F32),
                        pltpu.VMEM((lc, di), F32)],
        compiler_params=pltpu.CompilerParams(dimension_semantics=("parallel", "arbitrary")),
        name="ssd",
    )(proj, proj, proj, proj, cw, conv_b.reshape(1, conv_dim), pad(dt_bias), pad(a_log), d_full,
      norm_w.reshape(1, di), jnp.asarray(expand))


def _rope_tables(pos):
    half = NSA_HEAD_DIM // 2
    inv_freq = ROPE_THETA ** (-np.arange(half, dtype=np.float32) / half)
    ang = jnp.asarray(pos, F32)[:, None] * jnp.asarray(inv_freq)[None, :]
    cos, sin, zero = jnp.cos(ang), jnp.sin(ang), jnp.zeros_like(ang)
    cos_t = jnp.concatenate([cos, cos, cos, cos], -1)
    sin_a = jnp.concatenate([-sin, zero, -sin, zero], -1)
    sin_b = jnp.concatenate([zero, sin, zero, sin], -1)
    return cos_t, sin_a, sin_b


def _rope(x, cos_t, sin_a, sin_b):
    half = NSA_HEAD_DIM // 2
    return x * cos_t + pltpu.roll(x, LANES - half, 1) * sin_a + pltpu.roll(x, half, 1) * sin_b


def _pair_slabs(x):
    return x, pltpu.roll(x, NSA_HEAD_DIM, 1)


def _nsa_prep_kernel(q_ref, kv_ref, cos_ref, sa_ref, sb_ref, qo_ref, ka_ref, kw_ref, vs_ref, vw_ref,
                     *, q_scale, kvd):
    i = pl.program_id(1)
    ts = q_ref.shape[1]
    hd = NSA_HEAD_DIM
    cos_t, sin_a, sin_b = cos_ref[...], sa_ref[...], sb_ref[...]
    lane = lax.broadcasted_iota(jnp.int32, (ts, LANES), 1)
    low = lane < hd
    ones_lane = lane == hd
    for j in range(q_ref.shape[2] // LANES):
        xq = _rope(q_ref[0, :, j * LANES:(j + 1) * LANES].astype(F32), cos_t, sin_a, sin_b) * q_scale
        for u, slab in enumerate(_pair_slabs(xq)):
            qo_ref[0, 2 * j + u] = jnp.where(low, slab, 0.0).astype(BF16)
    blk = (i * ts + lax.broadcasted_iota(jnp.int32, (ts, LANES), 0)) // SEL_BLOCK
    onehot = (blk == lane - hd).astype(F32)
    slab_of = lambda c0, j: kv_ref[0, :, c0 + j * LANES:c0 + (j + 1) * LANES].astype(F32)
    for j in range(kvd // LANES):
        for u, slab in enumerate(_pair_slabs(_rope(slab_of(2 * kvd, j), cos_t, sin_a, sin_b))):
            ka_ref[0, 2 * j + u] = jnp.where(low, slab, onehot).astype(BF16)
        for u, slab in enumerate(_pair_slabs(_rope(slab_of(4 * kvd, j), cos_t, sin_a, sin_b))):
            kw_ref[0, 2 * j + u] = slab.astype(BF16)
        for u, slab in enumerate(_pair_slabs(slab_of(3 * kvd, j))):
            vs_ref[0, 2 * j + u] = jnp.where(ones_lane, 1.0, slab).astype(BF16)
        for u, slab in enumerate(_pair_slabs(slab_of(5 * kvd, j))):
            vw_ref[0, 2 * j + u] = jnp.where(ones_lane, 1.0, slab).astype(BF16)


def _nsa_prep(proj, col_q, col_kv, n_q, kvd, ts=512):
    bsz, seq, _ = proj.shape
    hd = NSA_HEAD_DIM
    ts = min(ts, seq)
    n_heads, n_kv = n_q // hd, kvd // hd
    assert 2 * hd == LANES and seq // SEL_BLOCK <= LANES - hd
    cos_t, sin_a, sin_b = _rope_tables(np.arange(seq))
    tab = pl.BlockSpec((ts, LANES), lambda b, i: (i, 0))
    per_head = lambda n: pl.BlockSpec((1, n, ts, LANES), lambda b, i: (b, 0, i, 0))
    return pl.pallas_call(
        functools.partial(_nsa_prep_kernel, q_scale=hd ** -0.5, kvd=kvd),
        grid=(bsz, seq // ts),
        in_specs=[pl.BlockSpec((1, ts, n_q), lambda b, i: (b, i, col_q // n_q)),
                  pl.BlockSpec((1, ts, 6 * kvd), lambda b, i: (b, i, col_kv // (6 * kvd))),
                  tab, tab, tab],
        out_specs=[per_head(n_heads)] + [per_head(n_kv)] * 4,
        out_shape=[jax.ShapeDtypeStruct((bsz, n_heads, seq, LANES), BF16)]
        + [jax.ShapeDtypeStruct((bsz, n_kv, seq, LANES), BF16)] * 4,
        compiler_params=pltpu.CompilerParams(dimension_semantics=("parallel", "parallel")),
        name="nsa_prep",
    )(proj, proj, cos_t, sin_a, sin_b)


def _gelu_tanh(x):
    return 0.5 * x * (1.0 + jnp.tanh(math.sqrt(2.0 / math.pi) * (x + 0.044715 * (x * x * x))))


def _compress_kernel(rk_ref, rv_ref, w1k_ref, w1v_ref, w2k_ref, w2v_ref, pek_ref, pev_ref,
                     cos_ref, sa_ref, sb_ref, ko_ref, vo_ref):
    n_grp, n_rows, half_in = rk_ref.shape[1:]

    def branch(r_ref, w1_ref, w2_ref, pe_ref):
        w1 = w1_ref[...]
        pe_term = _dot(pe_ref[...], w1)[0:1]
        hid = []
        for g in range(n_grp):
            r = r_ref[0, g]
            lo = _dot(r, w1[:half_in])
            hi = _dot(r, w1[half_in:])
            hid.append(_gelu_tanh(lo + pltpu.roll(hi, n_rows - 1, 0) + pe_term).astype(BF16))
        return _dot(jnp.concatenate(hid, axis=1), w2_ref[...])

    k = branch(rk_ref, w1k_ref, w2k_ref, pek_ref)
    v = branch(rv_ref, w1v_ref, w2v_ref, pev_ref)
    cos_t, sin_a, sin_b = cos_ref[...], sa_ref[...], sb_ref[...]
    for j in range(k.shape[1] // LANES):
        for u, slab in enumerate(_pair_slabs(_rope(k[:, j * LANES:(j + 1) * LANES], cos_t, sin_a, sin_b))):
            ko_ref[0, 2 * j + u] = slab.astype(BF16)
        for u, slab in enumerate(_pair_slabs(v[:, j * LANES:(j + 1) * LANES])):
            vo_ref[0, 2 * j + u] = slab.astype(BF16)


def _compress(k_c, v_c, pe_k, pe_v, k_w1, k_w2, v_w1, v_w2):
    bsz, seq, kvd = k_c.shape
    n_kv = kvd // NSA_HEAD_DIM
    assert CMP_LEN == 2 * CMP_STRIDE and seq % CMP_STRIDE == 0
    n_rows = seq // CMP_STRIDE
    half_in = CMP_STRIDE * NSA_HEAD_DIM
    hidden = k_w1.shape[1]

    def rows(a):
        return a.reshape(bsz, n_rows, CMP_STRIDE, n_kv, NSA_HEAD_DIM).transpose(0, 3, 1, 2, 4).reshape(
            bsz, n_kv, n_rows, half_in)

    blockdiag = lambda w: jnp.kron(jnp.eye(n_kv, dtype=F32), w).astype(BF16)
    pe_rows = lambda pe: jnp.broadcast_to(pe.reshape(1, -1), (8, CMP_LEN * NSA_HEAD_DIM)).astype(BF16)
    cos_t, sin_a, sin_b = _rope_tables(np.arange(n_rows) * CMP_STRIDE + CMP_LEN - 1)
    const = lambda shape: pl.BlockSpec(shape, lambda b: (0,) * len(shape))
    rspec = pl.BlockSpec((1, n_kv, n_rows, half_in), lambda b: (b, 0, 0, 0))
    ospec = pl.BlockSpec((1, n_kv, n_rows, LANES), lambda b: (b, 0, 0, 0))
    return pl.pallas_call(
        _compress_kernel,
        grid=(bsz,),
        in_specs=[rspec, rspec, const((2 * half_in, hidden)), const((2 * half_in, hidden)),
                  const((n_kv * hidden, kvd)), const((n_kv * hidden, kvd)),
                  const((8, 2 * half_in)), const((8, 2 * half_in)),
                  const((n_rows, LANES)), const((n_rows, LANES)), const((n_rows, LANES))],
        out_specs=[ospec, ospec],
        out_shape=[jax.ShapeDtypeStruct((bsz, n_kv, n_rows, LANES), BF16)] * 2,
        compiler_params=pltpu.CompilerParams(dimension_semantics=("parallel",)),
        name="compress",
    )(rows(k_c), rows(v_c), k_w1.astype(BF16), v_w1.astype(BF16), blockdiag(k_w2), blockdiag(v_w2),
      pe_rows(pe_k), pe_rows(pe_v), cos_t, sin_a, sin_b)


def _dot_tn(a, b):
    return lax.dot_general(a, b, (((0,), (0,)), ((), ())), preferred_element_type=F32)


def _nsa_kernel(q_ref, kc_ref, vc_ref, ka_ref, vs_ref, kw_ref, vw_ref, gl_ref, crel_ref, wrel_ref, o_ref,
                psum_s, imp_s):
    i = pl.program_id(2)
    gqa, qt = q_ref.shape[1:3]
    hd = NSA_HEAD_DIM
    cols = gqa * qt
    n_blk = imp_s.shape[0]
    n_rows = kc_ref.shape[2]
    band = WINDOW + qt
    qp = q_ref[0].reshape(cols, LANES)
    lane_q = lax.broadcasted_iota(jnp.int32, (1, cols), 1) % qt
    t0 = i * qt
    q0 = pl.multiple_of(t0, qt)
    w0 = pl.multiple_of(jnp.maximum(t0 - WINDOW, 0), qt)

    k_all = jnp.concatenate([kc_ref[0, 0], kw_ref[0, 0, pl.ds(w0, band), :], ka_ref[0, 0, pl.ds(q0, qt), :]], axis=0)
    s_all = _dot_nt(k_all, qp)
    s_c, s_w, s_d = s_all[:n_rows], s_all[n_rows:n_rows + band], s_all[n_rows + band:]

    cmask = crel_ref[...] <= t0
    s_c = jnp.where(cmask, s_c, NEG_INF)
    e_c = jnp.exp(s_c - jnp.max(s_c, axis=0, keepdims=True))
    has_cmp = t0 + lane_q >= CMP_LEN - 1
    p_cmp = e_c * jnp.where(has_cmp, 1.0 / jnp.sum(e_c, axis=0, keepdims=True), 0.0)
    o_cmp = _dot_tn(vc_ref[0, 0], p_cmp.astype(BF16))[:hd]

    w_rel = wrel_ref[...]
    wmask = (w_rel <= t0 - w0) & (w_rel > t0 - w0 - WINDOW)
    s_w = jnp.where(wmask, s_w, NEG_INF)
    e_w = jnp.exp((s_w - jnp.max(s_w, axis=0, keepdims=True)).astype(BF16))
    pv_w = _dot_tn(vw_ref[0, 0, pl.ds(w0, band), :], e_w)
    o_win = pv_w[:hd] * (1.0 / pv_w[hd:hd + 1])

    s_d = jnp.where(lax.broadcasted_iota(jnp.int32, (qt, cols), 0) <= lane_q, s_d, NEG_INF)
    m0 = jnp.max(s_d, axis=0, keepdims=True)
    pv_d = _dot_tn(vs_ref[0, 0, pl.ds(q0, qt), :], jnp.exp((s_d - m0).astype(BF16)))
    l0, acc0 = pv_d[hd:hd + 1], pv_d[:hd]

    p_sum = p_cmp[:, 0:qt]
    for r in range(1, gqa):
        p_sum = p_sum + p_cmp[:, r * qt:(r + 1) * qt]
    psum_s[0:IMP_PAD, :] = jnp.zeros((IMP_PAD, qt), F32)
    psum_s[IMP_PAD:, :] = p_sum
    ratio = SEL_BLOCK // CMP_STRIDE
    imp = psum_s[pl.ds(IMP_PAD - 1, n_blk, stride=ratio), :]
    for k in range(ratio):
        imp = imp + psum_s[pl.ds(IMP_PAD + k, n_blk, stride=ratio), :]
    blk = lax.broadcasted_iota(jnp.int32, (n_blk, qt), 0)
    tq = t0 + lax.broadcasted_iota(jnp.int32, (n_blk, qt), 1)
    cur = tq // SEL_BLOCK
    forced = (blk == 0) | (blk == cur) | (blk == cur - 1)
    imp = jnp.where(forced, FORCED_SCORE, jnp.where(blk * SEL_BLOCK > tq, -1.0, imp))
    imp_s[...] = imp

    n_live = (t0 + qt - 1) // SEL_BLOCK + 1

    def rank_body(jj, cnt):
        for u in range(RANK_UNROLL):
            j = jj * RANK_UNROLL + u
            other = imp_s[pl.ds(j, 1), :]
            ahead = (other > imp) | ((other == imp) & (j < blk))
            cnt = cnt + ahead.astype(F32)
        return cnt

    trips = jnp.where(n_live > N_SELECT, n_live // RANK_UNROLL, 0)
    rank = lax.fori_loop(0, trips, rank_body, jnp.zeros((n_blk, qt), F32))
    sel_bias = jnp.where((rank < N_SELECT) & (blk * SEL_BLOCK < t0), 0.0, NEG_INF)
    pad_rows = [jnp.zeros((LANES - hd - n_blk, qt), F32)] if LANES - hd > n_blk else []
    bias_t = jnp.concatenate([jnp.zeros((hd, qt), F32), sel_bias] + pad_rows, axis=0).T
    qa = (qp.astype(F32) + jnp.concatenate([bias_t] * gqa, axis=0)).astype(BF16)

    def sel_body(c, carry):
        m, l, acc = carry
        k0 = pl.multiple_of(c * SEL_CHUNK, SEL_CHUNK)
        kk = ka_ref[0, 0, pl.ds(k0, SEL_CHUNK), :]
        vv = vs_ref[0, 0, pl.ds(k0, SEL_CHUNK), :]
        s = _dot_nt(kk, qa)
        pv = _dot_tn(vv, jnp.exp((s - m).astype(BF16)))
        m_new = jnp.maximum(m, jnp.max(s, axis=0, keepdims=True))
        alpha = jnp.exp(m - m_new)
        l_lazy = alpha * (l + pv[hd:hd + 1])
        acc_lazy = alpha * (acc + pv[:hd])
        safe = jnp.max(m_new - m) <= LAZY_MAX_SLACK

        def exact():
            pv2 = _dot_tn(vv, jnp.exp((_dot_nt(kk, qa) - m_new).astype(BF16)))
            return alpha * l + pv2[hd:hd + 1], alpha * acc + pv2[:hd]

        l, acc = lax.cond(safe, lambda: (l_lazy, acc_lazy), exact)
        return m_new, l, acc

    n_chunks = (t0 + SEL_CHUNK - 1) // SEL_CHUNK
    _, l_sel, acc_sel = lax.fori_loop(0, n_chunks, sel_body, (m0, l0, acc0))
    o_sel = acc_sel / l_sel

    gate = jax.nn.sigmoid(gl_ref[0, 0, 0].astype(F32))
    out = gate[0:1] * o_cmp + gate[1:2] * o_sel + gate[2:3] * o_win
    pairs = [jnp.concatenate([out[:, r * qt:(r + 1) * qt], out[:, (r + 1) * qt:(r + 2) * qt]], axis=0).T
             for r in range(0, gqa, 2)]
    o_ref[0] = jnp.concatenate(pairs, axis=1).astype(BF16)


def _nsa_attention(q_hm, kc, vc, ka, vs, kw, vw, gl):
    bsz, n_heads, seq, _ = q_hm.shape
    hd = NSA_HEAD_DIM
    n_kv = ka.shape[1]
    gqa = n_heads // n_kv
    n_rows = kc.shape[2]
    n_cmp = (seq - CMP_LEN) // CMP_STRIDE + 1
    nq = seq // Q_TILE
    n_blk = seq // SEL_BLOCK
    assert N_SELECT <= n_blk <= LANES - hd and Q_TILE == 2 * SEL_BLOCK == 2 * hd and gqa % 2 == 0
    assert seq % SEL_CHUNK == 0 and seq >= WINDOW + Q_TILE
    assert n_rows * CMP_STRIDE == seq and n_blk * (SEL_BLOCK // CMP_STRIDE) <= n_rows and CMP_LEN == 2 * CMP_STRIDE
    cols = gqa * Q_TILE
    gl2 = gl.reshape(bsz, nq, Q_TILE, n_kv, gqa, 3).transpose(0, 3, 1, 5, 4, 2).reshape(bsz, n_kv, nq, 3, cols)
    lane_q = np.arange(cols)[None, :] % Q_TILE
    n_idx = np.arange(n_rows)[:, None]
    crel = np.where(n_idx < n_cmp, n_idx * CMP_STRIDE + (CMP_LEN - 1) - lane_q, np.iinfo(np.int32).max).astype(np.int32)
    wrel = (np.arange(WINDOW + Q_TILE)[:, None] - lane_q).astype(np.int32)
    const = lambda a: pl.BlockSpec(a.shape, lambda b, g, i: (0, 0))
    kv_spec = lambda n: pl.BlockSpec((1, 1, n, LANES), lambda b, g, i: (b, g, 0, 0))
    return pl.pallas_call(
        _nsa_kernel,
        grid=(bsz, n_kv, nq),
        in_specs=[pl.BlockSpec((1, gqa, Q_TILE, LANES), lambda b, g, i: (b, g, i, 0)),
                  kv_spec(n_rows), kv_spec(n_rows), kv_spec(seq), kv_spec(seq), kv_spec(seq), kv_spec(seq),
                  pl.BlockSpec((1, 1, 1, 3, cols), lambda b, g, i: (b, g, i, 0, 0)),
                  const(crel), const(wrel)],
        out_specs=pl.BlockSpec((1, Q_TILE, gqa * hd), lambda b, g, i: (b, i, g)),
        out_shape=jax.ShapeDtypeStruct((bsz, seq, n_heads * hd), BF16),
        scratch_shapes=[pltpu.VMEM((IMP_PAD + n_rows, Q_TILE), F32), pltpu.VMEM((n_blk, Q_TILE), F32)],
        compiler_params=pltpu.CompilerParams(dimension_semantics=("parallel", "parallel", "arbitrary")),
        name="nsa_attention",
    )(q_hm, kc, vc, ka, vs, kw, vw, gl2, jnp.asarray(crel), jnp.asarray(wrel))


def _merge_kernel(h_ref, mod_ref, ya_ref, yb_ref, yc_ref, gl_ref, wa_ref, wb_ref, wc_ref, wo_ref,
                  lng_ref, lnb_ref, o_ref, *, alpha):
    d = h_ref.shape[2]
    gate = lambda k: jax.nn.sigmoid(gl_ref[0, :, k * d:(k + 1) * d].astype(F32))
    merged = gate(0) * _dot(ya_ref[0], wa_ref[...])
    merged = merged + gate(1) * _dot(yb_ref[0], wb_ref[...])
    merged = merged + gate(2) * _dot(yc_ref[0], wc_ref[...])
    y = _dot(merged.astype(BF16), wo_ref[...])
    z = alpha * h_ref[0] + mod_ref[0, 2:3, :] * y
    o_ref[0] = _layer_norm(z, lng_ref[...], lnb_ref[...])


def _merge(h, mod3, ya, yb, yc, proj, col_gate, wa, wb, wc, wo, ln_g, ln_b, alpha, tm=512):
    bsz, seq, d = h.shape
    tm = min(tm, seq)
    act = pl.BlockSpec((1, tm, d), lambda b, i: (b, i, 0))
    wspec = pl.BlockSpec((d, d), lambda b, i: (0, 0))
    vec = pl.BlockSpec((1, d), lambda b, i: (0, 0))
    return pl.pallas_call(
        functools.partial(_merge_kernel, alpha=alpha),
        grid=(bsz, seq // tm),
        in_specs=[act, pl.BlockSpec((1, 3, d), lambda b, i: (b, 0, 0)), act, act, act,
                  pl.BlockSpec((1, tm, 3 * d), lambda b, i: (b, i, col_gate // (3 * d))),
                  wspec, wspec, wspec, wspec, vec, vec],
        out_specs=act,
        out_shape=jax.ShapeDtypeStruct((bsz, seq, d), F32),
        compiler_params=pltpu.CompilerParams(dimension_semantics=("parallel", "parallel")),
        name="merge",
    )(h, mod3, ya, yb, yc, proj, wa, wb, wc, wo, ln_g.reshape(1, d), ln_b.reshape(1, d))


def _token_mixer(h, mod3, w_in, conv_a_w, conv_a_b, norm_a_g, norm_a_b, w_a_out,
                 ssm_conv_w, ssm_conv_b, ssm_dt_bias, ssm_a_log, ssm_d, ssm_norm_w, w_b_out,
                 cmp_pe_k, cmp_pe_v, cmp_k_w1, cmp_k_w2, cmp_v_w1, cmp_v_w2, w_c_out, w_o, ln_g, ln_b, alpha):
    bsz, seq, d = h.shape
    ch = conv_a_w.shape[1]
    n_ssm_heads = ssm_dt_bias.shape[0]
    di = n_ssm_heads * SSM_HEAD_DIM
    gn = SSM_GROUPS * SSM_STATE
    n_q = w_c_out.shape[0]
    n_heads = n_q // NSA_HEAD_DIM
    kvd = NSA_KV_HEADS * NSA_HEAD_DIM
    gqa = n_heads // NSA_KV_HEADS
    n_gate = 3 * n_heads
    o_dt = 2 * ch + 2 * di + 2 * gn
    o_q = o_dt + n_ssm_heads
    o_gl = o_q + n_q + 6 * kvd
    o_g = o_gl + n_gate
    assert w_in.shape[1] == o_g + 3 * d
    col_z, col_xs, col_bc = 2 * ch, 2 * ch + di, 2 * ch + 2 * di
    col_q = o_dt
    col_gate = col_q + n_q
    col_kv = col_gate + 3 * d
    col_small = col_kv + 6 * kvd
    assert n_ssm_heads + n_gate <= LANES
    w_perm = jnp.concatenate(
        [w_in[:, :o_dt], w_in[:, o_q:o_q + n_q], w_in[:, o_g:], w_in[:, o_q + n_q:o_gl], w_in[:, o_dt:o_q],
         w_in[:, o_gl:o_g], jnp.zeros((d, LANES - n_ssm_heads - n_gate), F32)], axis=1).astype(BF16)

    proj = _inproj(h, mod3, w_perm)

    y_a = _conformer(proj, conv_a_w, conv_a_b, norm_a_g, norm_a_b)
    y_b = _ssd(proj, col_z, col_xs, col_bc, col_small, ssm_conv_w, ssm_conv_b, ssm_dt_bias, ssm_a_log,
               ssm_d, ssm_norm_w)

    q_hm, ka, kw, vs, vw = _nsa_prep(proj, col_q, col_kv, n_q, kvd)
    kv = lambda j: proj[:, :, col_kv + j * kvd:col_kv + (j + 1) * kvd]
    k_cmp, v_cmp = _compress(kv(0), kv(1), cmp_pe_k, cmp_pe_v, cmp_k_w1, cmp_k_w2, cmp_v_w1, cmp_v_w2)
    gl = proj[:, :, col_small + n_ssm_heads:col_small + n_ssm_heads + n_gate]
    y_c = _nsa_attention(q_hm, k_cmp, v_cmp, ka, vs, kw, vw, gl)

    return _merge(h, mod3, y_a, y_b, y_c, proj, col_gate, w_a_out.astype(BF16), w_b_out.astype(BF16),
                  w_c_out.astype(BF16), w_o.astype(BF16), ln_g, ln_b, alpha)


def kernel(x, c, ada_w, ada_b, ln_g, ln_b, ffn_w_gate, ffn_w_up, ffn_w_down, w_in, conv_a_w, conv_a_b, norm_a_g, norm_a_b, w_a_out, ssm_conv_w, ssm_conv_b, ssm_dt_bias, ssm_a_log, ssm_d, ssm_norm_w, w_b_out, cmp_pe_k, cmp_pe_v, cmp_k_w1, cmp_k_w2, cmp_v_w1, cmp_v_w2, w_c_out, w_o):
    depth = ada_w.shape[0]
    alpha = (2.0 * depth) ** 0.25
    mod = _ada_mod(c, ada_w, ada_b)
    wg, wu, wd = ffn_w_gate.astype(BF16), ffn_w_up.astype(BF16), ffn_w_down.astype(BF16)
    h = x
    for l in range(depth):
        h = _ffn_block(h, mod[l, :, 0:3], wg[l, 0], wu[l, 0], wd[l, 0], ln_g[l, 0], ln_b[l, 0], alpha)
        h = _token_mixer(h, mod[l, :, 3:6], w_in[l], conv_a_w[l], conv_a_b[l], norm_a_g[l], norm_a_b[l], w_a_out[l],
                         ssm_conv_w[l], ssm_conv_b[l], ssm_dt_bias[l], ssm_a_log[l], ssm_d[l], ssm_norm_w[l],
                         w_b_out[l], cmp_pe_k[l], cmp_pe_v[l], cmp_k_w1[l], cmp_k_w2[l], cmp_v_w1[l], cmp_v_w2[l],
                         w_c_out[l], w_o[l], ln_g[l, 1], ln_b[l, 1], alpha)
        h = _ffn_block(h, mod[l, :, 6:9], wg[l, 1], wu[l, 1], wd[l, 1], ln_g[l, 2], ln_b[l, 2], alpha)
    return h
```

```python
import functools
import math

import numpy as np
import jax
import jax.numpy as jnp
from jax import lax
from jax.experimental import pallas as pl
from jax.experimental.pallas import tpu as pltpu

F32 = jnp.float32
BF16 = jnp.bfloat16

LN_EPS = 1e-5
NEG_INF = -1e30
MACARON_WEIGHT = 0.5

SSM_HEAD_DIM = 64
SSM_GROUPS = 4
SSM_STATE = 128
SSM_CHUNK = 256

NSA_HEAD_DIM = 64
NSA_KV_HEADS = 4
CMP_LEN = 32
CMP_STRIDE = 16
SEL_BLOCK = 64
N_SELECT = 16
WINDOW = 512
FORCED_SCORE = 1e4
ROPE_THETA = 10000.0

LANES = 128
SUBLANES = 8
SEL_CHUNK = 1024
LAZY_MAX_SLACK = 60.0
RANK_UNROLL = 2
Q_TILE = 128
IMP_PAD = 8
CONV_HALO = 32


def _dot(a, b, precision=None):
    return jnp.dot(a, b, preferred_element_type=F32, precision=precision)


def _dot_nt(a, b, precision=None):
    return lax.dot_general(a, b, (((1,), (1,)), ((), ())), preferred_element_type=F32, precision=precision)


def _bf16_terms(x):
    hi = x.astype(BF16)
    r1 = x - hi.astype(F32)
    mid = r1.astype(BF16)
    lo = (r1 - mid.astype(F32)).astype(BF16)
    return hi, mid, lo


def _dot_exact_lhs(x, sel):
    hi, mid, lo = _bf16_terms(x)
    return _dot(hi, sel) + _dot(mid, sel) + _dot(lo, sel)


def _dot_exact_rhs(sel, x):
    hi, mid, lo = _bf16_terms(x)
    return _dot(sel, hi) + _dot(sel, mid) + _dot(sel, lo)


def _layer_norm(z, g, b):
    mu = jnp.mean(z, axis=-1, keepdims=True)
    zc = z - mu
    var = jnp.mean(zc * zc, axis=-1, keepdims=True)
    return zc * lax.rsqrt(var + LN_EPS) * g + b


def _silu(x):
    return x * jax.nn.sigmoid(x)


def _ada_kernel(c_ref, w_ref, b_ref, o_ref):
    a = _silu(c_ref[...]).astype(BF16)
    o_ref[0] = _dot(a, w_ref[0].astype(BF16)) + b_ref[0]


def _ada_mod(c, ada_w, ada_b):
    n_layer, d, n = ada_w.shape
    bsz = c.shape[0]
    rows = 8
    cp = jnp.zeros((rows, d), F32).at[:bsz].set(c)
    tn = 1024
    out = pl.pallas_call(
        _ada_kernel,
        grid=(n_layer, n // tn),
        in_specs=[pl.BlockSpec((rows, d), lambda l, j: (0, 0)),
                  pl.BlockSpec((1, d, tn), lambda l, j: (l, 0, j)),
                  pl.BlockSpec((1, 1, tn), lambda l, j: (l, 0, j))],
        out_specs=pl.BlockSpec((1, rows, tn), lambda l, j: (l, 0, j)),
        out_shape=jax.ShapeDtypeStruct((n_layer, rows, n), F32),
        name="ada_mod",
    )(cp, ada_w, ada_b.reshape(n_layer, 1, n))
    return out[:, :bsz].reshape(n_layer, bsz, 9, d)


def _ffn_kernel(h_ref, mod_ref, wg_ref, wu_ref, wd_ref, lng_ref, lnb_ref, o_ref, u_s, acc_s, *, alpha):
    f = pl.program_id(2)

    @pl.when(f == 0)
    def _():
        u_s[...] = (h_ref[0] * (1.0 + mod_ref[0, 1:2, :]) + mod_ref[0, 0:1, :]).astype(BF16)
        acc_s[...] = jnp.zeros_like(acc_s)

    u = u_s[...]
    g = _dot(u, wg_ref[...])
    up = _dot(u, wu_ref[...])
    a = (_silu(g) * up).astype(BF16)
    acc_s[...] += _dot(a, wd_ref[...])

    @pl.when(f == pl.num_programs(2) - 1)
    def _():
        z = alpha * h_ref[0] + (MACARON_WEIGHT * mod_ref[0, 2:3, :]) * acc_s[...]
        o_ref[0] = _layer_norm(z, lng_ref[...], lnb_ref[...])


def _ffn_block(h, mod3, wg, wu, wd, ln_g, ln_b, alpha, tm=1024):
    bsz, seq, d = h.shape
    d_ff = wg.shape[1]
    tf = d_ff // 2 if (d_ff // 2) % LANES == 0 else d_ff
    tm = min(tm, seq)
    return pl.pallas_call(
        functools.partial(_ffn_kernel, alpha=alpha),
        grid=(bsz, seq // tm, d_ff // tf),
        in_specs=[pl.BlockSpec((1, tm, d), lambda b, i, f: (b, i, 0)),
                  pl.BlockSpec((1, 3, d), lambda b, i, f: (b, 0, 0)),
                  pl.BlockSpec((d, tf), lambda b, i, f: (0, f)),
                  pl.BlockSpec((d, tf), lambda b, i, f: (0, f)),
                  pl.BlockSpec((tf, d), lambda b, i, f: (f, 0)),
                  pl.BlockSpec((1, d), lambda b, i, f: (0, 0)),
                  pl.BlockSpec((1, d), lambda b, i, f: (0, 0))],
        out_specs=pl.BlockSpec((1, tm, d), lambda b, i, f: (b, i, 0)),
        out_shape=jax.ShapeDtypeStruct((bsz, seq, d), F32),
        scratch_shapes=[pltpu.VMEM((tm, d), BF16), pltpu.VMEM((tm, d), F32)],
        compiler_params=pltpu.CompilerParams(dimension_semantics=("parallel", "parallel", "arbitrary")),
        name="ffn",
    )(h, mod3, wg, wu, wd, ln_g.reshape(1, d), ln_b.reshape(1, d))


def _inproj_kernel(h_ref, mod_ref, w_ref, o_ref, u_s):
    @pl.when(pl.program_id(2) == 0)
    def _():
        u_s[...] = (h_ref[0] * (1.0 + mod_ref[0, 1:2, :]) + mod_ref[0, 0:1, :]).astype(BF16)

    o_ref[0] = _dot(u_s[...], w_ref[...]).astype(BF16)


def _inproj(h, mod3, w, tm=1024):
    bsz, seq, d = h.shape
    n = w.shape[1]
    tn = n // 5
    tm = min(tm, seq)
    return pl.pallas_call(
        _inproj_kernel,
        grid=(bsz, seq // tm, n // tn),
        in_specs=[pl.BlockSpec((1, tm, d), lambda b, i, j: (b, i, 0)),
                  pl.BlockSpec((1, 3, d), lambda b, i, j: (b, 0, 0)),
                  pl.BlockSpec((d, tn), lambda b, i, j: (0, j))],
        out_specs=pl.BlockSpec((1, tm, tn), lambda b, i, j: (b, i, j)),
        out_shape=jax.ShapeDtypeStruct((bsz, seq, n), BF16),
        scratch_shapes=[pltpu.VMEM((tm, d), BF16)],
        compiler_params=pltpu.CompilerParams(dimension_semantics=("parallel", "parallel", "arbitrary")),
        name="inproj",
    )(h, mod3, w)


def _conf_kernel(cur_ref, halo_ref, w_ref, cb_ref, g_ref, b_ref, o_ref, buf_s, sh_s, y_s, *, width):
    i = pl.program_id(1)
    ts, ch = y_s.shape
    x = cur_ref[0].astype(F32)
    buf_s[CONV_HALO:, :] = x[:, :ch] * jax.nn.sigmoid(x[:, ch:])
    xh = halo_ref[0].astype(F32)
    ah = xh[:, :ch] * jax.nn.sigmoid(xh[:, ch:])
    buf_s[0:CONV_HALO, :] = jnp.where(i > 0, ah, 0.0)
    off = CONV_HALO - (width - 1)
    n_sh = sh_s.shape[1]
    for r in range(1, SUBLANES):
        sh_s[r - 1] = buf_s[r:r + n_sh, :]
    rc, lc = 64, 256
    for r0 in range(0, ts, rc):
        for l0 in range(0, ch, lc):
            acc = jnp.zeros((rc // SUBLANES, SUBLANES, lc), F32) + cb_ref[:, l0:l0 + lc]
            for k in range(width):
                a, r = divmod(k + off, SUBLANES)
                rows = slice(r0 + SUBLANES * a, r0 + SUBLANES * a + rc)
                tap = buf_s[rows, l0:l0 + lc] if r == 0 else sh_s[r - 1, rows, l0:l0 + lc]
                acc = acc + tap.reshape(rc // SUBLANES, SUBLANES, lc) * w_ref[k, :, l0:l0 + lc]
            y_s[r0:r0 + rc, l0:l0 + lc] = acc.reshape(rc, lc)
    yn = _layer_norm(y_s[...], g_ref[...], b_ref[...])
    o_ref[0] = _silu(yn).astype(BF16)


def _conformer(proj, conv_w, conv_b, norm_g, norm_b, ts=256):
    bsz, seq, _ = proj.shape
    width, ch = conv_w.shape
    assert width - 1 <= CONV_HALO
    ts = min(ts, seq)
    wpad = jnp.zeros((CONV_HALO, ch), F32).at[:width].set(conv_w)
    wpad = jnp.broadcast_to(wpad[:, None, :], (CONV_HALO, SUBLANES, ch))
    hb = ts // CONV_HALO
    return pl.pallas_call(
        functools.partial(_conf_kernel, width=width),
        grid=(bsz, seq // ts),
        in_specs=[pl.BlockSpec((1, ts, 2 * ch), lambda b, i: (b, i, 0)),
                  pl.BlockSpec((1, CONV_HALO, 2 * ch), lambda b, i: (b, jnp.maximum(i * hb - 1, 0), 0)),
                  pl.BlockSpec((CONV_HALO, SUBLANES, ch), lambda b, i: (0, 0, 0)),
                  pl.BlockSpec((1, ch), lambda b, i: (0, 0)),
                  pl.BlockSpec((1, ch), lambda b, i: (0, 0)),
                  pl.BlockSpec((1, ch), lambda b, i: (0, 0))],
        out_specs=pl.BlockSpec((1, ts, ch), lambda b, i: (b, i, 0)),
        out_shape=jax.ShapeDtypeStruct((bsz, seq, ch), BF16),
        scratch_shapes=[pltpu.VMEM((CONV_HALO + ts, ch), F32),
                        pltpu.VMEM((SUBLANES - 1, CONV_HALO + ts - SUBLANES, ch), F32),
                        pltpu.VMEM((ts, ch), F32)],
        compiler_params=pltpu.CompilerParams(dimension_semantics=("parallel", "parallel")),
        name="conformer",
    )(proj, proj, wpad, conv_b.reshape(1, ch), norm_g.reshape(1, ch), norm_b.reshape(1, ch))


def _ssd_kernel(z_ref, xs_ref, bc_ref, sm_ref, cw_ref, cbias_ref, dtb_ref, alog_ref, dfull_ref, nw_ref, ex_ref,
                o_ref, buf_s, st_s, y_s, *, conv_width):
    c = pl.program_id(1)
    lc, di = y_s.shape
    n_grp, n_state, gw = st_s.shape
    hpg = gw // SSM_HEAD_DIM
    carry = 8

    @pl.when(c == 0)
    def _():
        buf_s[0:carry, :] = jnp.zeros((carry, buf_s.shape[1]), F32)
        st_s[...] = jnp.zeros_like(st_s)

    @pl.when(c > 0)
    def _():
        buf_s[0:carry, :] = buf_s[lc:lc + carry, :]

    buf_s[carry:, 0:di] = xs_ref[0].astype(F32)
    buf_s[carry:, di:] = bc_ref[0].astype(F32)
    acc = jnp.zeros((lc, buf_s.shape[1]), F32) + cbias_ref[...]
    xb = buf_s[...]
    for k in range(conv_width):
        delay = conv_width - 1 - k
        tap = xb if delay == 0 else pltpu.roll(xb, delay, 0)
        acc = acc + tap[carry:, :] * cw_ref[k:k + 1, :]
    xbc = _silu(acc)
    xs = xbc[:, 0:di]
    gn = n_grp * n_state
    bm = xbc[:, di:di + gn]
    cm = xbc[:, di + gn:di + 2 * gn]

    x_dt = sm_ref[0].astype(F32) + dtb_ref[...]
    dt = jnp.maximum(x_dt, 0.0) + jnp.log(1.0 + jnp.exp(-jnp.abs(x_dt)))
    adt = dt * (-jnp.exp(alog_ref[...]))
    row = lax.broadcasted_iota(jnp.int32, (lc, lc), 0)
    col = lax.broadcasted_iota(jnp.int32, (lc, lc), 1)
    causal = row >= col
    acs = _dot_exact_rhs(causal.astype(BF16), adt)
    acs_t = acs.T
    ex = ex_ref[...]
    acs_full = _dot_exact_lhs(acs, ex)
    dt_full = _dot_exact_lhs(dt, ex)
    last_full = acs_full[lc - 1:lc, :]
    xdt = xs * dt_full
    e_acs = jnp.exp(acs_full)
    x_in = (xdt * jnp.exp(last_full - acs_full)).astype(BF16)
    chunk_decay = jnp.exp(last_full)
    lane_head = lax.broadcasted_iota(jnp.int32, (lc, gw), 1) // SSM_HEAD_DIM

    for g in range(n_grp):
        bg = bm[:, g * n_state:(g + 1) * n_state]
        cg = cm[:, g * n_state:(g + 1) * n_state].astype(BF16)
        cb = _dot_nt(cg, bg.astype(BF16))
        hg = st_s[g]
        y_g = _dot(cg, hg.astype(BF16)) * e_acs[:, g * gw:(g + 1) * gw]
        xg = xdt[:, g * gw:(g + 1) * gw]
        for r in range(hpg):
            hd = g * hpg + r
            seg = jnp.exp(jnp.where(causal, acs[:, hd:hd + 1] - acs_t[hd:hd + 1, :], NEG_INF))
            xr = jnp.where(lane_head == r, xg, 0.0).astype(BF16)
            y_g = y_g + _dot((cb * seg).astype(BF16), xr)
        st_s[g] = hg * chunk_decay[:, g * gw:(g + 1) * gw] + _dot(bg.T.astype(BF16), x_in[:, g * gw:(g + 1) * gw])
        y_s[:, g * gw:(g + 1) * gw] = y_g

    y = y_s[...] + dfull_ref[...] * xs
    y = y * _silu(z_ref[0].astype(F32))
    y = y * lax.rsqrt(jnp.mean(y * y, axis=-1, keepdims=True) + LN_EPS) * nw_ref[...]
    o_ref[0] = y.astype(BF16)


def _ssd(proj, col_z, col_xs, col_bc, col_small, conv_w, conv_b, dt_bias, a_log, d_skip, norm_w):
    bsz, seq, _ = proj.shape
    n_heads = dt_bias.shape[0]
    di = n_heads * SSM_HEAD_DIM
    conv_width, conv_dim = conv_w.shape
    gn = SSM_GROUPS * SSM_STATE
    assert conv_dim == di + 2 * gn and 2 * gn == di
    lc = math.gcd(SSM_CHUNK, seq)
    gw = di // SSM_GROUPS
    cw = jnp.zeros((8, conv_dim), F32).at[:conv_width].set(conv_w)
    pad = lambda v: jnp.zeros((1, LANES), F32).at[0, :n_heads].set(v)
    expand = np.zeros((LANES, di), np.float32)
    expand[np.arange(di) // SSM_HEAD_DIM, np.arange(di)] = 1.0
    d_full = jnp.repeat(d_skip, SSM_HEAD_DIM).reshape(1, di)
    cblk = lambda col, w: pl.BlockSpec((1, lc, w), lambda b, c: (b, c, col // w))
    const = lambda shape: pl.BlockSpec(shape, lambda b, c: (0,) * len(shape))
    return pl.pallas_call(
        functools.partial(_ssd_kernel, conv_width=conv_width),
        grid=(bsz, seq // lc),
        in_specs=[cblk(col_z, di), cblk(col_xs, di), cblk(col_bc, di), cblk(col_small, LANES),
                  const((8, conv_dim)), const((1, conv_dim)), const((1, LANES)), const((1, LANES)),
                  const((1, di)), const((1, di)), const((LANES, di))],
        out_specs=pl.BlockSpec((1, lc, di), lambda b, c: (b, c, 0)),
        out_shape=jax.ShapeDtypeStruct((bsz, seq, di), BF16),
        scratch_shapes=[pltpu.VMEM((8 + lc, conv_dim), F32),
                        pltpu.VMEM((SSM_GROUPS, SSM_STATE, gw), F32),
                        pltpu.VMEM((lc, di), F32)],
        compiler_params=pltpu.CompilerParams(dimension_semantics=("parallel", "arbitrary")),
        name="ssd",
    )(proj, proj, proj, proj, cw, conv_b.reshape(1, conv_dim), pad(dt_bias), pad(a_log), d_full,
      norm_w.reshape(1, di), jnp.asarray(expand, BF16))


def _rope_tables(pos):
    half = NSA_HEAD_DIM // 2
    inv_freq = ROPE_THETA ** (-np.arange(half, dtype=np.float32) / half)
    ang = jnp.asarray(pos, F32)[:, None] * jnp.asarray(inv_freq)[None, :]
    cos, sin, zero = jnp.cos(ang), jnp.sin(ang), jnp.zeros_like(ang)
    cos_t = jnp.concatenate([cos, cos, cos, cos], -1)
    sin_a = jnp.concatenate([-sin, zero, -sin, zero], -1)
    sin_b = jnp.concatenate([zero, sin, zero, sin], -1)
    return cos_t, sin_a, sin_b


def _rope(x, cos_t, sin_a, sin_b):
    half = NSA_HEAD_DIM // 2
    return x * cos_t + pltpu.roll(x, LANES - half, 1) * sin_a + pltpu.roll(x, half, 1) * sin_b


def _pair_slabs(x):
    return x, pltpu.roll(x, NSA_HEAD_DIM, 1)


def _nsa_prep_kernel(q_ref, kv_ref, cos_ref, sa_ref, sb_ref, qo_ref, ka_ref, kw_ref, vs_ref, vw_ref,
                     *, q_scale, kvd):
    i = pl.program_id(1)
    ts = q_ref.shape[1]
    hd = NSA_HEAD_DIM
    cos_t, sin_a, sin_b = cos_ref[...], sa_ref[...], sb_ref[...]
    lane = lax.broadcasted_iota(jnp.int32, (ts, LANES), 1)
    low = lane < hd
    ones_lane = lane == hd
    for j in range(q_ref.shape[2] // LANES):
        xq = _rope(q_ref[0, :, j * LANES:(j + 1) * LANES].astype(F32), cos_t, sin_a, sin_b) * q_scale
        for u, slab in enumerate(_pair_slabs(xq)):
            qo_ref[0, 2 * j + u] = jnp.where(low, slab, 0.0).astype(BF16)
    blk = (i * ts + lax.broadcasted_iota(jnp.int32, (ts, LANES), 0)) // SEL_BLOCK
    onehot = (blk == lane - hd).astype(F32)
    slab_of = lambda c0, j: kv_ref[0, :, c0 + j * LANES:c0 + (j + 1) * LANES].astype(F32)
    for j in range(kvd // LANES):
        for u, slab in enumerate(_pair_slabs(_rope(slab_of(2 * kvd, j), cos_t, sin_a, sin_b))):
            ka_ref[0, 2 * j + u] = jnp.where(low, slab, onehot).astype(BF16)
        for u, slab in enumerate(_pair_slabs(_rope(slab_of(4 * kvd, j), cos_t, sin_a, sin_b))):
            kw_ref[0, 2 * j + u] = slab.astype(BF16)
        for u, slab in enumerate(_pair_slabs(slab_of(3 * kvd, j))):
            vs_ref[0, 2 * j + u] = jnp.where(ones_lane, 1.0, slab).astype(BF16)
        for u, slab in enumerate(_pair_slabs(slab_of(5 * kvd, j))):
            vw_ref[0, 2 * j + u] = jnp.where(ones_lane, 1.0, slab).astype(BF16)


def _nsa_prep(proj, col_q, col_kv, n_q, kvd, ts=512):
    bsz, seq, _ = proj.shape
    hd = NSA_HEAD_DIM
    ts = min(ts, seq)
    n_heads, n_kv = n_q // hd, kvd // hd
    assert 2 * hd == LANES and seq // SEL_BLOCK <= LANES - hd
    cos_t, sin_a, sin_b = _rope_tables(np.arange(seq))
    tab = pl.BlockSpec((ts, LANES), lambda b, i: (i, 0))
    per_head = lambda n: pl.BlockSpec((1, n, ts, LANES), lambda b, i: (b, 0, i, 0))
    return pl.pallas_call(
        functools.partial(_nsa_prep_kernel, q_scale=hd ** -0.5, kvd=kvd),
        grid=(bsz, seq // ts),
        in_specs=[pl.BlockSpec((1, ts, n_q), lambda b, i: (b, i, col_q // n_q)),
                  pl.BlockSpec((1, ts, 6 * kvd), lambda b, i: (b, i, col_kv // (6 * kvd))),
                  tab, tab, tab],
        out_specs=[per_head(n_heads)] + [per_head(n_kv)] * 4,
        out_shape=[jax.ShapeDtypeStruct((bsz, n_heads, seq, LANES), BF16)]
        + [jax.ShapeDtypeStruct((bsz, n_kv, seq, LANES), BF16)] * 4,
        compiler_params=pltpu.CompilerParams(dimension_semantics=("parallel", "parallel")),
        name="nsa_prep",
    )(proj, proj, cos_t, sin_a, sin_b)


def _gelu_tanh(x):
    return 0.5 * x * (1.0 + jnp.tanh(math.sqrt(2.0 / math.pi) * (x + 0.044715 * (x * x * x))))


def _compress_kernel(x_ref, w1k_ref, w1v_ref, w2k_ref, w2v_ref, pek_ref, pev_ref,
                     cos_ref, sa_ref, sb_ref, ko_ref, vo_ref, x_s):
    seq = x_ref.shape[1]
    kvd = x_ref.shape[2] // 2
    n_rows = ko_ref.shape[2]
    n_pair = kvd // LANES
    for c in range(2 * n_pair):
        x_s[c, 0:seq, :] = x_ref[0, :, c * LANES:(c + 1) * LANES].astype(F32)
        x_s[c, seq:, :] = jnp.zeros((x_s.shape[1] - seq, LANES), F32)
    cos_t, sin_a, sin_b = cos_ref[...], sa_ref[...], sb_ref[...]

    def pair(c, w1_ref, w2_ref, pe_ref):
        taps = [x_s[c, pl.ds(p, n_rows, stride=CMP_STRIDE), :].astype(BF16) for p in range(CMP_LEN)]
        w1 = w1_ref[...]
        hid = _dot(jnp.concatenate(taps, axis=1), w1) + _dot(pe_ref[...], w1)[0:1]
        return _dot(_gelu_tanh(hid).astype(BF16), w2_ref[...])

    for j in range(n_pair):
        k = _rope(pair(j, w1k_ref, w2k_ref, pek_ref), cos_t, sin_a, sin_b)
        for u, slab in enumerate(_pair_slabs(k)):
            ko_ref[0, 2 * j + u] = slab.astype(BF16)
        for u, slab in enumerate(_pair_slabs(pair(n_pair + j, w1v_ref, w2v_ref, pev_ref))):
            vo_ref[0, 2 * j + u] = slab.astype(BF16)


def _compress(proj, col_kv, kvd, pe_k, pe_v, k_w1, k_w2, v_w1, v_w2):
    bsz, seq, _ = proj.shape
    hd = NSA_HEAD_DIM
    n_kv = kvd // hd
    assert seq % CMP_STRIDE == 0 and 2 * hd == LANES and col_kv % (2 * kvd) == 0
    n_rows = seq // CMP_STRIDE
    hidden = k_w1.shape[1]

    def pair_w1(w1):
        w = w1.reshape(CMP_LEN, 1, hd, 1, hidden) * jnp.eye(2, dtype=F32).reshape(1, 2, 1, 2, 1)
        return w.reshape(CMP_LEN * LANES, 2 * hidden).astype(BF16)

    pair_w2 = lambda w2: jnp.kron(jnp.eye(2, dtype=F32), w2).astype(BF16)
    pe_rows = lambda pe: jnp.broadcast_to(pe.reshape(1, CMP_LEN, 1, hd), (8, CMP_LEN, 2, hd)).reshape(
        8, CMP_LEN * LANES).astype(BF16)
    cos_t, sin_a, sin_b = _rope_tables(np.arange(n_rows) * CMP_STRIDE + CMP_LEN - 1)
    const = lambda shape: pl.BlockSpec(shape, lambda b: (0,) * len(shape))
    ospec = pl.BlockSpec((1, n_kv, n_rows, LANES), lambda b: (b, 0, 0, 0))
    return pl.pallas_call(
        _compress_kernel,
        grid=(bsz,),
        in_specs=[pl.BlockSpec((1, seq, 2 * kvd), lambda b: (b, 0, col_kv // (2 * kvd))),
                  const((CMP_LEN * LANES, 2 * hidden)), const((CMP_LEN * LANES, 2 * hidden)),
                  const((2 * hidden, LANES)), const((2 * hidden, LANES)),
                  const((8, CMP_LEN * LANES)), const((8, CMP_LEN * LANES)),
                  const((n_rows, LANES)), const((n_rows, LANES)), const((n_rows, LANES))],
        out_specs=[ospec, ospec],
        out_shape=[jax.ShapeDtypeStruct((bsz, n_kv, n_rows, LANES), BF16)] * 2,
        scratch_shapes=[pltpu.VMEM((2 * kvd // LANES, seq + CMP_LEN, LANES), F32)],
        compiler_params=pltpu.CompilerParams(dimension_semantics=("parallel",)),
        name="compress",
    )(proj, pair_w1(k_w1), pair_w1(v_w1), pair_w2(k_w2), pair_w2(v_w2), pe_rows(pe_k), pe_rows(pe_v),
      cos_t, sin_a, sin_b)


def _dot_tn(a, b):
    return lax.dot_general(a, b, (((0,), (0,)), ((), ())), preferred_element_type=F32)


def _nsa_kernel(q_ref, kc_ref, vc_ref, ka_ref, vs_ref, kw_ref, vw_ref, gl_ref, crel_ref, wrel_ref, o_ref,
                psum_s, imp_s):
    i = pl.program_id(2)
    gqa, qt = q_ref.shape[1:3]
    hd = NSA_HEAD_DIM
    cols = gqa * qt
    n_blk = imp_s.shape[0]
    n_rows = kc_ref.shape[2]
    band = WINDOW + qt
    qp = q_ref[0].reshape(cols, LANES)
    lane_q = lax.broadcasted_iota(jnp.int32, (1, cols), 1) % qt
    t0 = i * qt
    q0 = pl.multiple_of(t0, qt)
    w0 = pl.multiple_of(jnp.maximum(t0 - WINDOW, 0), qt)

    k_all = jnp.concatenate([kc_ref[0, 0], kw_ref[0, 0, pl.ds(w0, band), :], ka_ref[0, 0, pl.ds(q0, qt), :]], axis=0)
    s_all = _dot_nt(k_all, qp)
    s_c, s_w, s_d = s_all[:n_rows], s_all[n_rows:n_rows + band], s_all[n_rows + band:]

    cmask = crel_ref[...] <= t0
    s_c = jnp.where(cmask, s_c, NEG_INF)
    e_c = jnp.exp(s_c - jnp.max(s_c, axis=0, keepdims=True))
    has_cmp = t0 + lane_q >= CMP_LEN - 1
    p_cmp = e_c * jnp.where(has_cmp, 1.0 / jnp.sum(e_c, axis=0, keepdims=True), 0.0)
    o_cmp = _dot_tn(vc_ref[0, 0], p_cmp.astype(BF16))[:hd]

    w_rel = wrel_ref[...]
    wmask = (w_rel <= t0 - w0) & (w_rel > t0 - w0 - WINDOW)
    s_w = jnp.where(wmask, s_w, NEG_INF)
    e_w = jnp.exp((s_w - jnp.max(s_w, axis=0, keepdims=True)).astype(BF16))
    pv_w = _dot_tn(vw_ref[0, 0, pl.ds(w0, band), :], e_w)
    o_win = pv_w[:hd] * (1.0 / pv_w[hd:hd + 1])

    s_d = jnp.where(lax.broadcasted_iota(jnp.int32, (qt, cols), 0) <= lane_q, s_d, NEG_INF)
    m0 = jnp.max(s_d, axis=0, keepdims=True)
    pv_d = _dot_tn(vs_ref[0, 0, pl.ds(q0, qt), :], jnp.exp((s_d - m0).astype(BF16)))
    l0, acc0 = pv_d[hd:hd + 1], pv_d[:hd]

    p_sum = p_cmp[:, 0:qt]
    for r in range(1, gqa):
        p_sum = p_sum + p_cmp[:, r * qt:(r + 1) * qt]
    psum_s[0:IMP_PAD, :] = jnp.zeros((IMP_PAD, qt), F32)
    psum_s[IMP_PAD:, :] = p_sum
    ratio = SEL_BLOCK // CMP_STRIDE
    imp = psum_s[pl.ds(IMP_PAD - 1, n_blk, stride=ratio), :]
    for k in range(ratio):
        imp = imp + psum_s[pl.ds(IMP_PAD + k, n_blk, stride=ratio), :]
    blk = lax.broadcasted_iota(jnp.int32, (n_blk, qt), 0)
    tq = t0 + lax.broadcasted_iota(jnp.int32, (n_blk, qt), 1)
    cur = tq // SEL_BLOCK
    forced = (blk == 0) | (blk == cur) | (blk == cur - 1)
    imp = jnp.where(forced, FORCED_SCORE, jnp.where(blk * SEL_BLOCK > tq, -1.0, imp))
    imp_s[...] = imp

    n_live = (t0 + qt - 1) // SEL_BLOCK + 1

    def rank_body(jj, cnt):
        for u in range(RANK_UNROLL):
            j = jj * RANK_UNROLL + u
            other = imp_s[pl.ds(j, 1), :]
            ahead = (other > imp) | ((other == imp) & (j < blk))
            cnt = cnt + ahead.astype(F32)
        return cnt

    trips = jnp.where(n_live > N_SELECT, n_live // RANK_UNROLL, 0)
    rank = lax.fori_loop(0, trips, rank_body, jnp.zeros((n_blk, qt), F32))
    sel_bias = jnp.where((rank < N_SELECT) & (blk * SEL_BLOCK < t0), 0.0, NEG_INF)
    pad_rows = [jnp.zeros((LANES - hd - n_blk, qt), F32)] if LANES - hd > n_blk else []
    bias_t = jnp.concatenate([jnp.zeros((hd, qt), F32), sel_bias] + pad_rows, axis=0).T
    qa = (qp.astype(F32) + jnp.concatenate([bias_t] * gqa, axis=0)).astype(BF16)

    def sel_body(c, carry):
        m, l, acc = carry
        k0 = pl.multiple_of(c * SEL_CHUNK, SEL_CHUNK)
        kk = ka_ref[0, 0, pl.ds(k0, SEL_CHUNK), :]
        vv = vs_ref[0, 0, pl.ds(k0, SEL_CHUNK), :]
        s = _dot_nt(kk, qa)
        pv = _dot_tn(vv, jnp.exp((s - m).astype(BF16)))
        m_new = jnp.maximum(m, jnp.max(s, axis=0, keepdims=True))
        alpha = jnp.exp(m - m_new)
        l_lazy = alpha * (l + pv[hd:hd + 1])
        acc_lazy = alpha * (acc + pv[:hd])
        safe = jnp.max(m_new - m) <= LAZY_MAX_SLACK

        def exact():
            pv2 = _dot_tn(vv, jnp.exp((_dot_nt(kk, qa) - m_new).astype(BF16)))
            return alpha * l + pv2[hd:hd + 1], alpha * acc + pv2[:hd]

        l, acc = lax.cond(safe, lambda: (l_lazy, acc_lazy), exact)
        return m_new, l, acc

    n_chunks = (t0 + SEL_CHUNK - 1) // SEL_CHUNK
    _, l_sel, acc_sel = lax.fori_loop(0, n_chunks, sel_body, (m0, l0, acc0))
    o_sel = acc_sel / l_sel

    gate = jax.nn.sigmoid(gl_ref[0, 0, 0].astype(F32))
    out = gate[0:1] * o_cmp + gate[1:2] * o_sel + gate[2:3] * o_win
    pairs = [jnp.concatenate([out[:, r * qt:(r + 1) * qt], out[:, (r + 1) * qt:(r + 2) * qt]], axis=0).T
             for r in range(0, gqa, 2)]
    o_ref[0] = jnp.concatenate(pairs, axis=1).astype(BF16)


def _nsa_attention(q_hm, kc, vc, ka, vs, kw, vw, gl):
    bsz, n_heads, seq, _ = q_hm.shape
    hd = NSA_HEAD_DIM
    n_kv = ka.shape[1]
    gqa = n_heads // n_kv
    n_rows = kc.shape[2]
    n_cmp = (seq - CMP_LEN) // CMP_STRIDE + 1
    nq = seq // Q_TILE
    n_blk = seq // SEL_BLOCK
    assert N_SELECT <= n_blk <= LANES - hd and Q_TILE == 2 * SEL_BLOCK == 2 * hd and gqa % 2 == 0
    assert seq % SEL_CHUNK == 0 and seq >= WINDOW + Q_TILE
    assert n_rows * CMP_STRIDE == seq and n_blk * (SEL_BLOCK // CMP_STRIDE) <= n_rows and CMP_LEN == 2 * CMP_STRIDE
    cols = gqa * Q_TILE
    gl2 = gl.reshape(bsz, nq, Q_TILE, n_kv, gqa, 3).transpose(0, 3, 1, 5, 4, 2).reshape(bsz, n_kv, nq, 3, cols)
    lane_q = np.arange(cols)[None, :] % Q_TILE
    n_idx = np.arange(n_rows)[:, None]
    crel = np.where(n_idx < n_cmp, n_idx * CMP_STRIDE + (CMP_LEN - 1) - lane_q, np.iinfo(np.int32).max).astype(np.int32)
    wrel = (np.arange(WINDOW + Q_TILE)[:, None] - lane_q).astype(np.int32)
    const = lambda a: pl.BlockSpec(a.shape, lambda b, g, i: (0, 0))
    kv_spec = lambda n: pl.BlockSpec((1, 1, n, LANES), lambda b, g, i: (b, g, 0, 0))
    return pl.pallas_call(
        _nsa_kernel,
        grid=(bsz, n_kv, nq),
        in_specs=[pl.BlockSpec((1, gqa, Q_TILE, LANES), lambda b, g, i: (b, g, i, 0)),
                  kv_spec(n_rows), kv_spec(n_rows), kv_spec(seq), kv_spec(seq), kv_spec(seq), kv_spec(seq),
                  pl.BlockSpec((1, 1, 1, 3, cols), lambda b, g, i: (b, g, i, 0, 0)),
                  const(crel), const(wrel)],
        out_specs=pl.BlockSpec((1, Q_TILE, gqa * hd), lambda b, g, i: (b, i, g)),
        out_shape=jax.ShapeDtypeStruct((bsz, seq, n_heads * hd), BF16),
        scratch_shapes=[pltpu.VMEM((IMP_PAD + n_rows, Q_TILE), F32), pltpu.VMEM((n_blk, Q_TILE), F32)],
        compiler_params=pltpu.CompilerParams(dimension_semantics=("parallel", "parallel", "arbitrary")),
        name="nsa_attention",
    )(q_hm, kc, vc, ka, vs, kw, vw, gl2, jnp.asarray(crel), jnp.asarray(wrel))


def _merge_kernel(h_ref, mod_ref, ya_ref, yb_ref, yc_ref, gl_ref, wa_ref, wb_ref, wc_ref, wo_ref,
                  lng_ref, lnb_ref, o_ref, *, alpha):
    d = h_ref.shape[2]
    gate = lambda k: jax.nn.sigmoid(gl_ref[0, :, k * d:(k + 1) * d].astype(F32))
    merged = gate(0) * _dot(ya_ref[0], wa_ref[...])
    merged = merged + gate(1) * _dot(yb_ref[0], wb_ref[...])
    merged = merged + gate(2) * _dot(yc_ref[0], wc_ref[...])
    y = _dot(merged.astype(BF16), wo_ref[...])
    z = alpha * h_ref[0] + mod_ref[0, 2:3, :] * y
    o_ref[0] = _layer_norm(z, lng_ref[...], lnb_ref[...])


def _merge(h, mod3, ya, yb, yc, proj, col_gate, wa, wb, wc, wo, ln_g, ln_b, alpha, tm=512):
    bsz, seq, d = h.shape
    tm = min(tm, seq)
    act = pl.BlockSpec((1, tm, d), lambda b, i: (b, i, 0))
    wspec = pl.BlockSpec((d, d), lambda b, i: (0, 0))
    vec = pl.BlockSpec((1, d), lambda b, i: (0, 0))
    return pl.pallas_call(
        functools.partial(_merge_kernel, alpha=alpha),
        grid=(bsz, seq // tm),
        in_specs=[act, pl.BlockSpec((1, 3, d), lambda b, i: (b, 0, 0)), act, act, act,
                  pl.BlockSpec((1, tm, 3 * d), lambda b, i: (b, i, col_gate // (3 * d))),
                  wspec, wspec, wspec, wspec, vec, vec],
        out_specs=act,
        out_shape=jax.ShapeDtypeStruct((bsz, seq, d), F32),
        compiler_params=pltpu.CompilerParams(dimension_semantics=("parallel", "parallel")),
        name="merge",
    )(h, mod3, ya, yb, yc, proj, wa, wb, wc, wo, ln_g.reshape(1, d), ln_b.reshape(1, d))


def _token_mixer(h, mod3, w_in, conv_a_w, conv_a_b, norm_a_g, norm_a_b, w_a_out,
                 ssm_conv_w, ssm_conv_b, ssm_dt_bias, ssm_a_log, ssm_d, ssm_norm_w, w_b_out,
                 cmp_pe_k, cmp_pe_v, cmp_k_w1, cmp_k_w2, cmp_v_w1, cmp_v_w2, w_c_out, w_o, ln_g, ln_b, alpha):
    bsz, seq, d = h.shape
    ch = conv_a_w.shape[1]
    n_ssm_heads = ssm_dt_bias.shape[0]
    di = n_ssm_heads * SSM_HEAD_DIM
    gn = SSM_GROUPS * SSM_STATE
    n_q = w_c_out.shape[0]
    n_heads = n_q // NSA_HEAD_DIM
    kvd = NSA_KV_HEADS * NSA_HEAD_DIM
    gqa = n_heads // NSA_KV_HEADS
    n_gate = 3 * n_heads
    o_dt = 2 * ch + 2 * di + 2 * gn
    o_q = o_dt + n_ssm_heads
    o_gl = o_q + n_q + 6 * kvd
    o_g = o_gl + n_gate
    assert w_in.shape[1] == o_g + 3 * d
    col_z, col_xs, col_bc = 2 * ch, 2 * ch + di, 2 * ch + 2 * di
    col_q = o_dt
    col_gate = col_q + n_q
    col_kv = col_gate + 3 * d
    col_small = col_kv + 6 * kvd
    assert n_ssm_heads + n_gate <= LANES
    w_perm = jnp.concatenate(
        [w_in[:, :o_dt], w_in[:, o_q:o_q + n_q], w_in[:, o_g:], w_in[:, o_q + n_q:o_gl], w_in[:, o_dt:o_q],
         w_in[:, o_gl:o_g], jnp.zeros((d, LANES - n_ssm_heads - n_gate), F32)], axis=1).astype(BF16)

    proj = _inproj(h, mod3, w_perm)

    y_a = _conformer(proj, conv_a_w, conv_a_b, norm_a_g, norm_a_b)
    y_b = _ssd(proj, col_z, col_xs, col_bc, col_small, ssm_conv_w, ssm_conv_b, ssm_dt_bias, ssm_a_log,
               ssm_d, ssm_norm_w)

    q_hm, ka, kw, vs, vw = _nsa_prep(proj, col_q, col_kv, n_q, kvd)
    k_cmp, v_cmp = _compress(proj, col_kv, kvd, cmp_pe_k, cmp_pe_v, cmp_k_w1, cmp_k_w2, cmp_v_w1, cmp_v_w2)
    gl = proj[:, :, col_small + n_ssm_heads:col_small + n_ssm_heads + n_gate]
    y_c = _nsa_attention(q_hm, k_cmp, v_cmp, ka, vs, kw, vw, gl)

    return _merge(h, mod3, y_a, y_b, y_c, proj, col_gate, w_a_out.astype(BF16), w_b_out.astype(BF16),
                  w_c_out.astype(BF16), w_o.astype(BF16), ln_g, ln_b, alpha)


def kernel(x, c, ada_w, ada_b, ln_g, ln_b, ffn_w_gate, ffn_w_up, ffn_w_down, w_in, conv_a_w, conv_a_b, norm_a_g, norm_a_b, w_a_out, ssm_conv_w, ssm_conv_b, ssm_dt_bias, ssm_a_log, ssm_d, ssm_norm_w, w_b_out, cmp_pe_k, cmp_pe_v, cmp_k_w1, cmp_k_w2, cmp_v_w1, cmp_v_w2, w_c_out, w_o):
    depth = ada_w.shape[0]
    alpha = (2.0 * depth) ** 0.25
    mod = _ada_mod(c, ada_w, ada_b)
    wg, wu, wd = ffn_w_gate.astype(BF16), ffn_w_up.astype(BF16), ffn_w_down.astype(BF16)
    h = x
    for l in range(depth):
        h = _ffn_block(h, mod[l, :, 0:3], wg[l, 0], wu[l, 0], wd[l, 0], ln_g[l, 0], ln_b[l, 0], alpha)
        h = _token_mixer(h, mod[l, :, 3:6], w_in[l], conv_a_w[l], conv_a_b[l], norm_a_g[l], norm_a_b[l], w_a_out[l],
                         ssm_conv_w[l], ssm_conv_b[l], ssm_dt_bias[l], ssm_a_log[l], ssm_d[l], ssm_norm_w[l],
                         w_b_out[l], cmp_pe_k[l], cmp_pe_v[l], cmp_k_w1[l], cmp_k_w2[l], cmp_v_w1[l], cmp_v_w2[l],
                         w_c_out[l], w_o[l], ln_g[l, 1], ln_b[l, 1], alpha)
        h = _ffn_block(h, mod[l, :, 6:9], wg[l, 1], wu[l, 1], wd[l, 1], ln_g[l, 2], ln_b[l, 2], alpha)
    return h
```

```python
import functools
import math

import numpy as np
import jax
import jax.numpy as jnp
from jax import lax
from jax.experimental import pallas as pl
from jax.experimental.pallas import tpu as pltpu

F32 = jnp.float32
BF16 = jnp.bfloat16

LN_EPS = 1e-5
NEG_INF = -1e30
MACARON_WEIGHT = 0.5

SSM_HEAD_DIM = 64
SSM_GROUPS = 4
SSM_STATE = 128
SSM_CHUNK = 256

NSA_HEAD_DIM = 64
NSA_KV_HEADS = 4
CMP_LEN = 32
CMP_STRIDE = 16
SEL_BLOCK = 64
N_SELECT = 16
WINDOW = 512
FORCED_SCORE = 1e4
ROPE_THETA = 10000.0

LANES = 128
SUBLANES = 8
SEL_CHUNK = 1024
LAZY_MAX_SLACK = 60.0
RANK_UNROLL = 4
Q_TILE = 256
IMP_PAD = 8
CONV_HALO = 32


def _dot(a, b, precision=None):
    return jnp.dot(a, b, preferred_element_type=F32, precision=precision)


def _dot_nt(a, b, precision=None):
    return lax.dot_general(a, b, (((1,), (1,)), ((), ())), preferred_element_type=F32, precision=precision)


def _bf16_terms(x):
    hi = x.astype(BF16)
    r1 = x - hi.astype(F32)
    mid = r1.astype(BF16)
    lo = (r1 - mid.astype(F32)).astype(BF16)
    return hi, mid, lo


def _dot_exact_lhs(x, sel):
    hi, mid, lo = _bf16_terms(x)
    return _dot(hi, sel) + _dot(mid, sel) + _dot(lo, sel)


def _dot_exact_rhs(sel, x):
    hi, mid, lo = _bf16_terms(x)
    return _dot(sel, hi) + _dot(sel, mid) + _dot(sel, lo)


def _layer_norm(z, g, b):
    mu = jnp.mean(z, axis=-1, keepdims=True)
    zc = z - mu
    var = jnp.mean(zc * zc, axis=-1, keepdims=True)
    return zc * lax.rsqrt(var + LN_EPS) * g + b


def _silu(x):
    return x * jax.nn.sigmoid(x)


def _ada_kernel(c_ref, w_ref, b_ref, o_ref):
    a = _silu(c_ref[...]).astype(BF16)
    o_ref[0] = _dot(a, w_ref[0].astype(BF16)) + b_ref[0]


def _ada_mod(c, ada_w, ada_b):
    n_layer, d, n = ada_w.shape
    bsz = c.shape[0]
    rows = 8
    cp = jnp.zeros((rows, d), F32).at[:bsz].set(c)
    tn = 1024
    out = pl.pallas_call(
        _ada_kernel,
        grid=(n_layer, n // tn),
        in_specs=[pl.BlockSpec((rows, d), lambda l, j: (0, 0)),
                  pl.BlockSpec((1, d, tn), lambda l, j: (l, 0, j)),
                  pl.BlockSpec((1, 1, tn), lambda l, j: (l, 0, j))],
        out_specs=pl.BlockSpec((1, rows, tn), lambda l, j: (l, 0, j)),
        out_shape=jax.ShapeDtypeStruct((n_layer, rows, n), F32),
        name="ada_mod",
    )(cp, ada_w, ada_b.reshape(n_layer, 1, n))
    return out[:, :bsz].reshape(n_layer, bsz, 9, d)


def _ffn_kernel(h_ref, mod_ref, wg_ref, wu_ref, wd_ref, lng_ref, lnb_ref, o_ref, u_s, acc_s, *, alpha):
    f = pl.program_id(2)

    @pl.when(f == 0)
    def _():
        u_s[...] = (h_ref[0] * (1.0 + mod_ref[0, 1:2, :]) + mod_ref[0, 0:1, :]).astype(BF16)
        acc_s[...] = jnp.zeros_like(acc_s)

    u = u_s[...]
    g = _dot(u, wg_ref[...])
    up = _dot(u, wu_ref[...])
    a = (_silu(g) * up).astype(BF16)
    acc_s[...] += _dot(a, wd_ref[...])

    @pl.when(f == pl.num_programs(2) - 1)
    def _():
        z = alpha * h_ref[0] + (MACARON_WEIGHT * mod_ref[0, 2:3, :]) * acc_s[...]
        o_ref[0] = _layer_norm(z, lng_ref[...], lnb_ref[...])


def _ffn_block(h, mod3, wg, wu, wd, layer, half, ln_g, ln_b, alpha, tm=1024):
    bsz, seq, d = h.shape
    d_ff = wg.shape[3]
    tf = d_ff // 2 if (d_ff // 2) % LANES == 0 else d_ff
    tm = min(tm, seq)
    return pl.pallas_call(
        functools.partial(_ffn_kernel, alpha=alpha),
        grid=(bsz, seq // tm, d_ff // tf),
        in_specs=[pl.BlockSpec((1, tm, d), lambda b, i, f: (b, i, 0)),
                  pl.BlockSpec((1, 3, d), lambda b, i, f: (b, 0, 0)),
                  pl.BlockSpec((None, None, d, tf), lambda b, i, f: (layer, half, 0, f)),
                  pl.BlockSpec((None, None, d, tf), lambda b, i, f: (layer, half, 0, f)),
                  pl.BlockSpec((None, None, tf, d), lambda b, i, f: (layer, half, f, 0)),
                  pl.BlockSpec((1, d), lambda b, i, f: (0, 0)),
                  pl.BlockSpec((1, d), lambda b, i, f: (0, 0))],
        out_specs=pl.BlockSpec((1, tm, d), lambda b, i, f: (b, i, 0)),
        out_shape=jax.ShapeDtypeStruct((bsz, seq, d), F32),
        scratch_shapes=[pltpu.VMEM((tm, d), BF16), pltpu.VMEM((tm, d), F32)],
        compiler_params=pltpu.CompilerParams(dimension_semantics=("parallel", "parallel", "arbitrary")),
        name="ffn",
    )(h, mod3, wg, wu, wd, ln_g.reshape(1, d), ln_b.reshape(1, d))


def _inproj_kernel(h_ref, mod_ref, w_ref, o_ref, u_s):
    @pl.when(pl.program_id(2) == 0)
    def _():
        u_s[...] = (h_ref[0] * (1.0 + mod_ref[0, 1:2, :]) + mod_ref[0, 0:1, :]).astype(BF16)

    o_ref[0] = _dot(u_s[...], w_ref[...]).astype(BF16)


def _inproj(h, mod3, w, tm=1024):
    bsz, seq, d = h.shape
    n = w.shape[1]
    tn = n // 5
    tm = min(tm, seq)
    return pl.pallas_call(
        _inproj_kernel,
        grid=(bsz, seq // tm, n // tn),
        in_specs=[pl.BlockSpec((1, tm, d), lambda b, i, j: (b, i, 0)),
                  pl.BlockSpec((1, 3, d), lambda b, i, j: (b, 0, 0)),
                  pl.BlockSpec((d, tn), lambda b, i, j: (0, j))],
        out_specs=pl.BlockSpec((1, tm, tn), lambda b, i, j: (b, i, j)),
        out_shape=jax.ShapeDtypeStruct((bsz, seq, n), BF16),
        scratch_shapes=[pltpu.VMEM((tm, d), BF16)],
        compiler_params=pltpu.CompilerParams(dimension_semantics=("parallel", "parallel", "arbitrary")),
        name="inproj",
    )(h, mod3, w)


def _conf_kernel(cur_ref, halo_ref, w_ref, cb_ref, g_ref, b_ref, o_ref, buf_s, sh_s, y_s, *, width):
    i = pl.program_id(1)
    ts, ch = y_s.shape
    x = cur_ref[0].astype(F32)
    buf_s[CONV_HALO:, :] = x[:, :ch] * jax.nn.sigmoid(x[:, ch:])
    xh = halo_ref[0].astype(F32)
    ah = xh[:, :ch] * jax.nn.sigmoid(xh[:, ch:])
    buf_s[0:CONV_HALO, :] = jnp.where(i > 0, ah, 0.0)
    off = CONV_HALO - (width - 1)
    n_sh = sh_s.shape[1]
    for r in range(1, SUBLANES):
        sh_s[r - 1] = buf_s[r:r + n_sh, :]
    rc, lc = 64, 256
    for r0 in range(0, ts, rc):
        for l0 in range(0, ch, lc):
            acc = jnp.zeros((rc // SUBLANES, SUBLANES, lc), F32) + cb_ref[:, l0:l0 + lc]
            for k in range(width):
                a, r = divmod(k + off, SUBLANES)
                rows = slice(r0 + SUBLANES * a, r0 + SUBLANES * a + rc)
                tap = buf_s[rows, l0:l0 + lc] if r == 0 else sh_s[r - 1, rows, l0:l0 + lc]
                acc = acc + tap.reshape(rc // SUBLANES, SUBLANES, lc) * w_ref[k, :, l0:l0 + lc]
            y_s[r0:r0 + rc, l0:l0 + lc] = acc.reshape(rc, lc)
    yn = _layer_norm(y_s[...], g_ref[...], b_ref[...])
    o_ref[0] = _silu(yn).astype(BF16)


def _conformer(proj, conv_w, conv_b, norm_g, norm_b, ts=256):
    bsz, seq, _ = proj.shape
    width, ch = conv_w.shape
    assert width - 1 <= CONV_HALO
    ts = min(ts, seq)
    wpad = jnp.zeros((CONV_HALO, ch), F32).at[:width].set(conv_w)
    wpad = jnp.broadcast_to(wpad[:, None, :], (CONV_HALO, SUBLANES, ch))
    hb = ts // CONV_HALO
    return pl.pallas_call(
        functools.partial(_conf_kernel, width=width),
        grid=(bsz, seq // ts),
        in_specs=[pl.BlockSpec((1, ts, 2 * ch), lambda b, i: (b, i, 0)),
                  pl.BlockSpec((1, CONV_HALO, 2 * ch), lambda b, i: (b, jnp.maximum(i * hb - 1, 0), 0)),
                  pl.BlockSpec((CONV_HALO, SUBLANES, ch), lambda b, i: (0, 0, 0)),
                  pl.BlockSpec((1, ch), lambda b, i: (0, 0)),
                  pl.BlockSpec((1, ch), lambda b, i: (0, 0)),
                  pl.BlockSpec((1, ch), lambda b, i: (0, 0))],
        out_specs=pl.BlockSpec((1, ts, ch), lambda b, i: (b, i, 0)),
        out_shape=jax.ShapeDtypeStruct((bsz, seq, ch), BF16),
        scratch_shapes=[pltpu.VMEM((CONV_HALO + ts, ch), F32),
                        pltpu.VMEM((SUBLANES - 1, CONV_HALO + ts - SUBLANES, ch), F32),
                        pltpu.VMEM((ts, ch), F32)],
        compiler_params=pltpu.CompilerParams(dimension_semantics=("parallel", "parallel")),
        name="conformer",
    )(proj, proj, wpad, conv_b.reshape(1, ch), norm_g.reshape(1, ch), norm_b.reshape(1, ch))


def _ssd_kernel(z_ref, xs_ref, bc_ref, sm_ref, cw_ref, cbias_ref, dtb_ref, alog_ref, dfull_ref, nw_ref, ex_ref,
                o_ref, buf_s, st_s, y_s, *, conv_width):
    c = pl.program_id(1)
    lc, di = y_s.shape
    n_grp, n_state, gw = st_s.shape
    hpg = gw // SSM_HEAD_DIM
    carry = 8

    @pl.when(c == 0)
    def _():
        buf_s[0:carry, :] = jnp.zeros((carry, buf_s.shape[1]), F32)
        st_s[...] = jnp.zeros_like(st_s)

    @pl.when(c > 0)
    def _():
        buf_s[0:carry, :] = buf_s[lc:lc + carry, :]

    buf_s[carry:, 0:di] = xs_ref[0].astype(F32)
    buf_s[carry:, di:] = bc_ref[0].astype(F32)
    acc = jnp.zeros((lc, buf_s.shape[1]), F32) + cbias_ref[...]
    xb = buf_s[...]
    for k in range(conv_width):
        delay = conv_width - 1 - k
        tap = xb if delay == 0 else pltpu.roll(xb, delay, 0)
        acc = acc + tap[carry:, :] * cw_ref[k:k + 1, :]
    xbc = _silu(acc)
    xs = xbc[:, 0:di]
    gn = n_grp * n_state
    bm = xbc[:, di:di + gn]
    cm = xbc[:, di + gn:di + 2 * gn]

    x_dt = sm_ref[0].astype(F32) + dtb_ref[...]
    dt = jnp.maximum(x_dt, 0.0) + jnp.log(1.0 + jnp.exp(-jnp.abs(x_dt)))
    adt = dt * (-jnp.exp(alog_ref[...]))
    row = lax.broadcasted_iota(jnp.int32, (lc, lc), 0)
    col = lax.broadcasted_iota(jnp.int32, (lc, lc), 1)
    causal = row >= col
    acs = _dot_exact_rhs(causal.astype(BF16), adt)
    acs_t = acs.T
    ex = ex_ref[...]
    acs_full = _dot_exact_lhs(acs, ex)
    dt_full = _dot_exact_lhs(dt, ex)
    last_full = acs_full[lc - 1:lc, :]
    xdt = xs * dt_full
    e_acs = jnp.exp(acs_full)
    x_in = (xdt * jnp.exp(last_full - acs_full)).astype(BF16)
    chunk_decay = jnp.exp(last_full)
    lane_head = lax.broadcasted_iota(jnp.int32, (lc, gw), 1) // SSM_HEAD_DIM

    for g in range(n_grp):
        bg = bm[:, g * n_state:(g + 1) * n_state]
        cg = cm[:, g * n_state:(g + 1) * n_state].astype(BF16)
        cb = _dot_nt(cg, bg.astype(BF16))
        hg = st_s[g]
        y_g = _dot(cg, hg.astype(BF16)) * e_acs[:, g * gw:(g + 1) * gw]
        xg = xdt[:, g * gw:(g + 1) * gw]
        for r in range(hpg):
            hd = g * hpg + r
            seg = jnp.exp(jnp.where(causal, acs[:, hd:hd + 1] - acs_t[hd:hd + 1, :], NEG_INF))
            xr = jnp.where(lane_head == r, xg, 0.0).astype(BF16)
            y_g = y_g + _dot((cb * seg).astype(BF16), xr)
        st_s[g] = hg * chunk_decay[:, g * gw:(g + 1) * gw] + _dot(bg.T.astype(BF16), x_in[:, g * gw:(g + 1) * gw])
        y_s[:, g * gw:(g + 1) * gw] = y_g

    y = y_s[...] + dfull_ref[...] * xs
    y = y * _silu(z_ref[0].astype(F32))
    y = y * lax.rsqrt(jnp.mean(y * y, axis=-1, keepdims=True) + LN_EPS) * nw_ref[...]
    o_ref[0] = y.astype(BF16)


def _ssd(proj, col_z, col_xs, col_bc, col_small, conv_w, conv_b, dt_bias, a_log, d_skip, norm_w):
    bsz, seq, _ = proj.shape
    n_heads = dt_bias.shape[0]
    di = n_heads * SSM_HEAD_DIM
    conv_width, conv_dim = conv_w.shape
    gn = SSM_GROUPS * SSM_STATE
    assert conv_dim == di + 2 * gn and 2 * gn == di
    lc = math.gcd(SSM_CHUNK, seq)
    gw = di // SSM_GROUPS
    cw = jnp.zeros((8, conv_dim), F32).at[:conv_width].set(conv_w)
    pad = lambda v: jnp.zeros((1, LANES), F32).at[0, :n_heads].set(v)
    expand = np.zeros((LANES, di), np.float32)
    expand[np.arange(di) // SSM_HEAD_DIM, np.arange(di)] = 1.0
    d_full = jnp.repeat(d_skip, SSM_HEAD_DIM).reshape(1, di)
    cblk = lambda col, w: pl.BlockSpec((1, lc, w), lambda b, c: (b, c, col // w))
    const = lambda shape: pl.BlockSpec(shape, lambda b, c: (0,) * len(shape))
    return pl.pallas_call(
        functools.partial(_ssd_kernel, conv_width=conv_width),
        grid=(bsz, seq // lc),
        in_specs=[cblk(col_z, di), cblk(col_xs, di), cblk(col_bc, di), cblk(col_small, LANES),
                  const((8, conv_dim)), const((1, conv_dim)), const((1, LANES)), const((1, LANES)),
                  const((1, di)), const((1, di)), const((LANES, di))],
        out_specs=pl.BlockSpec((1, lc, di), lambda b, c: (b, c, 0)),
        out_shape=jax.ShapeDtypeStruct((bsz, seq, di), BF16),
        scratch_shapes=[pltpu.VMEM((8 + lc, conv_dim), F32),
                        pltpu.VMEM((SSM_GROUPS, SSM_STATE, gw), F32),
                        pltpu.VMEM((lc, di), F32)],
        compiler_params=pltpu.CompilerParams(dimension_semantics=("parallel", "arbitrary")),
        name="ssd",
    )(proj, proj, proj, proj, cw, conv_b.reshape(1, conv_dim), pad(dt_bias), pad(a_log), d_full,
      norm_w.reshape(1, di), jnp.asarray(expand, BF16))


def _rope_tables(pos):
    half = NSA_HEAD_DIM // 2
    inv_freq = ROPE_THETA ** (-np.arange(half, dtype=np.float32) / half)
    ang = jnp.asarray(pos, F32)[:, None] * jnp.asarray(inv_freq)[None, :]
    cos, sin, zero = jnp.cos(ang), jnp.sin(ang), jnp.zeros_like(ang)
    cos_t = jnp.concatenate([cos, cos, cos, cos], -1)
    sin_a = jnp.concatenate([-sin, zero, -sin, zero], -1)
    sin_b = jnp.concatenate([zero, sin, zero, sin], -1)
    return cos_t, sin_a, sin_b


def _rope(x, cos_t, sin_a, sin_b):
    half = NSA_HEAD_DIM // 2
    return x * cos_t + pltpu.roll(x, LANES - half, 1) * sin_a + pltpu.roll(x, half, 1) * sin_b


def _pair_slabs(x):
    return x, pltpu.roll(x, NSA_HEAD_DIM, 1)


def _nsa_prep_kernel(q_ref, kv_ref, cos_ref, sa_ref, sb_ref, qo_ref, ka_ref, kw_ref, vs_ref, vw_ref,
                     *, q_scale, kvd):
    i = pl.program_id(1)
    ts = q_ref.shape[1]
    hd = NSA_HEAD_DIM
    cos_t, sin_a, sin_b = cos_ref[...], sa_ref[...], sb_ref[...]
    lane = lax.broadcasted_iota(jnp.int32, (ts, LANES), 1)
    low = lane < hd
    ones_lane = lane == hd
    for j in range(q_ref.shape[2] // LANES):
        xq = _rope(q_ref[0, :, j * LANES:(j + 1) * LANES].astype(F32), cos_t, sin_a, sin_b) * q_scale
        for u, slab in enumerate(_pair_slabs(xq)):
            qo_ref[0, 2 * j + u] = jnp.where(low, slab, 0.0).astype(BF16)
    blk = (i * ts + lax.broadcasted_iota(jnp.int32, (ts, LANES), 0)) // SEL_BLOCK
    onehot = (blk == lane - hd).astype(F32)
    slab_of = lambda c0, j: kv_ref[0, :, c0 + j * LANES:c0 + (j + 1) * LANES].astype(F32)
    for j in range(kvd // LANES):
        for u, slab in enumerate(_pair_slabs(_rope(slab_of(2 * kvd, j), cos_t, sin_a, sin_b))):
            ka_ref[0, 2 * j + u] = jnp.where(low, slab, onehot).astype(BF16)
        for u, slab in enumerate(_pair_slabs(_rope(slab_of(4 * kvd, j), cos_t, sin_a, sin_b))):
            kw_ref[0, 2 * j + u] = slab.astype(BF16)
        for u, slab in enumerate(_pair_slabs(slab_of(3 * kvd, j))):
            vs_ref[0, 2 * j + u] = jnp.where(ones_lane, 1.0, slab).astype(BF16)
        for u, slab in enumerate(_pair_slabs(slab_of(5 * kvd, j))):
            vw_ref[0, 2 * j + u] = jnp.where(ones_lane, 1.0, slab).astype(BF16)


def _nsa_prep(proj, col_q, col_kv, n_q, kvd, ts=512):
    bsz, seq, _ = proj.shape
    hd = NSA_HEAD_DIM
    ts = min(ts, seq)
    n_heads, n_kv = n_q // hd, kvd // hd
    assert 2 * hd == LANES and seq // SEL_BLOCK <= LANES - hd
    cos_t, sin_a, sin_b = _rope_tables(np.arange(seq))
    tab = pl.BlockSpec((ts, LANES), lambda b, i: (i, 0))
    per_head = lambda n: pl.BlockSpec((1, n, ts, LANES), lambda b, i: (b, 0, i, 0))
    return pl.pallas_call(
        functools.partial(_nsa_prep_kernel, q_scale=hd ** -0.5, kvd=kvd),
        grid=(bsz, seq // ts),
        in_specs=[pl.BlockSpec((1, ts, n_q), lambda b, i: (b, i, col_q // n_q)),
                  pl.BlockSpec((1, ts, 6 * kvd), lambda b, i: (b, i, col_kv // (6 * kvd))),
                  tab, tab, tab],
        out_specs=[per_head(n_heads)] + [per_head(n_kv)] * 4,
        out_shape=[jax.ShapeDtypeStruct((bsz, n_heads, seq, LANES), BF16)]
        + [jax.ShapeDtypeStruct((bsz, n_kv, seq, LANES), BF16)] * 4,
        compiler_params=pltpu.CompilerParams(dimension_semantics=("parallel", "parallel")),
        name="nsa_prep",
    )(proj, proj, cos_t, sin_a, sin_b)


def _gelu_tanh(x):
    return 0.5 * x * (1.0 + jnp.tanh(math.sqrt(2.0 / math.pi) * (x + 0.044715 * (x * x * x))))


def _compress_kernel(x_ref, w1k_ref, w1v_ref, w2k_ref, w2v_ref, pek_ref, pev_ref,
                     cos_ref, sa_ref, sb_ref, ko_ref, vo_ref, x_s):
    seq = x_ref.shape[1]
    kvd = x_ref.shape[2] // 2
    n_rows = ko_ref.shape[2]
    n_pair = kvd // LANES
    for c in range(2 * n_pair):
        x_s[c, 0:seq, :] = x_ref[0, :, c * LANES:(c + 1) * LANES].astype(F32)
        x_s[c, seq:, :] = jnp.zeros((x_s.shape[1] - seq, LANES), F32)
    cos_t, sin_a, sin_b = cos_ref[...], sa_ref[...], sb_ref[...]

    def pair(c, w1_ref, w2_ref, pe_ref):
        taps = [x_s[c, pl.ds(p, n_rows, stride=CMP_STRIDE), :].astype(BF16) for p in range(CMP_LEN)]
        w1 = w1_ref[...]
        hid = _dot(jnp.concatenate(taps, axis=1), w1) + _dot(pe_ref[...], w1)[0:1]
        return _dot(_gelu_tanh(hid).astype(BF16), w2_ref[...])

    for j in range(n_pair):
        k = _rope(pair(j, w1k_ref, w2k_ref, pek_ref), cos_t, sin_a, sin_b)
        for u, slab in enumerate(_pair_slabs(k)):
            ko_ref[0, 2 * j + u] = slab.astype(BF16)
        for u, slab in enumerate(_pair_slabs(pair(n_pair + j, w1v_ref, w2v_ref, pev_ref))):
            vo_ref[0, 2 * j + u] = slab.astype(BF16)


def _compress(proj, col_kv, kvd, pe_k, pe_v, k_w1, k_w2, v_w1, v_w2):
    bsz, seq, _ = proj.shape
    hd = NSA_HEAD_DIM
    n_kv = kvd // hd
    assert seq % CMP_STRIDE == 0 and 2 * hd == LANES and col_kv % (2 * kvd) == 0
    n_rows = seq // CMP_STRIDE
    hidden = k_w1.shape[1]

    def pair_w1(w1):
        w = w1.reshape(CMP_LEN, 1, hd, 1, hidden) * jnp.eye(2, dtype=F32).reshape(1, 2, 1, 2, 1)
        return w.reshape(CMP_LEN * LANES, 2 * hidden).astype(BF16)

    pair_w2 = lambda w2: jnp.kron(jnp.eye(2, dtype=F32), w2).astype(BF16)
    pe_rows = lambda pe: jnp.broadcast_to(pe.reshape(1, CMP_LEN, 1, hd), (8, CMP_LEN, 2, hd)).reshape(
        8, CMP_LEN * LANES).astype(BF16)
    cos_t, sin_a, sin_b = _rope_tables(np.arange(n_rows) * CMP_STRIDE + CMP_LEN - 1)
    const = lambda shape: pl.BlockSpec(shape, lambda b: (0,) * len(shape))
    ospec = pl.BlockSpec((1, n_kv, n_rows, LANES), lambda b: (b, 0, 0, 0))
    return pl.pallas_call(
        _compress_kernel,
        grid=(bsz,),
        in_specs=[pl.BlockSpec((1, seq, 2 * kvd), lambda b: (b, 0, col_kv // (2 * kvd))),
                  const((CMP_LEN * LANES, 2 * hidden)), const((CMP_LEN * LANES, 2 * hidden)),
                  const((2 * hidden, LANES)), const((2 * hidden, LANES)),
                  const((8, CMP_LEN * LANES)), const((8, CMP_LEN * LANES)),
                  const((n_rows, LANES)), const((n_rows, LANES)), const((n_rows, LANES))],
        out_specs=[ospec, ospec],
        out_shape=[jax.ShapeDtypeStruct((bsz, n_kv, n_rows, LANES), BF16)] * 2,
        scratch_shapes=[pltpu.VMEM((2 * kvd // LANES, seq + CMP_LEN, LANES), F32)],
        compiler_params=pltpu.CompilerParams(dimension_semantics=("parallel",)),
        name="compress",
    )(proj, pair_w1(k_w1), pair_w1(v_w1), pair_w2(k_w2), pair_w2(v_w2), pe_rows(pe_k), pe_rows(pe_v),
      cos_t, sin_a, sin_b)


def _dot_tn(a, b):
    return lax.dot_general(a, b, (((0,), (0,)), ((), ())), preferred_element_type=F32)


def _nsa_kernel(q_ref, kc_ref, vc_ref, ka_ref, vs_ref, kw_ref, vw_ref, gl_ref, crel_ref, wrel_ref, o_ref,
                psum_s, imp_s):
    i = pl.program_id(2)
    gqa, qt = q_ref.shape[1:3]
    hd = NSA_HEAD_DIM
    cols = gqa * qt
    n_blk = imp_s.shape[0]
    n_rows = kc_ref.shape[2]
    band = WINDOW + qt
    qp = q_ref[0].reshape(cols, LANES)
    lane_q = lax.broadcasted_iota(jnp.int32, (1, cols), 1) % qt
    t0 = i * qt
    q0 = pl.multiple_of(t0, qt)
    w0 = pl.multiple_of(jnp.maximum(t0 - WINDOW, 0), qt)

    k_all = jnp.concatenate([kc_ref[0, 0], kw_ref[0, 0, pl.ds(w0, band), :]], axis=0)
    s_all = _dot_nt(k_all, qp)
    s_c, s_w = s_all[:n_rows], s_all[n_rows:]

    cmask = crel_ref[...] <= t0
    s_c = jnp.where(cmask, s_c, NEG_INF)
    e_c = jnp.exp(s_c - jnp.max(s_c, axis=0, keepdims=True))
    has_cmp = t0 + lane_q >= CMP_LEN - 1
    p_cmp = e_c * jnp.where(has_cmp, 1.0 / jnp.sum(e_c, axis=0, keepdims=True), 0.0)
    o_cmp = _dot_tn(vc_ref[0, 0], p_cmp.astype(BF16))[:hd]

    w_rel = wrel_ref[...]
    wmask = (w_rel <= t0 - w0) & (w_rel > t0 - w0 - WINDOW)
    s_w = jnp.where(wmask, s_w, NEG_INF)
    e_w = jnp.exp((s_w - jnp.max(s_w, axis=0, keepdims=True)).astype(BF16))
    pv_w = _dot_tn(vw_ref[0, 0, pl.ds(w0, band), :], e_w)
    o_win = pv_w[:hd] * (1.0 / pv_w[hd:hd + 1])

    p_sum = p_cmp[:, 0:qt]
    for r in range(1, gqa):
        p_sum = p_sum + p_cmp[:, r * qt:(r + 1) * qt]
    ratio = SEL_BLOCK // CMP_STRIDE
    imp_parts = []
    for c in range(qt // LANES):
        psum_s[c, 0:IMP_PAD, :] = jnp.zeros((IMP_PAD, LANES), F32)
        psum_s[c, IMP_PAD:, :] = p_sum[:, c * LANES:(c + 1) * LANES]
        part = psum_s[c, pl.ds(IMP_PAD - 1, n_blk, stride=ratio), :]
        for k in range(ratio):
            part = part + psum_s[c, pl.ds(IMP_PAD + k, n_blk, stride=ratio), :]
        imp_parts.append(part)
    imp = jnp.concatenate(imp_parts, axis=1)
    blk = lax.broadcasted_iota(jnp.int32, (n_blk, qt), 0)
    tq = t0 + lax.broadcasted_iota(jnp.int32, (n_blk, qt), 1)
    cur = tq // SEL_BLOCK
    forced = (blk == 0) | (blk == cur) | (blk == cur - 1)
    imp = jnp.where(forced, FORCED_SCORE, jnp.where(blk * SEL_BLOCK > tq, -1.0, imp))
    imp_s[...] = imp

    n_live = (t0 + qt - 1) // SEL_BLOCK + 1

    def rank_body(jj, cnt):
        for u in range(RANK_UNROLL):
            j = jj * RANK_UNROLL + u
            other = imp_s[pl.ds(j, 1), :]
            ahead = (other > imp) | ((other == imp) & (j < blk))
            cnt = cnt + ahead.astype(F32)
        return cnt

    trips = jnp.where(n_live > N_SELECT, n_live // RANK_UNROLL, 0)
    rank = lax.fori_loop(0, trips, rank_body, jnp.zeros((n_blk, qt), F32))

    def with_bias(bias):
        pad_rows = [jnp.zeros((LANES - hd - n_blk, qt), F32)] if LANES - hd > n_blk else []
        bias_t = jnp.concatenate([jnp.zeros((hd, qt), F32), bias] + pad_rows, axis=0).T
        return (qp.astype(F32) + jnp.concatenate([bias_t] * gqa, axis=0)).astype(BF16)

    selected = rank < N_SELECT
    qa = with_bias(jnp.where(selected & (blk * SEL_BLOCK < t0), 0.0, NEG_INF))
    qa_own = with_bias(jnp.where(selected, 0.0, NEG_INF))
    s_d = _dot_nt(ka_ref[0, 0, pl.ds(q0, qt), :], qa_own)
    s_d = jnp.where(lax.broadcasted_iota(jnp.int32, (qt, cols), 0) <= lane_q, s_d, NEG_INF)
    m0 = jnp.max(s_d, axis=0, keepdims=True)
    pv_d = _dot_tn(vs_ref[0, 0, pl.ds(q0, qt), :], jnp.exp((s_d - m0).astype(BF16)))
    l0, acc0 = pv_d[hd:hd + 1], pv_d[:hd]

    def sel_body(c, carry):
        m, l, acc = carry
        k0 = pl.multiple_of(c * SEL_CHUNK, SEL_CHUNK)
        kk = ka_ref[0, 0, pl.ds(k0, SEL_CHUNK), :]
        vv = vs_ref[0, 0, pl.ds(k0, SEL_CHUNK), :]
        s = _dot_nt(kk, qa)
        pv = _dot_tn(vv, jnp.exp((s - m).astype(BF16)))
        m_new = jnp.maximum(m, jnp.max(s, axis=0, keepdims=True))
        alpha = jnp.exp(m - m_new)
        l_lazy = alpha * (l + pv[hd:hd + 1])
        acc_lazy = alpha * (acc + pv[:hd])
        safe = jnp.max(m_new - m) <= LAZY_MAX_SLACK

        def exact():
            pv2 = _dot_tn(vv, jnp.exp((_dot_nt(kk, qa) - m_new).astype(BF16)))
            return alpha * l + pv2[hd:hd + 1], alpha * acc + pv2[:hd]

        l, acc = lax.cond(safe, lambda: (l_lazy, acc_lazy), exact)
        return m_new, l, acc

    n_chunks = (t0 + SEL_CHUNK - 1) // SEL_CHUNK
    _, l_sel, acc_sel = lax.fori_loop(0, n_chunks, sel_body, (m0, l0, acc0))
    o_sel = acc_sel / l_sel

    gate = jax.nn.sigmoid(gl_ref[0, 0, 0].astype(F32))
    out = gate[0:1] * o_cmp + gate[1:2] * o_sel + gate[2:3] * o_win
    pairs = [jnp.concatenate([out[:, r * qt:(r + 1) * qt], out[:, (r + 1) * qt:(r + 2) * qt]], axis=0).T
             for r in range(0, gqa, 2)]
    o_ref[0] = jnp.concatenate(pairs, axis=1).astype(BF16)


def _nsa_attention(q_hm, kc, vc, ka, vs, kw, vw, gl):
    bsz, n_heads, seq, _ = q_hm.shape
    hd = NSA_HEAD_DIM
    n_kv = ka.shape[1]
    gqa = n_heads // n_kv
    n_rows = kc.shape[2]
    n_cmp = (seq - CMP_LEN) // CMP_STRIDE + 1
    nq = seq // Q_TILE
    n_blk = seq // SEL_BLOCK
    assert N_SELECT <= n_blk <= LANES - hd and Q_TILE % (RANK_UNROLL * SEL_BLOCK) == 0 and Q_TILE % LANES == 0
    assert LANES == 2 * hd and gqa % 2 == 0
    assert seq % SEL_CHUNK == 0 and seq >= WINDOW + Q_TILE
    assert n_rows * CMP_STRIDE == seq and n_blk * (SEL_BLOCK // CMP_STRIDE) <= n_rows and CMP_LEN == 2 * CMP_STRIDE
    cols = gqa * Q_TILE
    gl2 = gl.reshape(bsz, nq, Q_TILE, n_kv, gqa, 3).transpose(0, 3, 1, 5, 4, 2).reshape(bsz, n_kv, nq, 3, cols)
    lane_q = np.arange(cols)[None, :] % Q_TILE
    n_idx = np.arange(n_rows)[:, None]
    crel = np.where(n_idx < n_cmp, n_idx * CMP_STRIDE + (CMP_LEN - 1) - lane_q, np.iinfo(np.int32).max).astype(np.int32)
    wrel = (np.arange(WINDOW + Q_TILE)[:, None] - lane_q).astype(np.int32)
    const = lambda a: pl.BlockSpec(a.shape, lambda b, g, i: (0, 0))
    kv_spec = lambda n: pl.BlockSpec((1, 1, n, LANES), lambda b, g, i: (b, g, 0, 0))
    return pl.pallas_call(
        _nsa_kernel,
        grid=(bsz, n_kv, nq),
        in_specs=[pl.BlockSpec((1, gqa, Q_TILE, LANES), lambda b, g, i: (b, g, i, 0)),
                  kv_spec(n_rows), kv_spec(n_rows), kv_spec(seq), kv_spec(seq), kv_spec(seq), kv_spec(seq),
                  pl.BlockSpec((1, 1, 1, 3, cols), lambda b, g, i: (b, g, i, 0, 0)),
                  const(crel), const(wrel)],
        out_specs=pl.BlockSpec((1, Q_TILE, gqa * hd), lambda b, g, i: (b, i, g)),
        out_shape=jax.ShapeDtypeStruct((bsz, seq, n_heads * hd), BF16),
        scratch_shapes=[pltpu.VMEM((Q_TILE // LANES, IMP_PAD + n_rows, LANES), F32),
                        pltpu.VMEM((n_blk, Q_TILE), F32)],
        compiler_params=pltpu.CompilerParams(dimension_semantics=("parallel", "parallel", "arbitrary")),
        name="nsa_attention",
    )(q_hm, kc, vc, ka, vs, kw, vw, gl2, jnp.asarray(crel), jnp.asarray(wrel))


def _merge_kernel(h_ref, mod_ref, ya_ref, yb_ref, yc_ref, gl_ref, wa_ref, wb_ref, wc_ref, wo_ref,
                  lng_ref, lnb_ref, o_ref, *, alpha):
    d = h_ref.shape[2]
    gate = lambda k: jax.nn.sigmoid(gl_ref[0, :, k * d:(k + 1) * d].astype(F32))
    merged = gate(0) * _dot(ya_ref[0], wa_ref[...])
    merged = merged + gate(1) * _dot(yb_ref[0], wb_ref[...])
    merged = merged + gate(2) * _dot(yc_ref[0], wc_ref[...])
    y = _dot(merged.astype(BF16), wo_ref[...])
    z = alpha * h_ref[0] + mod_ref[0, 2:3, :] * y
    o_ref[0] = _layer_norm(z, lng_ref[...], lnb_ref[...])


def _merge(h, mod3, ya, yb, yc, proj, col_gate, wa, wb, wc, wo, ln_g, ln_b, alpha, tm=512):
    bsz, seq, d = h.shape
    tm = min(tm, seq)
    act = pl.BlockSpec((1, tm, d), lambda b, i: (b, i, 0))
    wspec = pl.BlockSpec((d, d), lambda b, i: (0, 0))
    vec = pl.BlockSpec((1, d), lambda b, i: (0, 0))
    return pl.pallas_call(
        functools.partial(_merge_kernel, alpha=alpha),
        grid=(bsz, seq // tm),
        in_specs=[act, pl.BlockSpec((1, 3, d), lambda b, i: (b, 0, 0)), act, act, act,
                  pl.BlockSpec((1, tm, 3 * d), lambda b, i: (b, i, col_gate // (3 * d))),
                  wspec, wspec, wspec, wspec, vec, vec],
        out_specs=act,
        out_shape=jax.ShapeDtypeStruct((bsz, seq, d), F32),
        compiler_params=pltpu.CompilerParams(dimension_semantics=("parallel", "parallel")),
        name="merge",
    )(h, mod3, ya, yb, yc, proj, wa, wb, wc, wo, ln_g.reshape(1, d), ln_b.reshape(1, d))


def _token_mixer(h, mod3, w_in, conv_a_w, conv_a_b, norm_a_g, norm_a_b, w_a_out,
                 ssm_conv_w, ssm_conv_b, ssm_dt_bias, ssm_a_log, ssm_d, ssm_norm_w, w_b_out,
                 cmp_pe_k, cmp_pe_v, cmp_k_w1, cmp_k_w2, cmp_v_w1, cmp_v_w2, w_c_out, w_o, ln_g, ln_b, alpha):
    bsz, seq, d = h.shape
    ch = conv_a_w.shape[1]
    n_ssm_heads = ssm_dt_bias.shape[0]
    di = n_ssm_heads * SSM_HEAD_DIM
    gn = SSM_GROUPS * SSM_STATE
    n_q = w_c_out.shape[0]
    n_heads = n_q // NSA_HEAD_DIM
    kvd = NSA_KV_HEADS * NSA_HEAD_DIM
    gqa = n_heads // NSA_KV_HEADS
    n_gate = 3 * n_heads
    o_dt = 2 * ch + 2 * di + 2 * gn
    o_q = o_dt + n_ssm_heads
    o_gl = o_q + n_q + 6 * kvd
    o_g = o_gl + n_gate
    assert w_in.shape[1] == o_g + 3 * d
    col_z, col_xs, col_bc = 2 * ch, 2 * ch + di, 2 * ch + 2 * di
    col_q = o_dt
    col_gate = col_q + n_q
    col_kv = col_gate + 3 * d
    col_small = col_kv + 6 * kvd
    assert n_ssm_heads + n_gate <= LANES
    w_perm = jnp.concatenate(
        [w_in[:, :o_dt], w_in[:, o_q:o_q + n_q], w_in[:, o_g:], w_in[:, o_q + n_q:o_gl], w_in[:, o_dt:o_q],
         w_in[:, o_gl:o_g], jnp.zeros((d, LANES - n_ssm_heads - n_gate), F32)], axis=1).astype(BF16)

    proj = _inproj(h, mod3, w_perm)

    y_a = _conformer(proj, conv_a_w, conv_a_b, norm_a_g, norm_a_b)
    y_b = _ssd(proj, col_z, col_xs, col_bc, col_small, ssm_conv_w, ssm_conv_b, ssm_dt_bias, ssm_a_log,
               ssm_d, ssm_norm_w)

    q_hm, ka, kw, vs, vw = _nsa_prep(proj, col_q, col_kv, n_q, kvd)
    k_cmp, v_cmp = _compress(proj, col_kv, kvd, cmp_pe_k, cmp_pe_v, cmp_k_w1, cmp_k_w2, cmp_v_w1, cmp_v_w2)
    gl = proj[:, :, col_small + n_ssm_heads:col_small + n_ssm_heads + n_gate]
    y_c = _nsa_attention(q_hm, k_cmp, v_cmp, ka, vs, kw, vw, gl)

    return _merge(h, mod3, y_a, y_b, y_c, proj, col_gate, w_a_out.astype(BF16), w_b_out.astype(BF16),
                  w_c_out.astype(BF16), w_o.astype(BF16), ln_g, ln_b, alpha)


def kernel(x, c, ada_w, ada_b, ln_g, ln_b, ffn_w_gate, ffn_w_up, ffn_w_down, w_in, conv_a_w, conv_a_b, norm_a_g, norm_a_b, w_a_out, ssm_conv_w, ssm_conv_b, ssm_dt_bias, ssm_a_log, ssm_d, ssm_norm_w, w_b_out, cmp_pe_k, cmp_pe_v, cmp_k_w1, cmp_k_w2, cmp_v_w1, cmp_v_w2, w_c_out, w_o):
    depth = ada_w.shape[0]
    alpha = (2.0 * depth) ** 0.25
    mod = _ada_mod(c, ada_w, ada_b)
    wg, wu, wd = ffn_w_gate.astype(BF16), ffn_w_up.astype(BF16), ffn_w_down.astype(BF16)
    h = x
    for l in range(depth):
        h = _ffn_block(h, mod[l, :, 0:3], wg, wu, wd, l, 0, ln_g[l, 0], ln_b[l, 0], alpha)
        h = _token_mixer(h, mod[l, :, 3:6], w_in[l], conv_a_w[l], conv_a_b[l], norm_a_g[l], norm_a_b[l], w_a_out[l],
                         ssm_conv_w[l], ssm_conv_b[l], ssm_dt_bias[l], ssm_a_log[l], ssm_d[l], ssm_norm_w[l],
                         w_b_out[l], cmp_pe_k[l], cmp_pe_v[l], cmp_k_w1[l], cmp_k_w2[l], cmp_v_w1[l], cmp_v_w2[l],
                         w_c_out[l], w_o[l], ln_g[l, 1], ln_b[l, 1], alpha)
        h = _ffn_block(h, mod[l, :, 6:9], wg, wu, wd, l, 1, ln_g[l, 2], ln_b[l, 2], alpha)
    return h
```

```python
import functools
import math

import numpy as np
import jax
import jax.numpy as jnp
from jax import lax
from jax.experimental import pallas as pl
from jax.experimental.pallas import tpu as pltpu

F32 = jnp.float32
BF16 = jnp.bfloat16

LN_EPS = 1e-5
NEG_INF = -1e30
MACARON_WEIGHT = 0.5

SSM_HEAD_DIM = 64
SSM_GROUPS = 4
SSM_STATE = 128
SSM_CHUNK = 256

NSA_HEAD_DIM = 64
NSA_KV_HEADS = 4
CMP_LEN = 32
CMP_STRIDE = 16
SEL_BLOCK = 64
N_SELECT = 16
WINDOW = 512
FORCED_SCORE = 1e4
ROPE_THETA = 10000.0

LANES = 128
SUBLANES = 8
SEL_CHUNK = 1024
LAZY_MAX_SLACK = 60.0
RANK_UNROLL = 4
Q_TILE = 256
IMP_PAD = SUBLANES
CONV_HALO = 32
CONV_ACC_ROWS = 64
CONV_ACC_LANES = 256
INPROJ_MAX_TN = 2560


def _dot(a, b, precision=None):
    return jnp.dot(a, b, preferred_element_type=F32, precision=precision)


def _dot_nt(a, b, precision=None):
    return lax.dot_general(a, b, (((1,), (1,)), ((), ())), preferred_element_type=F32, precision=precision)


def _bf16_terms(x):
    hi = x.astype(BF16)
    r1 = x - hi.astype(F32)
    mid = r1.astype(BF16)
    lo = (r1 - mid.astype(F32)).astype(BF16)
    return hi, mid, lo


def _dot_exact_lhs(x, sel):
    hi, mid, lo = _bf16_terms(x)
    return _dot(hi, sel) + _dot(mid, sel) + _dot(lo, sel)


def _dot_exact_rhs(sel, x):
    hi, mid, lo = _bf16_terms(x)
    return _dot(sel, hi) + _dot(sel, mid) + _dot(sel, lo)


def _layer_norm(z, g, b):
    mu = jnp.mean(z, axis=-1, keepdims=True)
    zc = z - mu
    var = jnp.mean(zc * zc, axis=-1, keepdims=True)
    return zc * lax.rsqrt(var + LN_EPS) * g + b


def _silu(x):
    return x * jax.nn.sigmoid(x)


def _ada_kernel(c_ref, w_ref, b_ref, o_ref):
    a = _silu(c_ref[...]).astype(BF16)
    o_ref[0] = _dot(a, w_ref[0].astype(BF16)) + b_ref[0]


def _ada_mod(c, ada_w, ada_b):
    n_layer, d, n = ada_w.shape
    bsz = c.shape[0]
    rows = SUBLANES * pl.cdiv(bsz, SUBLANES)
    cp = jnp.zeros((rows, d), F32).at[:bsz].set(c)
    tn = d
    out = pl.pallas_call(
        _ada_kernel,
        grid=(n_layer, n // tn),
        in_specs=[pl.BlockSpec((rows, d), lambda l, j: (0, 0)),
                  pl.BlockSpec((1, d, tn), lambda l, j: (l, 0, j)),
                  pl.BlockSpec((1, 1, tn), lambda l, j: (l, 0, j))],
        out_specs=pl.BlockSpec((1, rows, tn), lambda l, j: (l, 0, j)),
        out_shape=jax.ShapeDtypeStruct((n_layer, rows, n), F32),
        name="ada_mod",
    )(cp, ada_w, ada_b.reshape(n_layer, 1, n))
    return out[:, :bsz].reshape(n_layer, bsz, 9, d)


def _ffn_kernel(h_ref, mod_ref, wg_ref, wu_ref, wd_ref, lng_ref, lnb_ref, o_ref, u_s, acc_s, *, alpha):
    f = pl.program_id(2)

    @pl.when(f == 0)
    def _():
        u_s[...] = (h_ref[0] * (1.0 + mod_ref[0, 1:2, :]) + mod_ref[0, 0:1, :]).astype(BF16)
        acc_s[...] = jnp.zeros_like(acc_s)

    u = u_s[...]
    g = _dot(u, wg_ref[...])
    up = _dot(u, wu_ref[...])
    a = (_silu(g) * up).astype(BF16)
    acc_s[...] += _dot(a, wd_ref[...])

    @pl.when(f == pl.num_programs(2) - 1)
    def _():
        z = alpha * h_ref[0] + (MACARON_WEIGHT * mod_ref[0, 2:3, :]) * acc_s[...]
        o_ref[0] = _layer_norm(z, lng_ref[...], lnb_ref[...])


def _ffn_block(h, mod3, wg, wu, wd, layer, half, ln_g, ln_b, alpha, tm=1024):
    bsz, seq, d = h.shape
    d_ff = wg.shape[3]
    tf = d_ff // 2 if (d_ff // 2) % LANES == 0 else d_ff
    tm = min(tm, seq)
    return pl.pallas_call(
        functools.partial(_ffn_kernel, alpha=alpha),
        grid=(bsz, seq // tm, d_ff // tf),
        in_specs=[pl.BlockSpec((1, tm, d), lambda b, i, f: (b, i, 0)),
                  pl.BlockSpec((1, 3, d), lambda b, i, f: (b, 0, 0)),
                  pl.BlockSpec((None, None, d, tf), lambda b, i, f: (layer, half, 0, f)),
                  pl.BlockSpec((None, None, d, tf), lambda b, i, f: (layer, half, 0, f)),
                  pl.BlockSpec((None, None, tf, d), lambda b, i, f: (layer, half, f, 0)),
                  pl.BlockSpec((1, d), lambda b, i, f: (0, 0)),
                  pl.BlockSpec((1, d), lambda b, i, f: (0, 0))],
        out_specs=pl.BlockSpec((1, tm, d), lambda b, i, f: (b, i, 0)),
        out_shape=jax.ShapeDtypeStruct((bsz, seq, d), F32),
        scratch_shapes=[pltpu.VMEM((tm, d), BF16), pltpu.VMEM((tm, d), F32)],
        compiler_params=pltpu.CompilerParams(dimension_semantics=("parallel", "parallel", "arbitrary")),
        name="ffn",
    )(h, mod3, wg, wu, wd, ln_g.reshape(1, d), ln_b.reshape(1, d))


def _inproj_kernel(h_ref, mod_ref, w_ref, o_ref, u_s):
    @pl.when(pl.program_id(2) == 0)
    def _():
        u_s[...] = (h_ref[0] * (1.0 + mod_ref[0, 1:2, :]) + mod_ref[0, 0:1, :]).astype(BF16)

    o_ref[0] = _dot(u_s[...], w_ref[...]).astype(BF16)


def _inproj(h, mod3, w, tm=1024):
    bsz, seq, d = h.shape
    n = w.shape[1]
    tn = max(t for t in range(LANES, INPROJ_MAX_TN + 1, LANES) if n % t == 0)
    tm = min(tm, seq)
    return pl.pallas_call(
        _inproj_kernel,
        grid=(bsz, seq // tm, n // tn),
        in_specs=[pl.BlockSpec((1, tm, d), lambda b, i, j: (b, i, 0)),
                  pl.BlockSpec((1, 3, d), lambda b, i, j: (b, 0, 0)),
                  pl.BlockSpec((d, tn), lambda b, i, j: (0, j))],
        out_specs=pl.BlockSpec((1, tm, tn), lambda b, i, j: (b, i, j)),
        out_shape=jax.ShapeDtypeStruct((bsz, seq, n), BF16),
        scratch_shapes=[pltpu.VMEM((tm, d), BF16)],
        compiler_params=pltpu.CompilerParams(dimension_semantics=("parallel", "parallel", "arbitrary")),
        name="inproj",
    )(h, mod3, w)


def _conf_kernel(cur_ref, halo_ref, w_ref, cb_ref, g_ref, b_ref, o_ref, buf_s, sh_s, y_s, *, width):
    i = pl.program_id(1)
    ts, ch = y_s.shape
    x = cur_ref[0].astype(F32)
    buf_s[CONV_HALO:, :] = x[:, :ch] * jax.nn.sigmoid(x[:, ch:])
    xh = halo_ref[0].astype(F32)
    ah = xh[:, :ch] * jax.nn.sigmoid(xh[:, ch:])
    buf_s[0:CONV_HALO, :] = jnp.where(i > 0, ah, 0.0)
    off = CONV_HALO - (width - 1)
    n_sh = sh_s.shape[1]
    for r in range(1, SUBLANES):
        sh_s[r - 1] = buf_s[r:r + n_sh, :]
    rc, lc = CONV_ACC_ROWS, CONV_ACC_LANES
    for r0 in range(0, ts, rc):
        for l0 in range(0, ch, lc):
            acc = jnp.zeros((rc // SUBLANES, SUBLANES, lc), F32) + cb_ref[:, l0:l0 + lc]
            for k in range(width):
                a, r = divmod(k + off, SUBLANES)
                rows = slice(r0 + SUBLANES * a, r0 + SUBLANES * a + rc)
                tap = buf_s[rows, l0:l0 + lc] if r == 0 else sh_s[r - 1, rows, l0:l0 + lc]
                acc = acc + tap.reshape(rc // SUBLANES, SUBLANES, lc) * w_ref[k, :, l0:l0 + lc]
            y_s[r0:r0 + rc, l0:l0 + lc] = acc.reshape(rc, lc)
    yn = _layer_norm(y_s[...], g_ref[...], b_ref[...])
    o_ref[0] = _silu(yn).astype(BF16)


def _conformer(proj, conv_w, conv_b, norm_g, norm_b, ts=256):
    bsz, seq, _ = proj.shape
    width, ch = conv_w.shape
    assert width - 1 <= CONV_HALO
    ts = min(ts, seq)
    wpad = jnp.zeros((CONV_HALO, ch), F32).at[:width].set(conv_w)
    wpad = jnp.broadcast_to(wpad[:, None, :], (CONV_HALO, SUBLANES, ch))
    hb = ts // CONV_HALO
    return pl.pallas_call(
        functools.partial(_conf_kernel, width=width),
        grid=(bsz, seq // ts),
        in_specs=[pl.BlockSpec((1, ts, 2 * ch), lambda b, i: (b, i, 0)),
                  pl.BlockSpec((1, CONV_HALO, 2 * ch), lambda b, i: (b, jnp.maximum(i * hb - 1, 0), 0)),
                  pl.BlockSpec((CONV_HALO, SUBLANES, ch), lambda b, i: (0, 0, 0)),
                  pl.BlockSpec((1, ch), lambda b, i: (0, 0)),
                  pl.BlockSpec((1, ch), lambda b, i: (0, 0)),
                  pl.BlockSpec((1, ch), lambda b, i: (0, 0))],
        out_specs=pl.BlockSpec((1, ts, ch), lambda b, i: (b, i, 0)),
        out_shape=jax.ShapeDtypeStruct((bsz, seq, ch), BF16),
        scratch_shapes=[pltpu.VMEM((CONV_HALO + ts, ch), F32),
                        pltpu.VMEM((SUBLANES - 1, CONV_HALO + ts - SUBLANES, ch), F32),
                        pltpu.VMEM((ts, ch), F32)],
        compiler_params=pltpu.CompilerParams(dimension_semantics=("parallel", "parallel")),
        name="conformer",
    )(proj, proj, wpad, conv_b.reshape(1, ch), norm_g.reshape(1, ch), norm_b.reshape(1, ch))


def _ssd_kernel(z_ref, xs_ref, bc_ref, sm_ref, cw_ref, cbias_ref, dtb_ref, alog_ref, dfull_ref, nw_ref, ex_ref,
                o_ref, buf_s, st_s, y_s, *, conv_width):
    c = pl.program_id(1)
    lc, di = y_s.shape
    n_grp, n_state, gw = st_s.shape
    hpg = gw // SSM_HEAD_DIM
    carry = SUBLANES

    @pl.when(c == 0)
    def _():
        buf_s[0:carry, :] = jnp.zeros((carry, buf_s.shape[1]), F32)
        st_s[...] = jnp.zeros_like(st_s)

    @pl.when(c > 0)
    def _():
        buf_s[0:carry, :] = buf_s[lc:lc + carry, :]

    buf_s[carry:, 0:di] = xs_ref[0].astype(F32)
    buf_s[carry:, di:] = bc_ref[0].astype(F32)
    acc = jnp.zeros((lc, buf_s.shape[1]), F32) + cbias_ref[...]
    xb = buf_s[...]
    for k in range(conv_width):
        delay = conv_width - 1 - k
        tap = xb if delay == 0 else pltpu.roll(xb, delay, 0)
        acc = acc + tap[carry:, :] * cw_ref[k:k + 1, :]
    xbc = _silu(acc)
    xs = xbc[:, 0:di]
    gn = n_grp * n_state
    bm = xbc[:, di:di + gn]
    cm = xbc[:, di + gn:di + 2 * gn]

    x_dt = sm_ref[0].astype(F32) + dtb_ref[...]
    dt = jnp.maximum(x_dt, 0.0) + jnp.log(1.0 + jnp.exp(-jnp.abs(x_dt)))
    adt = dt * (-jnp.exp(alog_ref[...]))
    row = lax.broadcasted_iota(jnp.int32, (lc, lc), 0)
    col = lax.broadcasted_iota(jnp.int32, (lc, lc), 1)
    causal = row >= col
    acs = _dot_exact_rhs(causal.astype(BF16), adt)
    acs_t = acs.T
    ex = ex_ref[...]
    acs_full = _dot_exact_lhs(acs, ex)
    dt_full = _dot_exact_lhs(dt, ex)
    last_full = acs_full[lc - 1:lc, :]
    xdt = xs * dt_full
    e_acs = jnp.exp(acs_full)
    x_in = (xdt * jnp.exp(last_full - acs_full)).astype(BF16)
    chunk_decay = jnp.exp(last_full)
    lane_head = lax.broadcasted_iota(jnp.int32, (lc, gw), 1) // SSM_HEAD_DIM

    for g in range(n_grp):
        bg = bm[:, g * n_state:(g + 1) * n_state]
        cg = cm[:, g * n_state:(g + 1) * n_state].astype(BF16)
        cb = _dot_nt(cg, bg.astype(BF16))
        hg = st_s[g]
        y_g = _dot(cg, hg.astype(BF16)) * e_acs[:, g * gw:(g + 1) * gw]
        xg = xdt[:, g * gw:(g + 1) * gw]
        for r in range(hpg):
            hd = g * hpg + r
            seg = jnp.exp(jnp.where(causal, acs[:, hd:hd + 1] - acs_t[hd:hd + 1, :], NEG_INF))
            xr = jnp.where(lane_head == r, xg, 0.0).astype(BF16)
            y_g = y_g + _dot((cb * seg).astype(BF16), xr)
        st_s[g] = hg * chunk_decay[:, g * gw:(g + 1) * gw] + _dot(bg.T.astype(BF16), x_in[:, g * gw:(g + 1) * gw])
        y_s[:, g * gw:(g + 1) * gw] = y_g

    y = y_s[...] + dfull_ref[...] * xs
    y = y * _silu(z_ref[0].astype(F32))
    y = y * lax.rsqrt(jnp.mean(y * y, axis=-1, keepdims=True) + LN_EPS) * nw_ref[...]
    o_ref[0] = y.astype(BF16)


def _ssd(proj, col_z, col_xs, col_bc, col_small, conv_w, conv_b, dt_bias, a_log, d_skip, norm_w):
    bsz, seq, _ = proj.shape
    n_heads = dt_bias.shape[0]
    di = n_heads * SSM_HEAD_DIM
    conv_width, conv_dim = conv_w.shape
    gn = SSM_GROUPS * SSM_STATE
    assert conv_dim == di + 2 * gn and 2 * gn == di
    lc = math.gcd(SSM_CHUNK, seq)
    gw = di // SSM_GROUPS
    assert conv_width - 1 <= SUBLANES
    cw = jnp.zeros((SUBLANES, conv_dim), F32).at[:conv_width].set(conv_w)
    pad = lambda v: jnp.zeros((1, LANES), F32).at[0, :n_heads].set(v)
    expand = np.zeros((LANES, di), np.float32)
    expand[np.arange(di) // SSM_HEAD_DIM, np.arange(di)] = 1.0
    d_full = jnp.repeat(d_skip, SSM_HEAD_DIM).reshape(1, di)
    cblk = lambda col, w: pl.BlockSpec((1, lc, w), lambda b, c: (b, c, col // w))
    const = lambda shape: pl.BlockSpec(shape, lambda b, c: (0,) * len(shape))
    return pl.pallas_call(
        functools.partial(_ssd_kernel, conv_width=conv_width),
        grid=(bsz, seq // lc),
        in_specs=[cblk(col_z, di), cblk(col_xs, di), cblk(col_bc, di), cblk(col_small, LANES),
                  const((SUBLANES, conv_dim)), const((1, conv_dim)), const((1, LANES)), const((1, LANES)),
                  const((1, di)), const((1, di)), const((LANES, di))],
        out_specs=pl.BlockSpec((1, lc, di), lambda b, c: (b, c, 0)),
        out_shape=jax.ShapeDtypeStruct((bsz, seq, di), BF16),
        scratch_shapes=[pltpu.VMEM((SUBLANES + lc, conv_dim), F32),
                        pltpu.VMEM((SSM_GROUPS, SSM_STATE, gw), F32),
                        pltpu.VMEM((lc, di), F32)],
        compiler_params=pltpu.CompilerParams(dimension_semantics=("parallel", "arbitrary")),
        name="ssd",
    )(proj, proj, proj, proj, cw, conv_b.reshape(1, conv_dim), pad(dt_bias), pad(a_log), d_full,
      norm_w.reshape(1, di), jnp.asarray(expand, BF16))


def _rope_tables(pos):
    half = NSA_HEAD_DIM // 2
    inv_freq = ROPE_THETA ** (-np.arange(half, dtype=np.float32) / half)
    ang = jnp.asarray(pos, F32)[:, None] * jnp.asarray(inv_freq)[None, :]
    cos, sin, zero = jnp.cos(ang), jnp.sin(ang), jnp.zeros_like(ang)
    cos_t = jnp.concatenate([cos, cos, cos, cos], -1)
    sin_a = jnp.concatenate([-sin, zero, -sin, zero], -1)
    sin_b = jnp.concatenate([zero, sin, zero, sin], -1)
    return cos_t, sin_a, sin_b


def _rope(x, cos_t, sin_a, sin_b):
    half = NSA_HEAD_DIM // 2
    return x * cos_t + pltpu.roll(x, LANES - half, 1) * sin_a + pltpu.roll(x, half, 1) * sin_b


def _pair_slabs(x):
    return x, pltpu.roll(x, NSA_HEAD_DIM, 1)


def _nsa_prep_kernel(q_ref, kv_ref, cos_ref, sa_ref, sb_ref, qo_ref, ka_ref, kw_ref, vs_ref, vw_ref,
                     *, q_scale, kvd):
    i = pl.program_id(1)
    ts = q_ref.shape[1]
    hd = NSA_HEAD_DIM
    cos_t, sin_a, sin_b = cos_ref[...], sa_ref[...], sb_ref[...]
    lane = lax.broadcasted_iota(jnp.int32, (ts, LANES), 1)
    low = lane < hd
    ones_lane = lane == hd
    for j in range(q_ref.shape[2] // LANES):
        xq = _rope(q_ref[0, :, j * LANES:(j + 1) * LANES].astype(F32), cos_t, sin_a, sin_b) * q_scale
        for u, slab in enumerate(_pair_slabs(xq)):
            qo_ref[0, 2 * j + u] = jnp.where(low, slab, 0.0).astype(BF16)
    blk = (i * ts + lax.broadcasted_iota(jnp.int32, (ts, LANES), 0)) // SEL_BLOCK
    onehot = (blk == lane - hd).astype(F32)
    slab_of = lambda c0, j: kv_ref[0, :, c0 + j * LANES:c0 + (j + 1) * LANES].astype(F32)
    for j in range(kvd // LANES):
        for u, slab in enumerate(_pair_slabs(_rope(slab_of(2 * kvd, j), cos_t, sin_a, sin_b))):
            ka_ref[0, 2 * j + u] = jnp.where(low, slab, onehot).astype(BF16)
        for u, slab in enumerate(_pair_slabs(_rope(slab_of(4 * kvd, j), cos_t, sin_a, sin_b))):
            kw_ref[0, 2 * j + u] = slab.astype(BF16)
        for u, slab in enumerate(_pair_slabs(slab_of(3 * kvd, j))):
            vs_ref[0, 2 * j + u] = jnp.where(ones_lane, 1.0, slab).astype(BF16)
        for u, slab in enumerate(_pair_slabs(slab_of(5 * kvd, j))):
            vw_ref[0, 2 * j + u] = jnp.where(ones_lane, 1.0, slab).astype(BF16)


def _nsa_prep(proj, col_q, col_kv, n_q, kvd, ts=1024):
    bsz, seq, _ = proj.shape
    hd = NSA_HEAD_DIM
    ts = min(ts, seq)
    n_heads, n_kv = n_q // hd, kvd // hd
    assert 2 * hd == LANES and seq // SEL_BLOCK <= LANES - hd
    cos_t, sin_a, sin_b = _rope_tables(np.arange(seq))
    tab = pl.BlockSpec((ts, LANES), lambda b, i: (i, 0))
    per_head = lambda n: pl.BlockSpec((1, n, ts, LANES), lambda b, i: (b, 0, i, 0))
    return pl.pallas_call(
        functools.partial(_nsa_prep_kernel, q_scale=hd ** -0.5, kvd=kvd),
        grid=(bsz, seq // ts),
        in_specs=[pl.BlockSpec((1, ts, n_q), lambda b, i: (b, i, col_q // n_q)),
                  pl.BlockSpec((1, ts, 6 * kvd), lambda b, i: (b, i, col_kv // (6 * kvd))),
                  tab, tab, tab],
        out_specs=[per_head(n_heads)] + [per_head(n_kv)] * 4,
        out_shape=[jax.ShapeDtypeStruct((bsz, n_heads, seq, LANES), BF16)]
        + [jax.ShapeDtypeStruct((bsz, n_kv, seq, LANES), BF16)] * 4,
        compiler_params=pltpu.CompilerParams(dimension_semantics=("parallel", "parallel")),
        name="nsa_prep",
    )(proj, proj, cos_t, sin_a, sin_b)


def _gelu_tanh(x):
    return 0.5 * x * (1.0 + jnp.tanh(math.sqrt(2.0 / math.pi) * (x + 0.044715 * (x * x * x))))


def _compress_kernel(x_ref, w1k_ref, w1v_ref, w2k_ref, w2v_ref, pek_ref, pev_ref,
                     cos_ref, sa_ref, sb_ref, ko_ref, vo_ref, x_s):
    seq = x_ref.shape[1]
    kvd = x_ref.shape[2] // 2
    n_rows = ko_ref.shape[2]
    n_pair = kvd // LANES
    for c in range(2 * n_pair):
        x_s[c, 0:seq, :] = x_ref[0, :, c * LANES:(c + 1) * LANES].astype(F32)
        x_s[c, seq:, :] = jnp.zeros((x_s.shape[1] - seq, LANES), F32)
    cos_t, sin_a, sin_b = cos_ref[...], sa_ref[...], sb_ref[...]

    def pair(c, w1_ref, w2_ref, pe_ref):
        taps = [x_s[c, pl.ds(p, n_rows, stride=CMP_STRIDE), :].astype(BF16) for p in range(CMP_LEN)]
        w1 = w1_ref[...]
        hid = _dot(jnp.concatenate(taps, axis=1), w1) + _dot(pe_ref[...], w1)[0:1]
        return _dot(_gelu_tanh(hid).astype(BF16), w2_ref[...])

    for j in range(n_pair):
        k = _rope(pair(j, w1k_ref, w2k_ref, pek_ref), cos_t, sin_a, sin_b)
        for u, slab in enumerate(_pair_slabs(k)):
            ko_ref[0, 2 * j + u] = slab.astype(BF16)
        for u, slab in enumerate(_pair_slabs(pair(n_pair + j, w1v_ref, w2v_ref, pev_ref))):
            vo_ref[0, 2 * j + u] = slab.astype(BF16)


def _compress(proj, col_kv, kvd, pe_k, pe_v, k_w1, k_w2, v_w1, v_w2):
    bsz, seq, _ = proj.shape
    hd = NSA_HEAD_DIM
    n_kv = kvd // hd
    assert seq % CMP_STRIDE == 0 and 2 * hd == LANES and col_kv % (2 * kvd) == 0
    n_rows = seq // CMP_STRIDE
    hidden = k_w1.shape[1]

    def pair_w1(w1):
        w = w1.reshape(CMP_LEN, 1, hd, 1, hidden) * jnp.eye(2, dtype=F32).reshape(1, 2, 1, 2, 1)
        return w.reshape(CMP_LEN * LANES, 2 * hidden).astype(BF16)

    pair_w2 = lambda w2: jnp.kron(jnp.eye(2, dtype=F32), w2).astype(BF16)
    pe_rows = lambda pe: jnp.broadcast_to(pe.reshape(1, CMP_LEN, 1, hd), (SUBLANES, CMP_LEN, 2, hd)).reshape(
        SUBLANES, CMP_LEN * LANES).astype(BF16)
    cos_t, sin_a, sin_b = _rope_tables(np.arange(n_rows) * CMP_STRIDE + CMP_LEN - 1)
    const = lambda shape: pl.BlockSpec(shape, lambda b: (0,) * len(shape))
    ospec = pl.BlockSpec((1, n_kv, n_rows, LANES), lambda b: (b, 0, 0, 0))
    return pl.pallas_call(
        _compress_kernel,
        grid=(bsz,),
        in_specs=[pl.BlockSpec((1, seq, 2 * kvd), lambda b: (b, 0, col_kv // (2 * kvd))),
                  const((CMP_LEN * LANES, 2 * hidden)), const((CMP_LEN * LANES, 2 * hidden)),
                  const((2 * hidden, LANES)), const((2 * hidden, LANES)),
                  const((SUBLANES, CMP_LEN * LANES)), const((SUBLANES, CMP_LEN * LANES)),
                  const((n_rows, LANES)), const((n_rows, LANES)), const((n_rows, LANES))],
        out_specs=[ospec, ospec],
        out_shape=[jax.ShapeDtypeStruct((bsz, n_kv, n_rows, LANES), BF16)] * 2,
        scratch_shapes=[pltpu.VMEM((2 * kvd // LANES, seq + CMP_LEN, LANES), F32)],
        compiler_params=pltpu.CompilerParams(dimension_semantics=("parallel",)),
        name="compress",
    )(proj, pair_w1(k_w1), pair_w1(v_w1), pair_w2(k_w2), pair_w2(v_w2), pe_rows(pe_k), pe_rows(pe_v),
      cos_t, sin_a, sin_b)


def _dot_tn(a, b):
    return lax.dot_general(a, b, (((0,), (0,)), ((), ())), preferred_element_type=F32)


def _nsa_kernel(q_ref, kc_ref, vc_ref, ka_ref, vs_ref, kw_ref, vw_ref, gl_ref, crel_ref, wrel_ref, o_ref,
                psum_s, imp_s):
    i = pl.program_id(2)
    gqa, qt = q_ref.shape[1:3]
    hd = NSA_HEAD_DIM
    cols = gqa * qt
    n_blk = imp_s.shape[0]
    n_rows = kc_ref.shape[2]
    band = WINDOW + qt
    qp = q_ref[0].reshape(cols, LANES)
    lane_q = lax.broadcasted_iota(jnp.int32, (1, cols), 1) % qt
    t0 = i * qt
    q0 = pl.multiple_of(t0, qt)
    w0 = pl.multiple_of(jnp.maximum(t0 - WINDOW, 0), qt)

    k_all = jnp.concatenate([kc_ref[0, 0], kw_ref[0, 0, pl.ds(w0, band), :]], axis=0)
    s_all = _dot_nt(k_all, qp)
    s_c, s_w = s_all[:n_rows], s_all[n_rows:]

    cmask = crel_ref[...] <= t0
    s_c = jnp.where(cmask, s_c, NEG_INF)
    e_c = jnp.exp(s_c - jnp.max(s_c, axis=0, keepdims=True))
    has_cmp = t0 + lane_q >= CMP_LEN - 1
    p_cmp = e_c * jnp.where(has_cmp, 1.0 / jnp.sum(e_c, axis=0, keepdims=True), 0.0)
    o_cmp = _dot_tn(vc_ref[0, 0], p_cmp.astype(BF16))[:hd]

    w_rel = wrel_ref[...]
    wmask = (w_rel <= t0 - w0) & (w_rel > t0 - w0 - WINDOW)
    s_w = jnp.where(wmask, s_w, NEG_INF)
    e_w = jnp.exp((s_w - jnp.max(s_w, axis=0, keepdims=True)).astype(BF16))
    pv_w = _dot_tn(vw_ref[0, 0, pl.ds(w0, band), :], e_w)
    o_win = pv_w[:hd] * (1.0 / pv_w[hd:hd + 1])

    p_sum = p_cmp[:, 0:qt]
    for r in range(1, gqa):
        p_sum = p_sum + p_cmp[:, r * qt:(r + 1) * qt]
    ratio = SEL_BLOCK // CMP_STRIDE
    imp_parts = []
    for c in range(qt // LANES):
        psum_s[c, 0:IMP_PAD, :] = jnp.zeros((IMP_PAD, LANES), F32)
        psum_s[c, IMP_PAD:, :] = p_sum[:, c * LANES:(c + 1) * LANES]
        part = psum_s[c, pl.ds(IMP_PAD - 1, n_blk, stride=ratio), :]
        for k in range(ratio):
            part = part + psum_s[c, pl.ds(IMP_PAD + k, n_blk, stride=ratio), :]
        imp_parts.append(part)
    imp = jnp.concatenate(imp_parts, axis=1)
    blk = lax.broadcasted_iota(jnp.int32, (n_blk, qt), 0)
    tq = t0 + lax.broadcasted_iota(jnp.int32, (n_blk, qt), 1)
    cur = tq // SEL_BLOCK
    forced = (blk == 0) | (blk == cur) | (blk == cur - 1)
    imp = jnp.where(forced, FORCED_SCORE, jnp.where(blk * SEL_BLOCK > tq, -1.0, imp))
    imp_s[...] = imp

    n_live = (t0 + qt - 1) // SEL_BLOCK + 1

    def rank_body(jj, cnt):
        for u in range(RANK_UNROLL):
            j = jj * RANK_UNROLL + u
            other = imp_s[pl.ds(j, 1), :]
            ahead = (other > imp) | ((other == imp) & (j < blk))
            cnt = cnt + ahead.astype(F32)
        return cnt

    trips = jnp.where(n_live > N_SELECT, n_live // RANK_UNROLL, 0)
    rank = lax.fori_loop(0, trips, rank_body, jnp.zeros((n_blk, qt), F32))

    def with_bias(bias):
        pad_rows = [jnp.zeros((LANES - hd - n_blk, qt), F32)] if LANES - hd > n_blk else []
        bias_t = jnp.concatenate([jnp.zeros((hd, qt), F32), bias] + pad_rows, axis=0).T
        return (qp.astype(F32) + jnp.concatenate([bias_t] * gqa, axis=0)).astype(BF16)

    selected = rank < N_SELECT
    qa = with_bias(jnp.where(selected & (blk * SEL_BLOCK < t0), 0.0, NEG_INF))
    qa_own = with_bias(jnp.where(selected, 0.0, NEG_INF))
    s_d = _dot_nt(ka_ref[0, 0, pl.ds(q0, qt), :], qa_own)
    s_d = jnp.where(lax.broadcasted_iota(jnp.int32, (qt, cols), 0) <= lane_q, s_d, NEG_INF)
    m0 = jnp.max(s_d, axis=0, keepdims=True)
    pv_d = _dot_tn(vs_ref[0, 0, pl.ds(q0, qt), :], jnp.exp((s_d - m0).astype(BF16)))
    l0, acc0 = pv_d[hd:hd + 1], pv_d[:hd]

    def sel_body(c, carry):
        m, l, acc = carry
        k0 = pl.multiple_of(c * SEL_CHUNK, SEL_CHUNK)
        kk = ka_ref[0, 0, pl.ds(k0, SEL_CHUNK), :]
        vv = vs_ref[0, 0, pl.ds(k0, SEL_CHUNK), :]
        s = _dot_nt(kk, qa)
        pv = _dot_tn(vv, jnp.exp((s - m).astype(BF16)))
        m_new = jnp.maximum(m, jnp.max(s, axis=0, keepdims=True))
        alpha = jnp.exp(m - m_new)
        l_lazy = alpha * (l + pv[hd:hd + 1])
        acc_lazy = alpha * (acc + pv[:hd])
        safe = jnp.max(m_new - m) <= LAZY_MAX_SLACK

        def exact():
            pv2 = _dot_tn(vv, jnp.exp((_dot_nt(kk, qa) - m_new).astype(BF16)))
            return alpha * l + pv2[hd:hd + 1], alpha * acc + pv2[:hd]

        l, acc = lax.cond(safe, lambda: (l_lazy, acc_lazy), exact)
        return m_new, l, acc

    n_chunks = (t0 + SEL_CHUNK - 1) // SEL_CHUNK
    _, l_sel, acc_sel = lax.fori_loop(0, n_chunks, sel_body, (m0, l0, acc0))
    o_sel = acc_sel / l_sel

    gate = jax.nn.sigmoid(gl_ref[0, 0, 0].astype(F32))
    out = gate[0:1] * o_cmp + gate[1:2] * o_sel + gate[2:3] * o_win
    pairs = [jnp.concatenate([out[:, r * qt:(r + 1) * qt], out[:, (r + 1) * qt:(r + 2) * qt]], axis=0).T
             for r in range(0, gqa, 2)]
    o_ref[0] = jnp.concatenate(pairs, axis=1).astype(BF16)


def _nsa_attention(q_hm, kc, vc, ka, vs, kw, vw, gl):
    bsz, n_heads, seq, _ = q_hm.shape
    hd = NSA_HEAD_DIM
    n_kv = ka.shape[1]
    gqa = n_heads // n_kv
    n_rows = kc.shape[2]
    n_cmp = (seq - CMP_LEN) // CMP_STRIDE + 1
    nq = seq // Q_TILE
    n_blk = seq // SEL_BLOCK
    assert N_SELECT <= n_blk <= LANES - hd and Q_TILE % (RANK_UNROLL * SEL_BLOCK) == 0 and Q_TILE % LANES == 0
    assert LANES == 2 * hd and gqa % 2 == 0
    assert seq % SEL_CHUNK == 0 and seq >= WINDOW + Q_TILE
    assert n_rows * CMP_STRIDE == seq and n_blk * (SEL_BLOCK // CMP_STRIDE) <= n_rows and CMP_LEN == 2 * CMP_STRIDE
    cols = gqa * Q_TILE
    gl2 = gl.reshape(bsz, nq, Q_TILE, n_kv, gqa, 3).transpose(0, 3, 1, 5, 4, 2).reshape(bsz, n_kv, nq, 3, cols)
    lane_q = np.arange(cols)[None, :] % Q_TILE
    n_idx = np.arange(n_rows)[:, None]
    crel = np.where(n_idx < n_cmp, n_idx * CMP_STRIDE + (CMP_LEN - 1) - lane_q, np.iinfo(np.int32).max).astype(np.int32)
    wrel = (np.arange(WINDOW + Q_TILE)[:, None] - lane_q).astype(np.int32)
    const = lambda a: pl.BlockSpec(a.shape, lambda b, g, i: (0, 0))
    kv_spec = lambda n: pl.BlockSpec((1, 1, n, LANES), lambda b, g, i: (b, g, 0, 0))
    return pl.pallas_call(
        _nsa_kernel,
        grid=(bsz, n_kv, nq),
        in_specs=[pl.BlockSpec((1, gqa, Q_TILE, LANES), lambda b, g, i: (b, g, i, 0)),
                  kv_spec(n_rows), kv_spec(n_rows), kv_spec(seq), kv_spec(seq), kv_spec(seq), kv_spec(seq),
                  pl.BlockSpec((1, 1, 1, 3, cols), lambda b, g, i: (b, g, i, 0, 0)),
                  const(crel), const(wrel)],
        out_specs=pl.BlockSpec((1, Q_TILE, gqa * hd), lambda b, g, i: (b, i, g)),
        out_shape=jax.ShapeDtypeStruct((bsz, seq, n_heads * hd), BF16),
        scratch_shapes=[pltpu.VMEM((Q_TILE // LANES, IMP_PAD + n_rows, LANES), F32),
                        pltpu.VMEM((n_blk, Q_TILE), F32)],
        compiler_params=pltpu.CompilerParams(dimension_semantics=("parallel", "parallel", "arbitrary")),
        name="nsa_attention",
    )(q_hm, kc, vc, ka, vs, kw, vw, gl2, jnp.asarray(crel), jnp.asarray(wrel))


def _merge_kernel(h_ref, mod_ref, ya_ref, yb_ref, yc_ref, gl_ref, wa_ref, wb_ref, wc_ref, wo_ref,
                  lng_ref, lnb_ref, o_ref, *, alpha):
    d = h_ref.shape[2]
    gate = lambda k: jax.nn.sigmoid(gl_ref[0, :, k * d:(k + 1) * d].astype(F32))
    merged = gate(0) * _dot(ya_ref[0], wa_ref[...])
    merged = merged + gate(1) * _dot(yb_ref[0], wb_ref[...])
    merged = merged + gate(2) * _dot(yc_ref[0], wc_ref[...])
    y = _dot(merged.astype(BF16), wo_ref[...])
    z = alpha * h_ref[0] + mod_ref[0, 2:3, :] * y
    o_ref[0] = _layer_norm(z, lng_ref[...], lnb_ref[...])


def _merge(h, mod3, ya, yb, yc, proj, col_gate, wa, wb, wc, wo, ln_g, ln_b, alpha, tm=512):
    bsz, seq, d = h.shape
    tm = min(tm, seq)
    act = pl.BlockSpec((1, tm, d), lambda b, i: (b, i, 0))
    wspec = pl.BlockSpec((d, d), lambda b, i: (0, 0))
    vec = pl.BlockSpec((1, d), lambda b, i: (0, 0))
    return pl.pallas_call(
        functools.partial(_merge_kernel, alpha=alpha),
        grid=(bsz, seq // tm),
        in_specs=[act, pl.BlockSpec((1, 3, d), lambda b, i: (b, 0, 0)), act, act, act,
                  pl.BlockSpec((1, tm, 3 * d), lambda b, i: (b, i, col_gate // (3 * d))),
                  wspec, wspec, wspec, wspec, vec, vec],
        out_specs=act,
        out_shape=jax.ShapeDtypeStruct((bsz, seq, d), F32),
        compiler_params=pltpu.CompilerParams(dimension_semantics=("parallel", "parallel")),
        name="merge",
    )(h, mod3, ya, yb, yc, proj, wa, wb, wc, wo, ln_g.reshape(1, d), ln_b.reshape(1, d))


def _token_mixer(h, mod3, w_in, conv_a_w, conv_a_b, norm_a_g, norm_a_b, w_a_out,
                 ssm_conv_w, ssm_conv_b, ssm_dt_bias, ssm_a_log, ssm_d, ssm_norm_w, w_b_out,
                 cmp_pe_k, cmp_pe_v, cmp_k_w1, cmp_k_w2, cmp_v_w1, cmp_v_w2, w_c_out, w_o, ln_g, ln_b, alpha):
    bsz, seq, d = h.shape
    ch = conv_a_w.shape[1]
    n_ssm_heads = ssm_dt_bias.shape[0]
    di = n_ssm_heads * SSM_HEAD_DIM
    gn = SSM_GROUPS * SSM_STATE
    n_q = w_c_out.shape[0]
    n_heads = n_q // NSA_HEAD_DIM
    kvd = NSA_KV_HEADS * NSA_HEAD_DIM
    gqa = n_heads // NSA_KV_HEADS
    n_gate = 3 * n_heads
    o_dt = 2 * ch + 2 * di + 2 * gn
    o_q = o_dt + n_ssm_heads
    o_gl = o_q + n_q + 6 * kvd
    o_g = o_gl + n_gate
    assert w_in.shape[1] == o_g + 3 * d
    col_z, col_xs, col_bc = 2 * ch, 2 * ch + di, 2 * ch + 2 * di
    col_q = o_dt
    col_gate = col_q + n_q
    col_kv = col_gate + 3 * d
    col_small = col_kv + 6 * kvd
    assert n_ssm_heads + n_gate <= LANES
    w_perm = jnp.concatenate(
        [w_in[:, :o_dt], w_in[:, o_q:o_q + n_q], w_in[:, o_g:], w_in[:, o_q + n_q:o_gl], w_in[:, o_dt:o_q],
         w_in[:, o_gl:o_g], jnp.zeros((d, LANES - n_ssm_heads - n_gate), F32)], axis=1).astype(BF16)

    proj = _inproj(h, mod3, w_perm)

    y_a = _conformer(proj, conv_a_w, conv_a_b, norm_a_g, norm_a_b)
    y_b = _ssd(proj, col_z, col_xs, col_bc, col_small, ssm_conv_w, ssm_conv_b, ssm_dt_bias, ssm_a_log,
               ssm_d, ssm_norm_w)

    q_hm, ka, kw, vs, vw = _nsa_prep(proj, col_q, col_kv, n_q, kvd)
    k_cmp, v_cmp = _compress(proj, col_kv, kvd, cmp_pe_k, cmp_pe_v, cmp_k_w1, cmp_k_w2, cmp_v_w1, cmp_v_w2)
    gl = proj[:, :, col_small + n_ssm_heads:col_small + n_ssm_heads + n_gate]
    y_c = _nsa_attention(q_hm, k_cmp, v_cmp, ka, vs, kw, vw, gl)

    return _merge(h, mod3, y_a, y_b, y_c, proj, col_gate, w_a_out.astype(BF16), w_b_out.astype(BF16),
                  w_c_out.astype(BF16), w_o.astype(BF16), ln_g, ln_b, alpha)


def kernel(x, c, ada_w, ada_b, ln_g, ln_b, ffn_w_gate, ffn_w_up, ffn_w_down, w_in, conv_a_w, conv_a_b, norm_a_g, norm_a_b, w_a_out, ssm_conv_w, ssm_conv_b, ssm_dt_bias, ssm_a_log, ssm_d, ssm_norm_w, w_b_out, cmp_pe_k, cmp_pe_v, cmp_k_w1, cmp_k_w2, cmp_v_w1, cmp_v_w2, w_c_out, w_o):
    depth = ada_w.shape[0]
    alpha = (2.0 * depth) ** 0.25
    mod = _ada_mod(c, ada_w, ada_b)
    wg, wu, wd = ffn_w_gate.astype(BF16), ffn_w_up.astype(BF16), ffn_w_down.astype(BF16)
    h = x
    for l in range(depth):
        h = _ffn_block(h, mod[l, :, 0:3], wg, wu, wd, l, 0, ln_g[l, 0], ln_b[l, 0], alpha)
        h = _token_mixer(h, mod[l, :, 3:6], w_in[l], conv_a_w[l], conv_a_b[l], norm_a_g[l], norm_a_b[l], w_a_out[l],
                         ssm_conv_w[l], ssm_conv_b[l], ssm_dt_bias[l], ssm_a_log[l], ssm_d[l], ssm_norm_w[l],
                         w_b_out[l], cmp_pe_k[l], cmp_pe_v[l], cmp_k_w1[l], cmp_k_w2[l], cmp_v_w1[l], cmp_v_w2[l],
                         w_c_out[l], w_o[l], ln_g[l, 1], ln_b[l, 1], alpha)
        h = _ffn_block(h, mod[l, :, 6:9], wg, wu, wd, l, 1, ln_g[l, 2], ln_b[l, 2], alpha)
    return h
```

```python
import functools
import math

import numpy as np
import jax
import jax.numpy as jnp
from jax import lax
from jax.experimental import pallas as pl
from jax.experimental.pallas import tpu as pltpu

F32 = jnp.float32
BF16 = jnp.bfloat16

LN_EPS = 1e-5
NEG_INF = -1e30
MACARON_WEIGHT = 0.5

SSM_HEAD_DIM = 64
SSM_GROUPS = 4
SSM_STATE = 128
SSM_CHUNK = 256

NSA_HEAD_DIM = 64
NSA_KV_HEADS = 4
CMP_LEN = 32
CMP_STRIDE = 16
SEL_BLOCK = 64
N_SELECT = 16
WINDOW = 512
FORCED_SCORE = 1e4
ROPE_THETA = 10000.0

LANES = 128
SUBLANES = 8
SEL_CHUNK = 1024
LAZY_MAX_SLACK = 60.0
RANK_UNROLL = 4
Q_TILE = 256
IMP_PAD = SUBLANES
CONV_HALO = 32
CONV_ACC_ROWS = 64
CONV_ACC_LANES = 256
INPROJ_MAX_TN = 2560


def _dot(a, b, precision=None):
    return jnp.dot(a, b, preferred_element_type=F32, precision=precision)


def _dot_nt(a, b, precision=None):
    return lax.dot_general(a, b, (((1,), (1,)), ((), ())), preferred_element_type=F32, precision=precision)


def _bf16_terms(x):
    hi = x.astype(BF16)
    r1 = x - hi.astype(F32)
    mid = r1.astype(BF16)
    lo = (r1 - mid.astype(F32)).astype(BF16)
    return hi, mid, lo


def _dot_exact_lhs(x, sel):
    hi, mid, lo = _bf16_terms(x)
    return _dot(hi, sel) + _dot(mid, sel) + _dot(lo, sel)


def _dot_exact_rhs(sel, x):
    hi, mid, lo = _bf16_terms(x)
    return _dot(sel, hi) + _dot(sel, mid) + _dot(sel, lo)


def _layer_norm(z, g, b):
    mu = jnp.mean(z, axis=-1, keepdims=True)
    zc = z - mu
    var = jnp.mean(zc * zc, axis=-1, keepdims=True)
    return zc * lax.rsqrt(var + LN_EPS) * g + b


def _silu(x):
    return x * jax.nn.sigmoid(x)


def _ada_kernel(c_ref, w_ref, b_ref, o_ref):
    a = _silu(c_ref[...]).astype(BF16)
    o_ref[0] = _dot(a, w_ref[0].astype(BF16)) + b_ref[0]


def _ada_mod(c, ada_w, ada_b):
    n_layer, d, n = ada_w.shape
    bsz = c.shape[0]
    rows = SUBLANES * pl.cdiv(bsz, SUBLANES)
    cp = jnp.zeros((rows, d), F32).at[:bsz].set(c)
    tn = d
    out = pl.pallas_call(
        _ada_kernel,
        grid=(n_layer, n // tn),
        in_specs=[pl.BlockSpec((rows, d), lambda l, j: (0, 0)),
                  pl.BlockSpec((1, d, tn), lambda l, j: (l, 0, j)),
                  pl.BlockSpec((1, 1, tn), lambda l, j: (l, 0, j))],
        out_specs=pl.BlockSpec((1, rows, tn), lambda l, j: (l, 0, j)),
        out_shape=jax.ShapeDtypeStruct((n_layer, rows, n), F32),
        name="ada_mod",
    )(cp, ada_w, ada_b.reshape(n_layer, 1, n))
    return out[:, :bsz].reshape(n_layer, bsz, 9, d)


def _ffn_kernel(h_ref, mod_ref, wg_ref, wu_ref, wd_ref, lng_ref, lnb_ref, o_ref, u_s, acc_s, *, alpha):
    f = pl.program_id(2)

    @pl.when(f == 0)
    def _():
        u_s[...] = (h_ref[0] * (1.0 + mod_ref[0, 1:2, :]) + mod_ref[0, 0:1, :]).astype(BF16)
        acc_s[...] = jnp.zeros_like(acc_s)

    u = u_s[...]
    g = _dot(u, wg_ref[...])
    up = _dot(u, wu_ref[...])
    a = (_silu(g) * up).astype(BF16)
    acc_s[...] += _dot(a, wd_ref[...])

    @pl.when(f == pl.num_programs(2) - 1)
    def _():
        z = alpha * h_ref[0] + (MACARON_WEIGHT * mod_ref[0, 2:3, :]) * acc_s[...]
        o_ref[0] = _layer_norm(z, lng_ref[...], lnb_ref[...])


def _ffn_block(h, mod3, wg, wu, wd, layer, half, ln_g, ln_b, alpha, tm=1024):
    bsz, seq, d = h.shape
    d_ff = wg.shape[3]
    tf = d_ff // 2 if (d_ff // 2) % LANES == 0 else d_ff
    tm = min(tm, seq)
    return pl.pallas_call(
        functools.partial(_ffn_kernel, alpha=alpha),
        grid=(bsz, seq // tm, d_ff // tf),
        in_specs=[pl.BlockSpec((1, tm, d), lambda b, i, f: (b, i, 0)),
                  pl.BlockSpec((1, 3, d), lambda b, i, f: (b, 0, 0)),
                  pl.BlockSpec((None, None, d, tf), lambda b, i, f: (layer, half, 0, f)),
                  pl.BlockSpec((None, None, d, tf), lambda b, i, f: (layer, half, 0, f)),
                  pl.BlockSpec((None, None, tf, d), lambda b, i, f: (layer, half, f, 0)),
                  pl.BlockSpec((1, d), lambda b, i, f: (0, 0)),
                  pl.BlockSpec((1, d), lambda b, i, f: (0, 0))],
        out_specs=pl.BlockSpec((1, tm, d), lambda b, i, f: (b, i, 0)),
        out_shape=jax.ShapeDtypeStruct((bsz, seq, d), F32),
        scratch_shapes=[pltpu.VMEM((tm, d), BF16), pltpu.VMEM((tm, d), F32)],
        compiler_params=pltpu.CompilerParams(dimension_semantics=("parallel", "parallel", "arbitrary")),
        name="ffn",
    )(h, mod3, wg, wu, wd, ln_g.reshape(1, d), ln_b.reshape(1, d))


def _inproj_kernel(h_ref, mod_ref, w_ref, o_ref, u_s):
    @pl.when(pl.program_id(2) == 0)
    def _():
        u_s[...] = (h_ref[0] * (1.0 + mod_ref[0, 1:2, :]) + mod_ref[0, 0:1, :]).astype(BF16)

    o_ref[0] = _dot(u_s[...], w_ref[...]).astype(BF16)


def _inproj(h, mod3, w, tm=1024):
    bsz, seq, d = h.shape
    n = w.shape[1]
    tn = max(t for t in range(LANES, INPROJ_MAX_TN + 1, LANES) if n % t == 0)
    tm = min(tm, seq)
    return pl.pallas_call(
        _inproj_kernel,
        grid=(bsz, seq // tm, n // tn),
        in_specs=[pl.BlockSpec((1, tm, d), lambda b, i, j: (b, i, 0)),
                  pl.BlockSpec((1, 3, d), lambda b, i, j: (b, 0, 0)),
                  pl.BlockSpec((d, tn), lambda b, i, j: (0, j))],
        out_specs=pl.BlockSpec((1, tm, tn), lambda b, i, j: (b, i, j)),
        out_shape=jax.ShapeDtypeStruct((bsz, seq, n), BF16),
        scratch_shapes=[pltpu.VMEM((tm, d), BF16)],
        compiler_params=pltpu.CompilerParams(dimension_semantics=("parallel", "parallel", "arbitrary")),
        name="inproj",
    )(h, mod3, w)


def _conf_kernel(cur_ref, halo_ref, w_ref, cb_ref, g_ref, b_ref, o_ref, buf_s, sh_s, y_s, *, width):
    i = pl.program_id(1)
    ts, ch = y_s.shape
    x = cur_ref[0].astype(F32)
    buf_s[CONV_HALO:, :] = x[:, :ch] * jax.nn.sigmoid(x[:, ch:])
    xh = halo_ref[0].astype(F32)
    ah = xh[:, :ch] * jax.nn.sigmoid(xh[:, ch:])
    buf_s[0:CONV_HALO, :] = jnp.where(i > 0, ah, 0.0)
    off = CONV_HALO - (width - 1)
    n_sh = sh_s.shape[1]
    for r in range(1, SUBLANES):
        sh_s[r - 1] = buf_s[r:r + n_sh, :]
    rc, lc = CONV_ACC_ROWS, CONV_ACC_LANES
    for r0 in range(0, ts, rc):
        for l0 in range(0, ch, lc):
            acc = jnp.zeros((rc // SUBLANES, SUBLANES, lc), F32) + cb_ref[:, l0:l0 + lc]
            for k in range(width):
                a, r = divmod(k + off, SUBLANES)
                rows = slice(r0 + SUBLANES * a, r0 + SUBLANES * a + rc)
                tap = buf_s[rows, l0:l0 + lc] if r == 0 else sh_s[r - 1, rows, l0:l0 + lc]
                acc = acc + tap.reshape(rc // SUBLANES, SUBLANES, lc) * w_ref[k, :, l0:l0 + lc]
            y_s[r0:r0 + rc, l0:l0 + lc] = acc.reshape(rc, lc)
    yn = _layer_norm(y_s[...], g_ref[...], b_ref[...])
    o_ref[0] = _silu(yn).astype(BF16)


def _conformer(proj, conv_w, conv_b, norm_g, norm_b, ts=256):
    bsz, seq, _ = proj.shape
    width, ch = conv_w.shape
    assert width - 1 <= CONV_HALO
    ts = min(ts, seq)
    wpad = jnp.zeros((CONV_HALO, ch), F32).at[:width].set(conv_w)
    wpad = jnp.broadcast_to(wpad[:, None, :], (CONV_HALO, SUBLANES, ch))
    hb = ts // CONV_HALO
    return pl.pallas_call(
        functools.partial(_conf_kernel, width=width),
        grid=(bsz, seq // ts),
        in_specs=[pl.BlockSpec((1, ts, 2 * ch), lambda b, i: (b, i, 0)),
                  pl.BlockSpec((1, CONV_HALO, 2 * ch), lambda b, i: (b, jnp.maximum(i * hb - 1, 0), 0)),
                  pl.BlockSpec((CONV_HALO, SUBLANES, ch), lambda b, i: (0, 0, 0)),
                  pl.BlockSpec((1, ch), lambda b, i: (0, 0)),
                  pl.BlockSpec((1, ch), lambda b, i: (0, 0)),
                  pl.BlockSpec((1, ch), lambda b, i: (0, 0))],
        out_specs=pl.BlockSpec((1, ts, ch), lambda b, i: (b, i, 0)),
        out_shape=jax.ShapeDtypeStruct((bsz, seq, ch), BF16),
        scratch_shapes=[pltpu.VMEM((CONV_HALO + ts, ch), F32),
                        pltpu.VMEM((SUBLANES - 1, CONV_HALO + ts - SUBLANES, ch), F32),
                        pltpu.VMEM((ts, ch), F32)],
        compiler_params=pltpu.CompilerParams(dimension_semantics=("parallel", "parallel")),
        name="conformer",
    )(proj, proj, wpad, conv_b.reshape(1, ch), norm_g.reshape(1, ch), norm_b.reshape(1, ch))


def _ssd_kernel(z_ref, xs_ref, bc_ref, sm_ref, cw_ref, cbias_ref, dtb_ref, alog_ref, dfull_ref, nw_ref, ex_ref,
                o_ref, buf_s, st_s, y_s, *, conv_width):
    c = pl.program_id(1)
    lc, di = y_s.shape
    n_grp, n_state, gw = st_s.shape
    hpg = gw // SSM_HEAD_DIM
    carry = SUBLANES

    @pl.when(c == 0)
    def _():
        buf_s[0:carry, :] = jnp.zeros((carry, buf_s.shape[1]), F32)
        st_s[...] = jnp.zeros_like(st_s)

    @pl.when(c > 0)
    def _():
        buf_s[0:carry, :] = buf_s[lc:lc + carry, :]

    buf_s[carry:, 0:di] = xs_ref[0].astype(F32)
    buf_s[carry:, di:] = bc_ref[0].astype(F32)
    acc = jnp.zeros((lc, buf_s.shape[1]), F32) + cbias_ref[...]
    xb = buf_s[...]
    for k in range(conv_width):
        delay = conv_width - 1 - k
        tap = xb if delay == 0 else pltpu.roll(xb, delay, 0)
        acc = acc + tap[carry:, :] * cw_ref[k:k + 1, :]
    xbc = _silu(acc)
    xs = xbc[:, 0:di]
    gn = n_grp * n_state
    bm = xbc[:, di:di + gn]
    cm = xbc[:, di + gn:di + 2 * gn]

    x_dt = sm_ref[0].astype(F32) + dtb_ref[...]
    dt = jnp.maximum(x_dt, 0.0) + jnp.log(1.0 + jnp.exp(-jnp.abs(x_dt)))
    adt = dt * (-jnp.exp(alog_ref[...]))
    row = lax.broadcasted_iota(jnp.int32, (lc, lc), 0)
    col = lax.broadcasted_iota(jnp.int32, (lc, lc), 1)
    causal = row >= col
    acs = _dot_exact_rhs(causal.astype(BF16), adt)
    acs_t = acs.T
    ex = ex_ref[...]
    acs_full = _dot_exact_lhs(acs, ex)
    dt_full = _dot_exact_lhs(dt, ex)
    last_full = acs_full[lc - 1:lc, :]
    xdt = xs * dt_full
    e_acs = jnp.exp(acs_full)
    x_in = (xdt * jnp.exp(last_full - acs_full)).astype(BF16)
    chunk_decay = jnp.exp(last_full)
    lane_head = lax.broadcasted_iota(jnp.int32, (lc, gw), 1) // SSM_HEAD_DIM

    for g in range(n_grp):
        bg = bm[:, g * n_state:(g + 1) * n_state]
        cg = cm[:, g * n_state:(g + 1) * n_state].astype(BF16)
        cb = _dot_nt(cg, bg.astype(BF16))
        hg = st_s[g]
        y_g = _dot(cg, hg.astype(BF16)) * e_acs[:, g * gw:(g + 1) * gw]
        xg = xdt[:, g * gw:(g + 1) * gw]
        for r in range(hpg):
            hd = g * hpg + r
            seg = jnp.exp(jnp.where(causal, acs[:, hd:hd + 1] - acs_t[hd:hd + 1, :], NEG_INF))
            xr = jnp.where(lane_head == r, xg, 0.0).astype(BF16)
            y_g = y_g + _dot((cb * seg).astype(BF16), xr)
        st_s[g] = hg * chunk_decay[:, g * gw:(g + 1) * gw] + _dot(bg.T.astype(BF16), x_in[:, g * gw:(g + 1) * gw])
        y_s[:, g * gw:(g + 1) * gw] = y_g

    y = y_s[...] + dfull_ref[...] * xs
    y = y * _silu(z_ref[0].astype(F32))
    y = y * lax.rsqrt(jnp.mean(y * y, axis=-1, keepdims=True) + LN_EPS) * nw_ref[...]
    o_ref[0] = y.astype(BF16)


def _ssd(proj, col_z, col_xs, col_bc, col_small, conv_w, conv_b, dt_bias, a_log, d_skip, norm_w):
    bsz, seq, _ = proj.shape
    n_heads = dt_bias.shape[0]
    di = n_heads * SSM_HEAD_DIM
    conv_width, conv_dim = conv_w.shape
    gn = SSM_GROUPS * SSM_STATE
    assert conv_dim == di + 2 * gn and 2 * gn == di
    lc = math.gcd(SSM_CHUNK, seq)
    gw = di // SSM_GROUPS
    assert conv_width - 1 <= SUBLANES
    cw = jnp.zeros((SUBLANES, conv_dim), F32).at[:conv_width].set(conv_w)
    pad = lambda v: jnp.zeros((1, LANES), F32).at[0, :n_heads].set(v)
    expand = np.zeros((LANES, di), np.float32)
    expand[np.arange(di) // SSM_HEAD_DIM, np.arange(di)] = 1.0
    d_full = jnp.repeat(d_skip, SSM_HEAD_DIM).reshape(1, di)
    cblk = lambda col, w: pl.BlockSpec((1, lc, w), lambda b, c: (b, c, col // w))
    const = lambda shape: pl.BlockSpec(shape, lambda b, c: (0,) * len(shape))
    return pl.pallas_call(
        functools.partial(_ssd_kernel, conv_width=conv_width),
        grid=(bsz, seq // lc),
        in_specs=[cblk(col_z, di), cblk(col_xs, di), cblk(col_bc, di), cblk(col_small, LANES),
                  const((SUBLANES, conv_dim)), const((1, conv_dim)), const((1, LANES)), const((1, LANES)),
                  const((1, di)), const((1, di)), const((LANES, di))],
        out_specs=pl.BlockSpec((1, lc, di), lambda b, c: (b, c, 0)),
        out_shape=jax.ShapeDtypeStruct((bsz, seq, di), BF16),
        scratch_shapes=[pltpu.VMEM((SUBLANES + lc, conv_dim), F32),
                        pltpu.VMEM((SSM_GROUPS, SSM_STATE, gw), F32),
                        pltpu.VMEM((lc, di), F32)],
        compiler_params=pltpu.CompilerParams(dimension_semantics=("parallel", "arbitrary")),
        name="ssd",
    )(proj, proj, proj, proj, cw, conv_b.reshape(1, conv_dim), pad(dt_bias), pad(a_log), d_full,
      norm_w.reshape(1, di), jnp.asarray(expand, BF16))


def _rope_tables(pos):
    half = NSA_HEAD_DIM // 2
    inv_freq = ROPE_THETA ** (-np.arange(half, dtype=np.float32) / half)
    ang = jnp.asarray(pos, F32)[:, None] * jnp.asarray(inv_freq)[None, :]
    cos, sin, zero = jnp.cos(ang), jnp.sin(ang), jnp.zeros_like(ang)
    cos_t = jnp.concatenate([cos, cos, cos, cos], -1)
    sin_a = jnp.concatenate([-sin, zero, -sin, zero], -1)
    sin_b = jnp.concatenate([zero, sin, zero, sin], -1)
    return cos_t, sin_a, sin_b


def _rope(x, cos_t, sin_a, sin_b):
    half = NSA_HEAD_DIM // 2
    return x * cos_t + pltpu.roll(x, LANES - half, 1) * sin_a + pltpu.roll(x, half, 1) * sin_b


def _pair_slabs(x):
    return x, pltpu.roll(x, NSA_HEAD_DIM, 1)


def _nsa_prep_kernel(q_ref, kv_ref, cos_ref, sa_ref, sb_ref, qo_ref, ka_ref, kw_ref, vs_ref, vw_ref,
                     *, q_scale, kvd):
    i = pl.program_id(1)
    ts = q_ref.shape[1]
    hd = NSA_HEAD_DIM
    cos_t, sin_a, sin_b = cos_ref[...], sa_ref[...], sb_ref[...]
    lane = lax.broadcasted_iota(jnp.int32, (ts, LANES), 1)
    low = lane < hd
    ones_lane = lane == hd
    for j in range(q_ref.shape[2] // LANES):
        xq = _rope(q_ref[0, :, j * LANES:(j + 1) * LANES].astype(F32), cos_t, sin_a, sin_b) * q_scale
        for u, slab in enumerate(_pair_slabs(xq)):
            qo_ref[0, 2 * j + u] = jnp.where(low, slab, 0.0).astype(BF16)
    blk = (i * ts + lax.broadcasted_iota(jnp.int32, (ts, LANES), 0)) // SEL_BLOCK
    onehot = (blk == lane - hd).astype(F32)
    slab_of = lambda c0, j: kv_ref[0, :, c0 + j * LANES:c0 + (j + 1) * LANES].astype(F32)
    for j in range(kvd // LANES):
        for u, slab in enumerate(_pair_slabs(_rope(slab_of(2 * kvd, j), cos_t, sin_a, sin_b))):
            ka_ref[0, 2 * j + u] = jnp.where(low, slab, onehot).astype(BF16)
        for u, slab in enumerate(_pair_slabs(_rope(slab_of(4 * kvd, j), cos_t, sin_a, sin_b))):
            kw_ref[0, 2 * j + u] = slab.astype(BF16)
        for u, slab in enumerate(_pair_slabs(slab_of(3 * kvd, j))):
            vs_ref[0, 2 * j + u] = jnp.where(ones_lane, 1.0, slab).astype(BF16)
        for u, slab in enumerate(_pair_slabs(slab_of(5 * kvd, j))):
            vw_ref[0, 2 * j + u] = jnp.where(ones_lane, 1.0, slab).astype(BF16)


def _nsa_prep(proj, col_q, col_kv, n_q, kvd, ts=1024):
    bsz, seq, _ = proj.shape
    hd = NSA_HEAD_DIM
    ts = min(ts, seq)
    n_heads, n_kv = n_q // hd, kvd // hd
    assert 2 * hd == LANES and seq // SEL_BLOCK <= LANES - hd
    cos_t, sin_a, sin_b = _rope_tables(np.arange(seq))
    tab = pl.BlockSpec((ts, LANES), lambda b, i: (i, 0))
    per_head = lambda n: pl.BlockSpec((1, n, ts, LANES), lambda b, i: (b, 0, i, 0))
    return pl.pallas_call(
        functools.partial(_nsa_prep_kernel, q_scale=hd ** -0.5, kvd=kvd),
        grid=(bsz, seq // ts),
        in_specs=[pl.BlockSpec((1, ts, n_q), lambda b, i: (b, i, col_q // n_q)),
                  pl.BlockSpec((1, ts, 6 * kvd), lambda b, i: (b, i, col_kv // (6 * kvd))),
                  tab, tab, tab],
        out_specs=[per_head(n_heads)] + [per_head(n_kv)] * 4,
        out_shape=[jax.ShapeDtypeStruct((bsz, n_heads, seq, LANES), BF16)]
        + [jax.ShapeDtypeStruct((bsz, n_kv, seq, LANES), BF16)] * 4,
        compiler_params=pltpu.CompilerParams(dimension_semantics=("parallel", "parallel")),
        name="nsa_prep",
    )(proj, proj, cos_t, sin_a, sin_b)


def _gelu_tanh(x):
    return 0.5 * x * (1.0 + jnp.tanh(math.sqrt(2.0 / math.pi) * (x + 0.044715 * (x * x * x))))


def _compress_kernel(x_ref, w1k_ref, w1v_ref, w2k_ref, w2v_ref, pek_ref, pev_ref,
                     cos_ref, sa_ref, sb_ref, ko_ref, vo_ref, x_s):
    seq = x_ref.shape[1]
    kvd = x_ref.shape[2] // 2
    n_rows = ko_ref.shape[2]
    n_pair = kvd // LANES
    for c in range(2 * n_pair):
        x_s[c, 0:seq, :] = x_ref[0, :, c * LANES:(c + 1) * LANES].astype(F32)
        x_s[c, seq:, :] = jnp.zeros((x_s.shape[1] - seq, LANES), F32)
    cos_t, sin_a, sin_b = cos_ref[...], sa_ref[...], sb_ref[...]

    def pair(c, w1_ref, w2_ref, pe_ref):
        taps = [x_s[c, pl.ds(p, n_rows, stride=CMP_STRIDE), :].astype(BF16) for p in range(CMP_LEN)]
        w1 = w1_ref[...]
        hid = _dot(jnp.concatenate(taps, axis=1), w1) + _dot(pe_ref[...], w1)[0:1]
        return _dot(_gelu_tanh(hid).astype(BF16), w2_ref[...])

    for j in range(n_pair):
        k = _rope(pair(j, w1k_ref, w2k_ref, pek_ref), cos_t, sin_a, sin_b)
        for u, slab in enumerate(_pair_slabs(k)):
            ko_ref[0, 2 * j + u] = slab.astype(BF16)
        for u, slab in enumerate(_pair_slabs(pair(n_pair + j, w1v_ref, w2v_ref, pev_ref))):
            vo_ref[0, 2 * j + u] = slab.astype(BF16)


def _compress(proj, col_kv, kvd, pe_k, pe_v, k_w1, k_w2, v_w1, v_w2):
    bsz, seq, _ = proj.shape
    hd = NSA_HEAD_DIM
    n_kv = kvd // hd
    assert seq % CMP_STRIDE == 0 and 2 * hd == LANES and col_kv % (2 * kvd) == 0
    n_rows = seq // CMP_STRIDE
    hidden = k_w1.shape[1]

    def pair_w1(w1):
        w = w1.reshape(CMP_LEN, 1, hd, 1, hidden) * jnp.eye(2, dtype=F32).reshape(1, 2, 1, 2, 1)
        return w.reshape(CMP_LEN * LANES, 2 * hidden).astype(BF16)

    pair_w2 = lambda w2: jnp.kron(jnp.eye(2, dtype=F32), w2).astype(BF16)
    pe_rows = lambda pe: jnp.broadcast_to(pe.reshape(1, CMP_LEN, 1, hd), (SUBLANES, CMP_LEN, 2, hd)).reshape(
        SUBLANES, CMP_LEN * LANES).astype(BF16)
    cos_t, sin_a, sin_b = _rope_tables(np.arange(n_rows) * CMP_STRIDE + CMP_LEN - 1)
    const = lambda shape: pl.BlockSpec(shape, lambda b: (0,) * len(shape))
    ospec = pl.BlockSpec((1, n_kv, n_rows, LANES), lambda b: (b, 0, 0, 0))
    return pl.pallas_call(
        _compress_kernel,
        grid=(bsz,),
        in_specs=[pl.BlockSpec((1, seq, 2 * kvd), lambda b: (b, 0, col_kv // (2 * kvd))),
                  const((CMP_LEN * LANES, 2 * hidden)), const((CMP_LEN * LANES, 2 * hidden)),
                  const((2 * hidden, LANES)), const((2 * hidden, LANES)),
                  const((SUBLANES, CMP_LEN * LANES)), const((SUBLANES, CMP_LEN * LANES)),
                  const((n_rows, LANES)), const((n_rows, LANES)), const((n_rows, LANES))],
        out_specs=[ospec, ospec],
        out_shape=[jax.ShapeDtypeStruct((bsz, n_kv, n_rows, LANES), BF16)] * 2,
        scratch_shapes=[pltpu.VMEM((2 * kvd // LANES, seq + CMP_LEN, LANES), F32)],
        compiler_params=pltpu.CompilerParams(dimension_semantics=("parallel",)),
        name="compress",
    )(proj, pair_w1(k_w1), pair_w1(v_w1), pair_w2(k_w2), pair_w2(v_w2), pe_rows(pe_k), pe_rows(pe_v),
      cos_t, sin_a, sin_b)


def _dot_tn(a, b):
    return lax.dot_general(a, b, (((0,), (0,)), ((), ())), preferred_element_type=F32)


def _nsa_kernel(q_ref, kc_ref, vc_ref, ka_ref, vs_ref, kw_ref, vw_ref, gl_ref, crel_ref, wrel_ref, o_ref,
                psum_s, imp_s):
    i = pl.program_id(2)
    gqa, qt = q_ref.shape[1:3]
    hd = NSA_HEAD_DIM
    cols = gqa * qt
    n_blk = imp_s.shape[0]
    n_rows = kc_ref.shape[2]
    band = WINDOW + qt
    qp = q_ref[0].reshape(cols, LANES)
    lane_q = lax.broadcasted_iota(jnp.int32, (1, cols), 1) % qt
    t0 = i * qt
    q0 = pl.multiple_of(t0, qt)
    w0 = pl.multiple_of(jnp.maximum(t0 - WINDOW, 0), qt)

    k_all = jnp.concatenate([kc_ref[0, 0], kw_ref[0, 0, pl.ds(w0, band), :]], axis=0)
    s_all = _dot_nt(k_all, qp)
    s_c, s_w = s_all[:n_rows], s_all[n_rows:]

    cmask = crel_ref[...] <= t0
    s_c = jnp.where(cmask, s_c, NEG_INF)
    e_c = jnp.exp(s_c - jnp.max(s_c, axis=0, keepdims=True))
    has_cmp = t0 + lane_q >= CMP_LEN - 1
    p_cmp = e_c * jnp.where(has_cmp, 1.0 / jnp.sum(e_c, axis=0, keepdims=True), 0.0)
    o_cmp = _dot_tn(vc_ref[0, 0], p_cmp.astype(BF16))[:hd]

    w_rel = wrel_ref[...]
    wmask = (w_rel <= t0 - w0) & (w_rel > t0 - w0 - WINDOW)
    s_w = jnp.where(wmask, s_w, NEG_INF)
    e_w = jnp.exp((s_w - jnp.max(s_w, axis=0, keepdims=True)).astype(BF16))
    pv_w = _dot_tn(vw_ref[0, 0, pl.ds(w0, band), :], e_w)
    o_win = pv_w[:hd] * (1.0 / pv_w[hd:hd + 1])

    p_sum = p_cmp[:, 0:qt]
    for r in range(1, gqa):
        p_sum = p_sum + p_cmp[:, r * qt:(r + 1) * qt]
    ratio = SEL_BLOCK // CMP_STRIDE
    imp_parts = []
    for c in range(qt // LANES):
        psum_s[c, 0:IMP_PAD, :] = jnp.zeros((IMP_PAD, LANES), F32)
        psum_s[c, IMP_PAD:, :] = p_sum[:, c * LANES:(c + 1) * LANES]
        part = psum_s[c, pl.ds(IMP_PAD - 1, n_blk, stride=ratio), :]
        for k in range(ratio):
            part = part + psum_s[c, pl.ds(IMP_PAD + k, n_blk, stride=ratio), :]
        imp_parts.append(part)
    imp = jnp.concatenate(imp_parts, axis=1)
    blk = lax.broadcasted_iota(jnp.int32, (n_blk, qt), 0)
    tq = t0 + lax.broadcasted_iota(jnp.int32, (n_blk, qt), 1)
    cur = tq // SEL_BLOCK
    forced = (blk == 0) | (blk == cur) | (blk == cur - 1)
    imp = jnp.where(forced, FORCED_SCORE, jnp.where(blk * SEL_BLOCK > tq, -1.0, imp))
    imp_s[...] = imp

    n_live = (t0 + qt - 1) // SEL_BLOCK + 1

    def rank_body(jj, cnt):
        for u in range(RANK_UNROLL):
            j = jj * RANK_UNROLL + u
            other = imp_s[pl.ds(j, 1), :]
            cnt = cnt + jnp.where(j < blk, (other >= imp).astype(F32), (other > imp).astype(F32))
        return cnt

    trips = jnp.where(n_live > N_SELECT, n_live // RANK_UNROLL, 0)
    rank = lax.fori_loop(0, trips, rank_body, jnp.zeros((n_blk, qt), F32))

    def with_bias(bias):
        pad_rows = [jnp.zeros((LANES - hd - n_blk, qt), F32)] if LANES - hd > n_blk else []
        bias_t = jnp.concatenate([jnp.zeros((hd, qt), F32), bias] + pad_rows, axis=0).T
        return (qp.astype(F32) + jnp.concatenate([bias_t] * gqa, axis=0)).astype(BF16)

    selected = rank < N_SELECT
    qa = with_bias(jnp.where(selected & (blk * SEL_BLOCK < t0), 0.0, NEG_INF))
    qa_own = with_bias(jnp.where(selected, 0.0, NEG_INF))
    s_d = _dot_nt(ka_ref[0, 0, pl.ds(q0, qt), :], qa_own)
    s_d = jnp.where(lax.broadcasted_iota(jnp.int32, (qt, cols), 0) <= lane_q, s_d, NEG_INF)
    m0 = jnp.max(s_d, axis=0, keepdims=True)
    pv_d = _dot_tn(vs_ref[0, 0, pl.ds(q0, qt), :], jnp.exp((s_d - m0).astype(BF16)))
    l0, acc0 = pv_d[hd:hd + 1], pv_d[:hd]

    def sel_body(c, carry):
        m, l, acc = carry
        k0 = pl.multiple_of(c * SEL_CHUNK, SEL_CHUNK)
        kk = ka_ref[0, 0, pl.ds(k0, SEL_CHUNK), :]
        vv = vs_ref[0, 0, pl.ds(k0, SEL_CHUNK), :]
        s = _dot_nt(kk, qa)
        pv = _dot_tn(vv, jnp.exp((s - m).astype(BF16)))
        m_new = jnp.maximum(m, jnp.max(s, axis=0, keepdims=True))
        alpha = jnp.exp(m - m_new)
        l_lazy = alpha * (l + pv[hd:hd + 1])
        acc_lazy = alpha * (acc + pv[:hd])
        safe = jnp.max(m_new - m) <= LAZY_MAX_SLACK

        def exact():
            pv2 = _dot_tn(vv, jnp.exp((_dot_nt(kk, qa) - m_new).astype(BF16)))
            return alpha * l + pv2[hd:hd + 1], alpha * acc + pv2[:hd]

        l, acc = lax.cond(safe, lambda: (l_lazy, acc_lazy), exact)
        return m_new, l, acc

    n_chunks = (t0 + SEL_CHUNK - 1) // SEL_CHUNK
    _, l_sel, acc_sel = lax.fori_loop(0, n_chunks, sel_body, (m0, l0, acc0))
    o_sel = acc_sel / l_sel

    gate = jax.nn.sigmoid(gl_ref[0, 0, 0].astype(F32))
    out = gate[0:1] * o_cmp + gate[1:2] * o_sel + gate[2:3] * o_win
    pairs = [jnp.concatenate([out[:, r * qt:(r + 1) * qt], out[:, (r + 1) * qt:(r + 2) * qt]], axis=0).T
             for r in range(0, gqa, 2)]
    o_ref[0] = jnp.concatenate(pairs, axis=1).astype(BF16)


def _nsa_attention(q_hm, kc, vc, ka, vs, kw, vw, gl):
    bsz, n_heads, seq, _ = q_hm.shape
    hd = NSA_HEAD_DIM
    n_kv = ka.shape[1]
    gqa = n_heads // n_kv
    n_rows = kc.shape[2]
    n_cmp = (seq - CMP_LEN) // CMP_STRIDE + 1
    nq = seq // Q_TILE
    n_blk = seq // SEL_BLOCK
    assert N_SELECT <= n_blk <= LANES - hd and Q_TILE % (RANK_UNROLL * SEL_BLOCK) == 0 and Q_TILE % LANES == 0
    assert LANES == 2 * hd and gqa % 2 == 0
    assert seq % SEL_CHUNK == 0 and seq >= WINDOW + Q_TILE
    assert n_rows * CMP_STRIDE == seq and n_blk * (SEL_BLOCK // CMP_STRIDE) <= n_rows and CMP_LEN == 2 * CMP_STRIDE
    cols = gqa * Q_TILE
    gl2 = gl.reshape(bsz, nq, Q_TILE, n_kv, gqa, 3).transpose(0, 3, 1, 5, 4, 2).reshape(bsz, n_kv, nq, 3, cols)
    lane_q = np.arange(cols)[None, :] % Q_TILE
    n_idx = np.arange(n_rows)[:, None]
    crel = np.where(n_idx < n_cmp, n_idx * CMP_STRIDE + (CMP_LEN - 1) - lane_q, np.iinfo(np.int32).max).astype(np.int32)
    wrel = (np.arange(WINDOW + Q_TILE)[:, None] - lane_q).astype(np.int32)
    const = lambda a: pl.BlockSpec(a.shape, lambda b, g, i: (0, 0))
    kv_spec = lambda n: pl.BlockSpec((1, 1, n, LANES), lambda b, g, i: (b, g, 0, 0))
    return pl.pallas_call(
        _nsa_kernel,
        grid=(bsz, n_kv, nq),
        in_specs=[pl.BlockSpec((1, gqa, Q_TILE, LANES), lambda b, g, i: (b, g, i, 0)),
                  kv_spec(n_rows), kv_spec(n_rows), kv_spec(seq), kv_spec(seq), kv_spec(seq), kv_spec(seq),
                  pl.BlockSpec((1, 1, 1, 3, cols), lambda b, g, i: (b, g, i, 0, 0)),
                  const(crel), const(wrel)],
        out_specs=pl.BlockSpec((1, Q_TILE, gqa * hd), lambda b, g, i: (b, i, g)),
        out_shape=jax.ShapeDtypeStruct((bsz, seq, n_heads * hd), BF16),
        scratch_shapes=[pltpu.VMEM((Q_TILE // LANES, IMP_PAD + n_rows, LANES), F32),
                        pltpu.VMEM((n_blk, Q_TILE), F32)],
        compiler_params=pltpu.CompilerParams(dimension_semantics=("parallel", "parallel", "arbitrary")),
        name="nsa_attention",
    )(q_hm, kc, vc, ka, vs, kw, vw, gl2, jnp.asarray(crel), jnp.asarray(wrel))


def _merge_kernel(h_ref, mod_ref, ya_ref, yb_ref, yc_ref, gl_ref, wa_ref, wb_ref, wc_ref, wo_ref,
                  lng_ref, lnb_ref, o_ref, *, alpha):
    d = h_ref.shape[2]
    gate = lambda k: jax.nn.sigmoid(gl_ref[0, :, k * d:(k + 1) * d].astype(F32))
    merged = gate(0) * _dot(ya_ref[0], wa_ref[...])
    merged = merged + gate(1) * _dot(yb_ref[0], wb_ref[...])
    merged = merged + gate(2) * _dot(yc_ref[0], wc_ref[...])
    y = _dot(merged.astype(BF16), wo_ref[...])
    z = alpha * h_ref[0] + mod_ref[0, 2:3, :] * y
    o_ref[0] = _layer_norm(z, lng_ref[...], lnb_ref[...])


def _merge(h, mod3, ya, yb, yc, proj, col_gate, wa, wb, wc, wo, ln_g, ln_b, alpha, tm=512):
    bsz, seq, d = h.shape
    tm = min(tm, seq)
    act = pl.BlockSpec((1, tm, d), lambda b, i: (b, i, 0))
    wspec = pl.BlockSpec((d, d), lambda b, i: (0, 0))
    vec = pl.BlockSpec((1, d), lambda b, i: (0, 0))
    return pl.pallas_call(
        functools.partial(_merge_kernel, alpha=alpha),
        grid=(bsz, seq // tm),
        in_specs=[act, pl.BlockSpec((1, 3, d), lambda b, i: (b, 0, 0)), act, act, act,
                  pl.BlockSpec((1, tm, 3 * d), lambda b, i: (b, i, col_gate // (3 * d))),
                  wspec, wspec, wspec, wspec, vec, vec],
        out_specs=act,
        out_shape=jax.ShapeDtypeStruct((bsz, seq, d), F32),
        compiler_params=pltpu.CompilerParams(dimension_semantics=("parallel", "parallel")),
        name="merge",
    )(h, mod3, ya, yb, yc, proj, wa, wb, wc, wo, ln_g.reshape(1, d), ln_b.reshape(1, d))


def _token_mixer(h, mod3, w_in, conv_a_w, conv_a_b, norm_a_g, norm_a_b, w_a_out,
                 ssm_conv_w, ssm_conv_b, ssm_dt_bias, ssm_a_log, ssm_d, ssm_norm_w, w_b_out,
                 cmp_pe_k, cmp_pe_v, cmp_k_w1, cmp_k_w2, cmp_v_w1, cmp_v_w2, w_c_out, w_o, ln_g, ln_b, alpha):
    bsz, seq, d = h.shape
    ch = conv_a_w.shape[1]
    n_ssm_heads = ssm_dt_bias.shape[0]
    di = n_ssm_heads * SSM_HEAD_DIM
    gn = SSM_GROUPS * SSM_STATE
    n_q = w_c_out.shape[0]
    n_heads = n_q // NSA_HEAD_DIM
    kvd = NSA_KV_HEADS * NSA_HEAD_DIM
    gqa = n_heads // NSA_KV_HEADS
    n_gate = 3 * n_heads
    o_dt = 2 * ch + 2 * di + 2 * gn
    o_q = o_dt + n_ssm_heads
    o_gl = o_q + n_q + 6 * kvd
    o_g = o_gl + n_gate
    assert w_in.shape[1] == o_g + 3 * d
    col_z, col_xs, col_bc = 2 * ch, 2 * ch + di, 2 * ch + 2 * di
    col_q = o_dt
    col_gate = col_q + n_q
    col_kv = col_gate + 3 * d
    col_small = col_kv + 6 * kvd
    assert n_ssm_heads + n_gate <= LANES
    w_perm = jnp.concatenate(
        [w_in[:, :o_dt], w_in[:, o_q:o_q + n_q], w_in[:, o_g:], w_in[:, o_q + n_q:o_gl], w_in[:, o_dt:o_q],
         w_in[:, o_gl:o_g], jnp.zeros((d, LANES - n_ssm_heads - n_gate), F32)], axis=1).astype(BF16)

    proj = _inproj(h, mod3, w_perm)

    y_a = _conformer(proj, conv_a_w, conv_a_b, norm_a_g, norm_a_b)
    y_b = _ssd(proj, col_z, col_xs, col_bc, col_small, ssm_conv_w, ssm_conv_b, ssm_dt_bias, ssm_a_log,
               ssm_d, ssm_norm_w)

    q_hm, ka, kw, vs, vw = _nsa_prep(proj, col_q, col_kv, n_q, kvd)
    k_cmp, v_cmp = _compress(proj, col_kv, kvd, cmp_pe_k, cmp_pe_v, cmp_k_w1, cmp_k_w2, cmp_v_w1, cmp_v_w2)
    gl = proj[:, :, col_small + n_ssm_heads:col_small + n_ssm_heads + n_gate]
    y_c = _nsa_attention(q_hm, k_cmp, v_cmp, ka, vs, kw, vw, gl)

    return _merge(h, mod3, y_a, y_b, y_c, proj, col_gate, w_a_out.astype(BF16), w_b_out.astype(BF16),
                  w_c_out.astype(BF16), w_o.astype(BF16), ln_g, ln_b, alpha)


def kernel(x, c, ada_w, ada_b, ln_g, ln_b, ffn_w_gate, ffn_w_up, ffn_w_down, w_in, conv_a_w, conv_a_b, norm_a_g, norm_a_b, w_a_out, ssm_conv_w, ssm_conv_b, ssm_dt_bias, ssm_a_log, ssm_d, ssm_norm_w, w_b_out, cmp_pe_k, cmp_pe_v, cmp_k_w1, cmp_k_w2, cmp_v_w1, cmp_v_w2, w_c_out, w_o):
    depth = ada_w.shape[0]
    alpha = (2.0 * depth) ** 0.25
    mod = _ada_mod(c, ada_w, ada_b)
    wg, wu, wd = ffn_w_gate.astype(BF16), ffn_w_up.astype(BF16), ffn_w_down.astype(BF16)
    h = x
    for l in range(depth):
        h = _ffn_block(h, mod[l, :, 0:3], wg, wu, wd, l, 0, ln_g[l, 0], ln_b[l, 0], alpha)
        h = _token_mixer(h, mod[l, :, 3:6], w_in[l], conv_a_w[l], conv_a_b[l], norm_a_g[l], norm_a_b[l], w_a_out[l],
                         ssm_conv_w[l], ssm_conv_b[l], ssm_dt_bias[l], ssm_a_log[l], ssm_d[l], ssm_norm_w[l],
                         w_b_out[l], cmp_pe_k[l], cmp_pe_v[l], cmp_k_w1[l], cmp_k_w2[l], cmp_v_w1[l], cmp_v_w2[l],
                         w_c_out[l], w_o[l], ln_g[l, 1], ln_b[l, 1], alpha)
        h = _ffn_block(h, mod[l, :, 6:9], wg, wu, wd, l, 1, ln_g[l, 2], ln_b[l, 2], alpha)
    return h
```

```python
import functools
import math

import numpy as np
import jax
import jax.numpy as jnp
from jax import lax
from jax.experimental import pallas as pl
from jax.experimental.pallas import tpu as pltpu

F32 = jnp.float32
BF16 = jnp.bfloat16

LN_EPS = 1e-5
NEG_INF = -1e30
MACARON_WEIGHT = 0.5

SSM_HEAD_DIM = 64
SSM_GROUPS = 4
SSM_STATE = 128
SSM_CHUNK = 256

NSA_HEAD_DIM = 64
NSA_KV_HEADS = 4
CMP_LEN = 32
CMP_STRIDE = 16
SEL_BLOCK = 64
N_SELECT = 16
WINDOW = 512
FORCED_SCORE = 1e4
ROPE_THETA = 10000.0

LANES = 128
SUBLANES = 8
SEL_CHUNK = 1024
LAZY_MAX_SLACK = 60.0
RANK_UNROLL = 4
Q_TILE = 256
IMP_PAD = SUBLANES
CONV_HALO = 32
CONV_ACC_ROWS = 64
CONV_ACC_LANES = 256
INPROJ_MAX_TN = 2560


def _dot(a, b, precision=None):
    return jnp.dot(a, b, preferred_element_type=F32, precision=precision)


def _dot_nt(a, b, precision=None):
    return lax.dot_general(a, b, (((1,), (1,)), ((), ())), preferred_element_type=F32, precision=precision)


def _bf16_terms(x):
    hi = x.astype(BF16)
    r1 = x - hi.astype(F32)
    mid = r1.astype(BF16)
    lo = (r1 - mid.astype(F32)).astype(BF16)
    return hi, mid, lo


def _dot_exact_lhs(x, sel):
    hi, mid, lo = _bf16_terms(x)
    return _dot(hi, sel) + _dot(mid, sel) + _dot(lo, sel)


def _dot_exact_rhs(sel, x):
    hi, mid, lo = _bf16_terms(x)
    return _dot(sel, hi) + _dot(sel, mid) + _dot(sel, lo)


def _layer_norm(z, g, b):
    mu = jnp.mean(z, axis=-1, keepdims=True)
    zc = z - mu
    var = jnp.mean(zc * zc, axis=-1, keepdims=True)
    return zc * lax.rsqrt(var + LN_EPS) * g + b


def _silu(x):
    return x * jax.nn.sigmoid(x)


def _ada_kernel(c_ref, w_ref, b_ref, o_ref):
    a = _silu(c_ref[...]).astype(BF16)
    o_ref[0] = _dot(a, w_ref[0].astype(BF16)) + b_ref[0]


def _ada_mod(c, ada_w, ada_b):
    n_layer, d, n = ada_w.shape
    bsz = c.shape[0]
    rows = SUBLANES * pl.cdiv(bsz, SUBLANES)
    cp = jnp.zeros((rows, d), F32).at[:bsz].set(c)
    tn = 3 * d if n % (3 * d) == 0 else d
    out = pl.pallas_call(
        _ada_kernel,
        grid=(n_layer, n // tn),
        in_specs=[pl.BlockSpec((rows, d), lambda l, j: (0, 0)),
                  pl.BlockSpec((1, d, tn), lambda l, j: (l, 0, j)),
                  pl.BlockSpec((1, 1, tn), lambda l, j: (l, 0, j))],
        out_specs=pl.BlockSpec((1, rows, tn), lambda l, j: (l, 0, j)),
        out_shape=jax.ShapeDtypeStruct((n_layer, rows, n), F32),
        name="ada_mod",
    )(cp, ada_w, ada_b.reshape(n_layer, 1, n))
    return out[:, :bsz].reshape(n_layer, bsz, 9, d)


def _ffn_kernel(h_ref, mod_ref, wg_ref, wu_ref, wd_ref, lng_ref, lnb_ref, o_ref, u_s, acc_s, *, alpha):
    f = pl.program_id(2)

    @pl.when(f == 0)
    def _():
        u_s[...] = (h_ref[0] * (1.0 + mod_ref[0, 1:2, :]) + mod_ref[0, 0:1, :]).astype(BF16)
        acc_s[...] = jnp.zeros_like(acc_s)

    u = u_s[...]
    g = _dot(u, wg_ref[...])
    up = _dot(u, wu_ref[...])
    a = (_silu(g) * up).astype(BF16)
    acc_s[...] += _dot(a, wd_ref[...])

    @pl.when(f == pl.num_programs(2) - 1)
    def _():
        z = alpha * h_ref[0] + (MACARON_WEIGHT * mod_ref[0, 2:3, :]) * acc_s[...]
        o_ref[0] = _layer_norm(z, lng_ref[...], lnb_ref[...])


def _ffn_block(h, mod3, wg, wu, wd, layer, half, ln_g, ln_b, alpha, tm=1024):
    bsz, seq, d = h.shape
    d_ff = wg.shape[3]
    tf = d_ff // 2 if (d_ff // 2) % LANES == 0 else d_ff
    tm = min(tm, seq)
    return pl.pallas_call(
        functools.partial(_ffn_kernel, alpha=alpha),
        grid=(bsz, seq // tm, d_ff // tf),
        in_specs=[pl.BlockSpec((1, tm, d), lambda b, i, f: (b, i, 0)),
                  pl.BlockSpec((1, 3, d), lambda b, i, f: (b, 0, 0)),
                  pl.BlockSpec((None, None, d, tf), lambda b, i, f: (layer, half, 0, f)),
                  pl.BlockSpec((None, None, d, tf), lambda b, i, f: (layer, half, 0, f)),
                  pl.BlockSpec((None, None, tf, d), lambda b, i, f: (layer, half, f, 0)),
                  pl.BlockSpec((1, d), lambda b, i, f: (0, 0)),
                  pl.BlockSpec((1, d), lambda b, i, f: (0, 0))],
        out_specs=pl.BlockSpec((1, tm, d), lambda b, i, f: (b, i, 0)),
        out_shape=jax.ShapeDtypeStruct((bsz, seq, d), F32),
        scratch_shapes=[pltpu.VMEM((tm, d), BF16), pltpu.VMEM((tm, d), F32)],
        compiler_params=pltpu.CompilerParams(dimension_semantics=("parallel", "parallel", "arbitrary")),
        name="ffn",
    )(h, mod3, wg, wu, wd, ln_g.reshape(1, d), ln_b.reshape(1, d))


def _inproj_kernel(h_ref, mod_ref, w_ref, o_ref, u_s):
    @pl.when(pl.program_id(2) == 0)
    def _():
        u_s[...] = (h_ref[0] * (1.0 + mod_ref[0, 1:2, :]) + mod_ref[0, 0:1, :]).astype(BF16)

    o_ref[0] = _dot(u_s[...], w_ref[...]).astype(BF16)


def _inproj(h, mod3, w, tm=1024):
    bsz, seq, d = h.shape
    n = w.shape[1]
    tn = max(t for t in range(LANES, INPROJ_MAX_TN + 1, LANES) if n % t == 0)
    tm = min(tm, seq)
    return pl.pallas_call(
        _inproj_kernel,
        grid=(bsz, seq // tm, n // tn),
        in_specs=[pl.BlockSpec((1, tm, d), lambda b, i, j: (b, i, 0)),
                  pl.BlockSpec((1, 3, d), lambda b, i, j: (b, 0, 0)),
                  pl.BlockSpec((d, tn), lambda b, i, j: (0, j))],
        out_specs=pl.BlockSpec((1, tm, tn), lambda b, i, j: (b, i, j)),
        out_shape=jax.ShapeDtypeStruct((bsz, seq, n), BF16),
        scratch_shapes=[pltpu.VMEM((tm, d), BF16)],
        compiler_params=pltpu.CompilerParams(dimension_semantics=("parallel", "parallel", "arbitrary")),
        name="inproj",
    )(h, mod3, w)


def _conf_kernel(cur_ref, halo_ref, w_ref, cb_ref, g_ref, b_ref, o_ref, buf_s, sh_s, y_s, *, width):
    i = pl.program_id(1)
    ts, ch = y_s.shape
    x = cur_ref[0].astype(F32)
    buf_s[CONV_HALO:, :] = x[:, :ch] * jax.nn.sigmoid(x[:, ch:])
    xh = halo_ref[0].astype(F32)
    ah = xh[:, :ch] * jax.nn.sigmoid(xh[:, ch:])
    buf_s[0:CONV_HALO, :] = jnp.where(i > 0, ah, 0.0)
    off = CONV_HALO - (width - 1)
    n_sh = sh_s.shape[1]
    for r in range(1, SUBLANES):
        sh_s[r - 1] = buf_s[r:r + n_sh, :]
    rc, lc = CONV_ACC_ROWS, CONV_ACC_LANES
    for r0 in range(0, ts, rc):
        for l0 in range(0, ch, lc):
            acc = jnp.zeros((rc // SUBLANES, SUBLANES, lc), F32) + cb_ref[:, l0:l0 + lc]
            for k in range(width):
                a, r = divmod(k + off, SUBLANES)
                rows = slice(r0 + SUBLANES * a, r0 + SUBLANES * a + rc)
                tap = buf_s[rows, l0:l0 + lc] if r == 0 else sh_s[r - 1, rows, l0:l0 + lc]
                acc = acc + tap.reshape(rc // SUBLANES, SUBLANES, lc) * w_ref[k, :, l0:l0 + lc]
            y_s[r0:r0 + rc, l0:l0 + lc] = acc.reshape(rc, lc)
    yn = _layer_norm(y_s[...], g_ref[...], b_ref[...])
    o_ref[0] = _silu(yn).astype(BF16)


def _conformer(proj, conv_w, conv_b, norm_g, norm_b, ts=256):
    bsz, seq, _ = proj.shape
    width, ch = conv_w.shape
    assert width - 1 <= CONV_HALO
    ts = min(ts, seq)
    wpad = jnp.zeros((CONV_HALO, ch), F32).at[:width].set(conv_w)
    wpad = jnp.broadcast_to(wpad[:, None, :], (CONV_HALO, SUBLANES, ch))
    hb = ts // CONV_HALO
    return pl.pallas_call(
        functools.partial(_conf_kernel, width=width),
        grid=(bsz, seq // ts),
        in_specs=[pl.BlockSpec((1, ts, 2 * ch), lambda b, i: (b, i, 0)),
                  pl.BlockSpec((1, CONV_HALO, 2 * ch), lambda b, i: (b, jnp.maximum(i * hb - 1, 0), 0)),
                  pl.BlockSpec((CONV_HALO, SUBLANES, ch), lambda b, i: (0, 0, 0)),
                  pl.BlockSpec((1, ch), lambda b, i: (0, 0)),
                  pl.BlockSpec((1, ch), lambda b, i: (0, 0)),
                  pl.BlockSpec((1, ch), lambda b, i: (0, 0))],
        out_specs=pl.BlockSpec((1, ts, ch), lambda b, i: (b, i, 0)),
        out_shape=jax.ShapeDtypeStruct((bsz, seq, ch), BF16),
        scratch_shapes=[pltpu.VMEM((CONV_HALO + ts, ch), F32),
                        pltpu.VMEM((SUBLANES - 1, CONV_HALO + ts - SUBLANES, ch), F32),
                        pltpu.VMEM((ts, ch), F32)],
        compiler_params=pltpu.CompilerParams(dimension_semantics=("parallel", "parallel")),
        name="conformer",
    )(proj, proj, wpad, conv_b.reshape(1, ch), norm_g.reshape(1, ch), norm_b.reshape(1, ch))


def _ssd_kernel(z_ref, xs_ref, bc_ref, sm_ref, cw_ref, cbias_ref, dtb_ref, alog_ref, dfull_ref, nw_ref, ex_ref,
                o_ref, buf_s, st_s, y_s, *, conv_width):
    c = pl.program_id(1)
    lc, di = y_s.shape
    n_grp, n_state, gw = st_s.shape
    hpg = gw // SSM_HEAD_DIM
    carry = SUBLANES

    @pl.when(c == 0)
    def _():
        buf_s[0:carry, :] = jnp.zeros((carry, buf_s.shape[1]), F32)
        st_s[...] = jnp.zeros_like(st_s)

    @pl.when(c > 0)
    def _():
        buf_s[0:carry, :] = buf_s[lc:lc + carry, :]

    buf_s[carry:, 0:di] = xs_ref[0].astype(F32)
    buf_s[carry:, di:] = bc_ref[0].astype(F32)
    acc = jnp.zeros((lc, buf_s.shape[1]), F32) + cbias_ref[...]
    xb = buf_s[...]
    for k in range(conv_width):
        delay = conv_width - 1 - k
        tap = xb if delay == 0 else pltpu.roll(xb, delay, 0)
        acc = acc + tap[carry:, :] * cw_ref[k:k + 1, :]
    xbc = _silu(acc)
    xs = xbc[:, 0:di]
    gn = n_grp * n_state
    bm = xbc[:, di:di + gn]
    cm = xbc[:, di + gn:di + 2 * gn]

    x_dt = sm_ref[0].astype(F32) + dtb_ref[...]
    dt = jnp.maximum(x_dt, 0.0) + jnp.log(1.0 + jnp.exp(-jnp.abs(x_dt)))
    adt = dt * (-jnp.exp(alog_ref[...]))
    row = lax.broadcasted_iota(jnp.int32, (lc, lc), 0)
    col = lax.broadcasted_iota(jnp.int32, (lc, lc), 1)
    causal = row >= col
    acs = _dot_exact_rhs(causal.astype(BF16), adt)
    acs_t = acs.T
    ex = ex_ref[...]
    acs_full = _dot_exact_lhs(acs, ex)
    dt_full = _dot_exact_lhs(dt, ex)
    last_full = acs_full[lc - 1:lc, :]
    xdt = xs * dt_full
    e_acs = jnp.exp(acs_full)
    x_in = (xdt * jnp.exp(last_full - acs_full)).astype(BF16)
    chunk_decay = jnp.exp(last_full)
    lane_head = lax.broadcasted_iota(jnp.int32, (lc, gw), 1) // SSM_HEAD_DIM

    for g in range(n_grp):
        bg = bm[:, g * n_state:(g + 1) * n_state]
        cg = cm[:, g * n_state:(g + 1) * n_state].astype(BF16)
        cb = _dot_nt(cg, bg.astype(BF16))
        hg = st_s[g]
        y_g = _dot(cg, hg.astype(BF16)) * e_acs[:, g * gw:(g + 1) * gw]
        xg = xdt[:, g * gw:(g + 1) * gw]
        for r in range(hpg):
            hd = g * hpg + r
            seg = jnp.exp(jnp.where(causal, acs[:, hd:hd + 1] - acs_t[hd:hd + 1, :], NEG_INF))
            xr = jnp.where(lane_head == r, xg, 0.0).astype(BF16)
            y_g = y_g + _dot((cb * seg).astype(BF16), xr)
        st_s[g] = hg * chunk_decay[:, g * gw:(g + 1) * gw] + _dot(bg.T.astype(BF16), x_in[:, g * gw:(g + 1) * gw])
        y_s[:, g * gw:(g + 1) * gw] = y_g

    y = y_s[...] + dfull_ref[...] * xs
    y = y * _silu(z_ref[0].astype(F32))
    y = y * lax.rsqrt(jnp.mean(y * y, axis=-1, keepdims=True) + LN_EPS) * nw_ref[...]
    o_ref[0] = y.astype(BF16)


def _ssd(proj, col_z, col_xs, col_bc, col_small, conv_w, conv_b, dt_bias, a_log, d_skip, norm_w):
    bsz, seq, _ = proj.shape
    n_heads = dt_bias.shape[0]
    di = n_heads * SSM_HEAD_DIM
    conv_width, conv_dim = conv_w.shape
    gn = SSM_GROUPS * SSM_STATE
    assert conv_dim == di + 2 * gn and 2 * gn == di
    lc = math.gcd(SSM_CHUNK, seq)
    gw = di // SSM_GROUPS
    assert conv_width - 1 <= SUBLANES
    cw = jnp.zeros((SUBLANES, conv_dim), F32).at[:conv_width].set(conv_w)
    pad = lambda v: jnp.zeros((1, LANES), F32).at[0, :n_heads].set(v)
    expand = np.zeros((LANES, di), np.float32)
    expand[np.arange(di) // SSM_HEAD_DIM, np.arange(di)] = 1.0
    d_full = jnp.repeat(d_skip, SSM_HEAD_DIM).reshape(1, di)
    cblk = lambda col, w: pl.BlockSpec((1, lc, w), lambda b, c: (b, c, col // w))
    const = lambda shape: pl.BlockSpec(shape, lambda b, c: (0,) * len(shape))
    return pl.pallas_call(
        functools.partial(_ssd_kernel, conv_width=conv_width),
        grid=(bsz, seq // lc),
        in_specs=[cblk(col_z, di), cblk(col_xs, di), cblk(col_bc, di), cblk(col_small, LANES),
                  const((SUBLANES, conv_dim)), const((1, conv_dim)), const((1, LANES)), const((1, LANES)),
                  const((1, di)), const((1, di)), const((LANES, di))],
        out_specs=pl.BlockSpec((1, lc, di), lambda b, c: (b, c, 0)),
        out_shape=jax.ShapeDtypeStruct((bsz, seq, di), BF16),
        scratch_shapes=[pltpu.VMEM((SUBLANES + lc, conv_dim), F32),
                        pltpu.VMEM((SSM_GROUPS, SSM_STATE, gw), F32),
                        pltpu.VMEM((lc, di), F32)],
        compiler_params=pltpu.CompilerParams(dimension_semantics=("parallel", "arbitrary")),
        name="ssd",
    )(proj, proj, proj, proj, cw, conv_b.reshape(1, conv_dim), pad(dt_bias), pad(a_log), d_full,
      norm_w.reshape(1, di), jnp.asarray(expand, BF16))


def _rope_tables(pos):
    half = NSA_HEAD_DIM // 2
    inv_freq = ROPE_THETA ** (-np.arange(half, dtype=np.float32) / half)
    ang = jnp.asarray(pos, F32)[:, None] * jnp.asarray(inv_freq)[None, :]
    cos, sin, zero = jnp.cos(ang), jnp.sin(ang), jnp.zeros_like(ang)
    cos_t = jnp.concatenate([cos, cos, cos, cos], -1)
    sin_a = jnp.concatenate([-sin, zero, -sin, zero], -1)
    sin_b = jnp.concatenate([zero, sin, zero, sin], -1)
    return cos_t, sin_a, sin_b


def _rope(x, cos_t, sin_a, sin_b):
    half = NSA_HEAD_DIM // 2
    return x * cos_t + pltpu.roll(x, LANES - half, 1) * sin_a + pltpu.roll(x, half, 1) * sin_b


def _pair_slabs(x):
    return x, pltpu.roll(x, NSA_HEAD_DIM, 1)


def _nsa_prep_kernel(q_ref, kv_ref, cos_ref, sa_ref, sb_ref, qo_ref, ka_ref, kw_ref, vs_ref, vw_ref,
                     *, q_scale, kvd):
    i = pl.program_id(1)
    ts = q_ref.shape[1]
    hd = NSA_HEAD_DIM
    cos_t, sin_a, sin_b = cos_ref[...], sa_ref[...], sb_ref[...]
    lane = lax.broadcasted_iota(jnp.int32, (ts, LANES), 1)
    low = lane < hd
    ones_lane = lane == hd
    for j in range(q_ref.shape[2] // LANES):
        xq = _rope(q_ref[0, :, j * LANES:(j + 1) * LANES].astype(F32), cos_t, sin_a, sin_b) * q_scale
        for u, slab in enumerate(_pair_slabs(xq)):
            qo_ref[0, 2 * j + u] = jnp.where(low, slab, 0.0).astype(BF16)
    blk = (i * ts + lax.broadcasted_iota(jnp.int32, (ts, LANES), 0)) // SEL_BLOCK
    onehot = (blk == lane - hd).astype(F32)
    slab_of = lambda c0, j: kv_ref[0, :, c0 + j * LANES:c0 + (j + 1) * LANES].astype(F32)
    for j in range(kvd // LANES):
        for u, slab in enumerate(_pair_slabs(_rope(slab_of(2 * kvd, j), cos_t, sin_a, sin_b))):
            ka_ref[0, 2 * j + u] = jnp.where(low, slab, onehot).astype(BF16)
        for u, slab in enumerate(_pair_slabs(_rope(slab_of(4 * kvd, j), cos_t, sin_a, sin_b))):
            kw_ref[0, 2 * j + u] = slab.astype(BF16)
        for u, slab in enumerate(_pair_slabs(slab_of(3 * kvd, j))):
            vs_ref[0, 2 * j + u] = jnp.where(ones_lane, 1.0, slab).astype(BF16)
        for u, slab in enumerate(_pair_slabs(slab_of(5 * kvd, j))):
            vw_ref[0, 2 * j + u] = jnp.where(ones_lane, 1.0, slab).astype(BF16)


def _nsa_prep(proj, col_q, col_kv, n_q, kvd, ts=1024):
    bsz, seq, _ = proj.shape
    hd = NSA_HEAD_DIM
    ts = min(ts, seq)
    n_heads, n_kv = n_q // hd, kvd // hd
    assert 2 * hd == LANES and seq // SEL_BLOCK <= LANES - hd
    cos_t, sin_a, sin_b = _rope_tables(np.arange(seq))
    tab = pl.BlockSpec((ts, LANES), lambda b, i: (i, 0))
    per_head = lambda n: pl.BlockSpec((1, n, ts, LANES), lambda b, i: (b, 0, i, 0))
    return pl.pallas_call(
        functools.partial(_nsa_prep_kernel, q_scale=hd ** -0.5, kvd=kvd),
        grid=(bsz, seq // ts),
        in_specs=[pl.BlockSpec((1, ts, n_q), lambda b, i: (b, i, col_q // n_q)),
                  pl.BlockSpec((1, ts, 6 * kvd), lambda b, i: (b, i, col_kv // (6 * kvd))),
                  tab, tab, tab],
        out_specs=[per_head(n_heads)] + [per_head(n_kv)] * 4,
        out_shape=[jax.ShapeDtypeStruct((bsz, n_heads, seq, LANES), BF16)]
        + [jax.ShapeDtypeStruct((bsz, n_kv, seq, LANES), BF16)] * 4,
        compiler_params=pltpu.CompilerParams(dimension_semantics=("parallel", "parallel")),
        name="nsa_prep",
    )(proj, proj, cos_t, sin_a, sin_b)


def _gelu_tanh(x):
    return 0.5 * x * (1.0 + jnp.tanh(math.sqrt(2.0 / math.pi) * (x + 0.044715 * (x * x * x))))


def _compress_kernel(x_ref, w1k_ref, w1v_ref, w2k_ref, w2v_ref, pek_ref, pev_ref,
                     cos_ref, sa_ref, sb_ref, ko_ref, vo_ref, x_s):
    seq = x_ref.shape[1]
    kvd = x_ref.shape[2] // 2
    n_rows = ko_ref.shape[2]
    n_pair = kvd // LANES
    for c in range(2 * n_pair):
        x_s[c, 0:seq, :] = x_ref[0, :, c * LANES:(c + 1) * LANES].astype(F32)
        x_s[c, seq:, :] = jnp.zeros((x_s.shape[1] - seq, LANES), F32)
    cos_t, sin_a, sin_b = cos_ref[...], sa_ref[...], sb_ref[...]

    def pair(c, w1_ref, w2_ref, pe_ref):
        taps = [x_s[c, pl.ds(p, n_rows, stride=CMP_STRIDE), :].astype(BF16) for p in range(CMP_LEN)]
        w1 = w1_ref[...]
        hid = _dot(jnp.concatenate(taps, axis=1), w1) + _dot(pe_ref[...], w1)[0:1]
        return _dot(_gelu_tanh(hid).astype(BF16), w2_ref[...])

    for j in range(n_pair):
        k = _rope(pair(j, w1k_ref, w2k_ref, pek_ref), cos_t, sin_a, sin_b)
        for u, slab in enumerate(_pair_slabs(k)):
            ko_ref[0, 2 * j + u] = slab.astype(BF16)
        for u, slab in enumerate(_pair_slabs(pair(n_pair + j, w1v_ref, w2v_ref, pev_ref))):
            vo_ref[0, 2 * j + u] = slab.astype(BF16)


def _compress(proj, col_kv, kvd, pe_k, pe_v, k_w1, k_w2, v_w1, v_w2):
    bsz, seq, _ = proj.shape
    hd = NSA_HEAD_DIM
    n_kv = kvd // hd
    assert seq % CMP_STRIDE == 0 and 2 * hd == LANES and col_kv % (2 * kvd) == 0
    n_rows = seq // CMP_STRIDE
    hidden = k_w1.shape[1]

    def pair_w1(w1):
        w = w1.reshape(CMP_LEN, 1, hd, 1, hidden) * jnp.eye(2, dtype=F32).reshape(1, 2, 1, 2, 1)
        return w.reshape(CMP_LEN * LANES, 2 * hidden).astype(BF16)

    pair_w2 = lambda w2: jnp.kron(jnp.eye(2, dtype=F32), w2).astype(BF16)
    pe_rows = lambda pe: jnp.broadcast_to(pe.reshape(1, CMP_LEN, 1, hd), (SUBLANES, CMP_LEN, 2, hd)).reshape(
        SUBLANES, CMP_LEN * LANES).astype(BF16)
    cos_t, sin_a, sin_b = _rope_tables(np.arange(n_rows) * CMP_STRIDE + CMP_LEN - 1)
    const = lambda shape: pl.BlockSpec(shape, lambda b: (0,) * len(shape))
    ospec = pl.BlockSpec((1, n_kv, n_rows, LANES), lambda b: (b, 0, 0, 0))
    return pl.pallas_call(
        _compress_kernel,
        grid=(bsz,),
        in_specs=[pl.BlockSpec((1, seq, 2 * kvd), lambda b: (b, 0, col_kv // (2 * kvd))),
                  const((CMP_LEN * LANES, 2 * hidden)), const((CMP_LEN * LANES, 2 * hidden)),
                  const((2 * hidden, LANES)), const((2 * hidden, LANES)),
                  const((SUBLANES, CMP_LEN * LANES)), const((SUBLANES, CMP_LEN * LANES)),
                  const((n_rows, LANES)), const((n_rows, LANES)), const((n_rows, LANES))],
        out_specs=[ospec, ospec],
        out_shape=[jax.ShapeDtypeStruct((bsz, n_kv, n_rows, LANES), BF16)] * 2,
        scratch_shapes=[pltpu.VMEM((2 * kvd // LANES, seq + CMP_LEN, LANES), F32)],
        compiler_params=pltpu.CompilerParams(dimension_semantics=("parallel",)),
        name="compress",
    )(proj, pair_w1(k_w1), pair_w1(v_w1), pair_w2(k_w2), pair_w2(v_w2), pe_rows(pe_k), pe_rows(pe_v),
      cos_t, sin_a, sin_b)


def _dot_tn(a, b):
    return lax.dot_general(a, b, (((0,), (0,)), ((), ())), preferred_element_type=F32)


def _nsa_kernel(q_ref, kc_ref, vc_ref, ka_ref, vs_ref, kw_ref, vw_ref, gl_ref, crel_ref, wrel_ref, o_ref,
                psum_s, imp_s):
    i = pl.program_id(2)
    gqa, qt = q_ref.shape[1:3]
    hd = NSA_HEAD_DIM
    cols = gqa * qt
    n_blk = imp_s.shape[0]
    n_rows = kc_ref.shape[2]
    band = WINDOW + qt
    qp = q_ref[0].reshape(cols, LANES)
    lane_q = lax.broadcasted_iota(jnp.int32, (1, cols), 1) % qt
    t0 = i * qt
    q0 = pl.multiple_of(t0, qt)
    w0 = pl.multiple_of(jnp.maximum(t0 - WINDOW, 0), qt)

    k_all = jnp.concatenate([kc_ref[0, 0], kw_ref[0, 0, pl.ds(w0, band), :]], axis=0)
    s_all = _dot_nt(k_all, qp)
    s_c, s_w = s_all[:n_rows], s_all[n_rows:]

    cmask = crel_ref[...] <= t0
    s_c = jnp.where(cmask, s_c, NEG_INF)
    e_c = jnp.exp(s_c - jnp.max(s_c, axis=0, keepdims=True))
    has_cmp = t0 + lane_q >= CMP_LEN - 1
    p_cmp = e_c * jnp.where(has_cmp, 1.0 / jnp.sum(e_c, axis=0, keepdims=True), 0.0)
    o_cmp = _dot_tn(vc_ref[0, 0], p_cmp.astype(BF16))[:hd]

    w_rel = wrel_ref[...]
    wmask = (w_rel <= t0 - w0) & (w_rel > t0 - w0 - WINDOW)
    s_w = jnp.where(wmask, s_w, NEG_INF)
    e_w = jnp.exp((s_w - jnp.max(s_w, axis=0, keepdims=True)).astype(BF16))
    pv_w = _dot_tn(vw_ref[0, 0, pl.ds(w0, band), :], e_w)
    o_win = pv_w[:hd] * (1.0 / pv_w[hd:hd + 1])

    p_sum = p_cmp[:, 0:qt]
    for r in range(1, gqa):
        p_sum = p_sum + p_cmp[:, r * qt:(r + 1) * qt]
    ratio = SEL_BLOCK // CMP_STRIDE
    imp_parts = []
    for c in range(qt // LANES):
        psum_s[c, 0:IMP_PAD, :] = jnp.zeros((IMP_PAD, LANES), F32)
        psum_s[c, IMP_PAD:, :] = p_sum[:, c * LANES:(c + 1) * LANES]
        part = psum_s[c, pl.ds(IMP_PAD - 1, n_blk, stride=ratio), :]
        for k in range(ratio):
            part = part + psum_s[c, pl.ds(IMP_PAD + k, n_blk, stride=ratio), :]
        imp_parts.append(part)
    imp = jnp.concatenate(imp_parts, axis=1)
    blk = lax.broadcasted_iota(jnp.int32, (n_blk, qt), 0)
    tq = t0 + lax.broadcasted_iota(jnp.int32, (n_blk, qt), 1)
    cur = tq // SEL_BLOCK
    forced = (blk == 0) | (blk == cur) | (blk == cur - 1)
    imp = jnp.where(forced, FORCED_SCORE, jnp.where(blk * SEL_BLOCK > tq, -1.0, imp))
    imp_s[...] = imp

    n_live = (t0 + qt - 1) // SEL_BLOCK + 1

    def rank_body(jj, cnt):
        for u in range(RANK_UNROLL):
            j = jj * RANK_UNROLL + u
            other = imp_s[pl.ds(j, 1), :]
            cnt = cnt + jnp.where(j < blk, (other >= imp).astype(F32), (other > imp).astype(F32))
        return cnt

    trips = jnp.where(n_live > N_SELECT, n_live // RANK_UNROLL, 0)
    rank = lax.fori_loop(0, trips, rank_body, jnp.zeros((n_blk, qt), F32))

    def with_bias(bias):
        pad_rows = [jnp.zeros((LANES - hd - n_blk, qt), F32)] if LANES - hd > n_blk else []
        bias_t = jnp.concatenate([jnp.zeros((hd, qt), F32), bias] + pad_rows, axis=0).T
        return (qp.astype(F32) + jnp.concatenate([bias_t] * gqa, axis=0)).astype(BF16)

    selected = rank < N_SELECT
    qa = with_bias(jnp.where(selected & (blk * SEL_BLOCK < t0), 0.0, NEG_INF))
    qa_own = with_bias(jnp.where(selected, 0.0, NEG_INF))
    s_d = _dot_nt(ka_ref[0, 0, pl.ds(q0, qt), :], qa_own)
    s_d = jnp.where(lax.broadcasted_iota(jnp.int32, (qt, cols), 0) <= lane_q, s_d, NEG_INF)
    m0 = jnp.max(s_d, axis=0, keepdims=True)
    pv_d = _dot_tn(vs_ref[0, 0, pl.ds(q0, qt), :], jnp.exp((s_d - m0).astype(BF16)))
    l0, acc0 = pv_d[hd:hd + 1], pv_d[:hd]

    def sel_body(c, carry):
        m, l, acc = carry
        k0 = pl.multiple_of(c * SEL_CHUNK, SEL_CHUNK)
        kk = ka_ref[0, 0, pl.ds(k0, SEL_CHUNK), :]
        vv = vs_ref[0, 0, pl.ds(k0, SEL_CHUNK), :]
        s = _dot_nt(kk, qa)
        pv = _dot_tn(vv, jnp.exp((s - m).astype(BF16)))
        m_new = jnp.maximum(m, jnp.max(s, axis=0, keepdims=True))
        alpha = jnp.exp(m - m_new)
        l_lazy = alpha * (l + pv[hd:hd + 1])
        acc_lazy = alpha * (acc + pv[:hd])
        safe = jnp.max(m_new - m) <= LAZY_MAX_SLACK

        def exact():
            pv2 = _dot_tn(vv, jnp.exp((_dot_nt(kk, qa) - m_new).astype(BF16)))
            return alpha * l + pv2[hd:hd + 1], alpha * acc + pv2[:hd]

        l, acc = lax.cond(safe, lambda: (l_lazy, acc_lazy), exact)
        return m_new, l, acc

    n_chunks = (t0 + SEL_CHUNK - 1) // SEL_CHUNK
    _, l_sel, acc_sel = lax.fori_loop(0, n_chunks, sel_body, (m0, l0, acc0))
    o_sel = acc_sel / l_sel

    gate = jax.nn.sigmoid(gl_ref[0, 0, 0].astype(F32))
    out = gate[0:1] * o_cmp + gate[1:2] * o_sel + gate[2:3] * o_win
    pairs = [jnp.concatenate([out[:, r * qt:(r + 1) * qt], out[:, (r + 1) * qt:(r + 2) * qt]], axis=0).T
             for r in range(0, gqa, 2)]
    o_ref[0] = jnp.concatenate(pairs, axis=1).astype(BF16)


def _nsa_attention(q_hm, kc, vc, ka, vs, kw, vw, gl):
    bsz, n_heads, seq, _ = q_hm.shape
    hd = NSA_HEAD_DIM
    n_kv = ka.shape[1]
    gqa = n_heads // n_kv
    n_rows = kc.shape[2]
    n_cmp = (seq - CMP_LEN) // CMP_STRIDE + 1
    nq = seq // Q_TILE
    n_blk = seq // SEL_BLOCK
    assert N_SELECT <= n_blk <= LANES - hd and Q_TILE % (RANK_UNROLL * SEL_BLOCK) == 0 and Q_TILE % LANES == 0
    assert LANES == 2 * hd and gqa % 2 == 0
    assert seq % SEL_CHUNK == 0 and seq >= WINDOW + Q_TILE
    assert n_rows * CMP_STRIDE == seq and n_blk * (SEL_BLOCK // CMP_STRIDE) <= n_rows and CMP_LEN == 2 * CMP_STRIDE
    cols = gqa * Q_TILE
    gl2 = gl.reshape(bsz, nq, Q_TILE, n_kv, gqa, 3).transpose(0, 3, 1, 5, 4, 2).reshape(bsz, n_kv, nq, 3, cols)
    lane_q = np.arange(cols)[None, :] % Q_TILE
    n_idx = np.arange(n_rows)[:, None]
    crel = np.where(n_idx < n_cmp, n_idx * CMP_STRIDE + (CMP_LEN - 1) - lane_q, np.iinfo(np.int32).max).astype(np.int32)
    wrel = (np.arange(WINDOW + Q_TILE)[:, None] - lane_q).astype(np.int32)
    const = lambda a: pl.BlockSpec(a.shape, lambda b, g, i: (0, 0))
    kv_spec = lambda n: pl.BlockSpec((1, 1, n, LANES), lambda b, g, i: (b, g, 0, 0))
    return pl.pallas_call(
        _nsa_kernel,
        grid=(bsz, n_kv, nq),
        in_specs=[pl.BlockSpec((1, gqa, Q_TILE, LANES), lambda b, g, i: (b, g, i, 0)),
                  kv_spec(n_rows), kv_spec(n_rows), kv_spec(seq), kv_spec(seq), kv_spec(seq), kv_spec(seq),
                  pl.BlockSpec((1, 1, 1, 3, cols), lambda b, g, i: (b, g, i, 0, 0)),
                  const(crel), const(wrel)],
        out_specs=pl.BlockSpec((1, Q_TILE, gqa * hd), lambda b, g, i: (b, i, g)),
        out_shape=jax.ShapeDtypeStruct((bsz, seq, n_heads * hd), BF16),
        scratch_shapes=[pltpu.VMEM((Q_TILE // LANES, IMP_PAD + n_rows, LANES), F32),
                        pltpu.VMEM((n_blk, Q_TILE), F32)],
        compiler_params=pltpu.CompilerParams(dimension_semantics=("parallel", "parallel", "arbitrary")),
        name="nsa_attention",
    )(q_hm, kc, vc, ka, vs, kw, vw, gl2, jnp.asarray(crel), jnp.asarray(wrel))


def _merge_kernel(h_ref, mod_ref, ya_ref, yb_ref, yc_ref, gl_ref, wa_ref, wb_ref, wc_ref, wo_ref,
                  lng_ref, lnb_ref, o_ref, *, alpha):
    d = h_ref.shape[2]
    gate = lambda k: jax.nn.sigmoid(gl_ref[0, :, k * d:(k + 1) * d].astype(F32))
    merged = gate(0) * _dot(ya_ref[0], wa_ref[...])
    merged = merged + gate(1) * _dot(yb_ref[0], wb_ref[...])
    merged = merged + gate(2) * _dot(yc_ref[0], wc_ref[...])
    y = _dot(merged.astype(BF16), wo_ref[...])
    z = alpha * h_ref[0] + mod_ref[0, 2:3, :] * y
    o_ref[0] = _layer_norm(z, lng_ref[...], lnb_ref[...])


def _merge(h, mod3, ya, yb, yc, proj, col_gate, wa, wb, wc, wo, ln_g, ln_b, alpha, tm=512):
    bsz, seq, d = h.shape
    tm = min(tm, seq)
    act = pl.BlockSpec((1, tm, d), lambda b, i: (b, i, 0))
    wspec = pl.BlockSpec((d, d), lambda b, i: (0, 0))
    vec = pl.BlockSpec((1, d), lambda b, i: (0, 0))
    return pl.pallas_call(
        functools.partial(_merge_kernel, alpha=alpha),
        grid=(bsz, seq // tm),
        in_specs=[act, pl.BlockSpec((1, 3, d), lambda b, i: (b, 0, 0)), act, act, act,
                  pl.BlockSpec((1, tm, 3 * d), lambda b, i: (b, i, col_gate // (3 * d))),
                  wspec, wspec, wspec, wspec, vec, vec],
        out_specs=act,
        out_shape=jax.ShapeDtypeStruct((bsz, seq, d), F32),
        compiler_params=pltpu.CompilerParams(dimension_semantics=("parallel", "parallel")),
        name="merge",
    )(h, mod3, ya, yb, yc, proj, wa, wb, wc, wo, ln_g.reshape(1, d), ln_b.reshape(1, d))


def _token_mixer(h, mod3, w_in, conv_a_w, conv_a_b, norm_a_g, norm_a_b, w_a_out,
                 ssm_conv_w, ssm_conv_b, ssm_dt_bias, ssm_a_log, ssm_d, ssm_norm_w, w_b_out,
                 cmp_pe_k, cmp_pe_v, cmp_k_w1, cmp_k_w2, cmp_v_w1, cmp_v_w2, w_c_out, w_o, ln_g, ln_b, alpha):
    bsz, seq, d = h.shape
    ch = conv_a_w.shape[1]
    n_ssm_heads = ssm_dt_bias.shape[0]
    di = n_ssm_heads * SSM_HEAD_DIM
    gn = SSM_GROUPS * SSM_STATE
    n_q = w_c_out.shape[0]
    n_heads = n_q // NSA_HEAD_DIM
    kvd = NSA_KV_HEADS * NSA_HEAD_DIM
    gqa = n_heads // NSA_KV_HEADS
    n_gate = 3 * n_heads
    o_dt = 2 * ch + 2 * di + 2 * gn
    o_q = o_dt + n_ssm_heads
    o_gl = o_q + n_q + 6 * kvd
    o_g = o_gl + n_gate
    assert w_in.shape[1] == o_g + 3 * d
    col_z, col_xs, col_bc = 2 * ch, 2 * ch + di, 2 * ch + 2 * di
    col_q = o_dt
    col_gate = col_q + n_q
    col_kv = col_gate + 3 * d
    col_small = col_kv + 6 * kvd
    assert n_ssm_heads + n_gate <= LANES
    w_perm = jnp.concatenate(
        [w_in[:, :o_dt], w_in[:, o_q:o_q + n_q], w_in[:, o_g:], w_in[:, o_q + n_q:o_gl], w_in[:, o_dt:o_q],
         w_in[:, o_gl:o_g], jnp.zeros((d, LANES - n_ssm_heads - n_gate), F32)], axis=1).astype(BF16)

    proj = _inproj(h, mod3, w_perm)

    y_a = _conformer(proj, conv_a_w, conv_a_b, norm_a_g, norm_a_b)
    y_b = _ssd(proj, col_z, col_xs, col_bc, col_small, ssm_conv_w, ssm_conv_b, ssm_dt_bias, ssm_a_log,
               ssm_d, ssm_norm_w)

    q_hm, ka, kw, vs, vw = _nsa_prep(proj, col_q, col_kv, n_q, kvd)
    k_cmp, v_cmp = _compress(proj, col_kv, kvd, cmp_pe_k, cmp_pe_v, cmp_k_w1, cmp_k_w2, cmp_v_w1, cmp_v_w2)
    gl = proj[:, :, col_small + n_ssm_heads:col_small + n_ssm_heads + n_gate]
    y_c = _nsa_attention(q_hm, k_cmp, v_cmp, ka, vs, kw, vw, gl)

    return _merge(h, mod3, y_a, y_b, y_c, proj, col_gate, w_a_out.astype(BF16), w_b_out.astype(BF16),
                  w_c_out.astype(BF16), w_o.astype(BF16), ln_g, ln_b, alpha)


def kernel(x, c, ada_w, ada_b, ln_g, ln_b, ffn_w_gate, ffn_w_up, ffn_w_down, w_in, conv_a_w, conv_a_b, norm_a_g, norm_a_b, w_a_out, ssm_conv_w, ssm_conv_b, ssm_dt_bias, ssm_a_log, ssm_d, ssm_norm_w, w_b_out, cmp_pe_k, cmp_pe_v, cmp_k_w1, cmp_k_w2, cmp_v_w1, cmp_v_w2, w_c_out, w_o):
    depth = ada_w.shape[0]
    alpha = (2.0 * depth) ** 0.25
    mod = _ada_mod(c, ada_w, ada_b)
    wg, wu, wd = ffn_w_gate.astype(BF16), ffn_w_up.astype(BF16), ffn_w_down.astype(BF16)
    h = x
    for l in range(depth):
        h = _ffn_block(h, mod[l, :, 0:3], wg, wu, wd, l, 0, ln_g[l, 0], ln_b[l, 0], alpha)
        h = _token_mixer(h, mod[l, :, 3:6], w_in[l], conv_a_w[l], conv_a_b[l], norm_a_g[l], norm_a_b[l], w_a_out[l],
                         ssm_conv_w[l], ssm_conv_b[l], ssm_dt_bias[l], ssm_a_log[l], ssm_d[l], ssm_norm_w[l],
                         w_b_out[l], cmp_pe_k[l], cmp_pe_v[l], cmp_k_w1[l], cmp_k_w2[l], cmp_v_w1[l], cmp_v_w2[l],
                         w_c_out[l], w_o[l], ln_g[l, 1], ln_b[l, 1], alpha)
        h = _ffn_block(h, mod[l, :, 6:9], wg, wu, wd, l, 1, ln_g[l, 2], ln_b[l, 2], alpha)
    return h
```
